```python
import math
import jax, jax.numpy as jnp
from jax import lax
import numpy as np

D_MODEL = 1024
BATCH = 8
SEQ = 4096
DEPTH = 1

N_META = 16
H_A = 8
DH_A = 64
W_A = H_A * DH_A
H_IDX = 8
D_IDX = 64
TOPK_MAX = 256
Q_BLOCK = 64
N_BUCKETS = 32
MAX_DISTANCE = 128
H_B = 8
DK_B = 128
DV_B = 128
W_B = H_B * DV_B
CONV_K = 4
CHUNK = 64
EPS = 1e-6

COLUMN_SIZES = (W_A, W_A, W_A, W_A, H_IDX * D_IDX, D_IDX, H_IDX,
                H_B * DK_B, H_B * DK_B, W_B, W_B, H_B, H_B, D_MODEL, D_MODEL)
N_IN = 4 * W_A + H_IDX * D_IDX + D_IDX + H_IDX + 2 * H_B * DK_B + 2 * W_B + 2 * H_B + 2 * D_MODEL

kernel_name = 'hybrid_dsa_gdn_gated_merge'


def rms_norm(x, w):
    xf = x.astype(jnp.float32)
    y = xf * lax.rsqrt(jnp.mean(xf * xf, axis=-1, keepdims=True) + EPS)
    return (y * w.astype(jnp.float32)).astype(x.dtype)


def l2_norm(x):
    xf = x.astype(jnp.float32)
    return xf * lax.rsqrt(jnp.sum(xf * xf, axis=-1, keepdims=True) + EPS)


def split_columns(p):
    offsets = []
    acc = 0
    for s in COLUMN_SIZES[:-1]:
        acc += s
        offsets.append(acc)
    return jnp.split(p, offsets, axis=-1)


def t5_bucket(dist):
    max_exact = N_BUCKETS // 2
    d = jnp.maximum(dist, 0)
    ratio = jnp.maximum(d, max_exact).astype(jnp.float32) / max_exact
    large = max_exact + (jnp.log(ratio) / math.log(MAX_DISTANCE / max_exact)
                         * (N_BUCKETS - max_exact)).astype(jnp.int32)
    large = jnp.minimum(large, N_BUCKETS - 1)
    return jnp.where(d < max_exact, d, large)


def sparse_attention(q, k, v, q_idx, k_idx, w_idx, rel_bias, q_norm_w, k_norm_w):
    B, T = q.shape[0], q.shape[1]
    topk = min(TOPK_MAX, SEQ // 4)
    q = rms_norm(q, q_norm_w)
    k = rms_norm(k, k_norm_w)
    n_blk = -(-T // Q_BLOCK)
    pad = n_blk * Q_BLOCK - T

    def to_blocks(a):
        a = jnp.pad(a, [(0, 0), (0, pad)] + [(0, 0)] * (a.ndim - 2))
        return a.reshape((B, n_blk, Q_BLOCK) + a.shape[2:]).swapaxes(0, 1)

    key_pos = jnp.arange(T, dtype=jnp.int32)
    idx_scale = (H_IDX ** -0.5) * (D_IDX ** -0.5)
    attn_scale = DH_A ** -0.5

    def block(args):
        blk, q_b, qi_b, wi_b = args
        q_pos = blk * Q_BLOCK + jnp.arange(Q_BLOCK, dtype=jnp.int32)
        causal = key_pos[None, :] <= q_pos[:, None]
        rel = jax.nn.relu(jnp.einsum('bqhd,bsd->bqhs', qi_b, k_idx).astype(jnp.float32))
        score = jnp.einsum('bqhs,bqh->bqs', rel, wi_b.astype(jnp.float32)) * idx_scale
        score = jnp.where(causal[None], score, -jnp.inf)
        _, sel = lax.top_k(score, topk)
        valid = sel <= q_pos[None, :, None]
        k_sel = jax.vmap(lambda kb, ib: kb[ib])(k, sel)
        v_sel = jax.vmap(lambda vb, ib: vb[ib])(v, sel)
        logits = jnp.einsum('bqhd,bqkhd->bhqk', q_b, k_sel).astype(jnp.float32) * attn_scale
        bias = rel_bias[t5_bucket(q_pos[None, :, None] - sel)]
        logits = logits + jnp.transpose(bias, (0, 3, 1, 2)).astype(jnp.float32)
        logits = jnp.where(valid[:, None], logits, -jnp.inf)
        p = jax.nn.softmax(logits, axis=-1)
        return jnp.einsum('bhqk,bqkhd->bqhd', p.astype(v.dtype), v_sel)

    out = lax.map(block, (jnp.arange(n_blk, dtype=jnp.int32), to_blocks(q), to_blocks(q_idx), to_blocks(w_idx)))
    out = jnp.transpose(out, (1, 0, 2, 3, 4)).reshape(B, n_blk * Q_BLOCK, H_A, DH_A)
    return out[:, :T]


def causal_depthwise_conv_silu(x, w):
    y = lax.conv_general_dilated(x, w[:, None, :].astype(x.dtype), window_strides=(1,),
                                 padding=[(CONV_K - 1, 0)],
                                 dimension_numbers=('NWC', 'WIO', 'NWC'),
                                 feature_group_count=x.shape[-1])
    return jax.nn.silu(y)


def gated_delta_rule(q, k, v, g, beta):
    B, T, H, dk = k.shape
    dv = v.shape[-1]
    pad = (-T) % CHUNK

    def lpad(a):
        return jnp.pad(a, [(0, 0), (pad, 0)] + [(0, 0)] * (a.ndim - 2))

    N = (T + pad) // CHUNK

    def chunkify(a):
        return lpad(a).reshape(B, N, CHUNK, H, -1).transpose(0, 3, 1, 2, 4)

    def chunkify_s(a):
        return lpad(a).reshape(B, N, CHUNK, H).transpose(0, 3, 1, 2)

    q = chunkify(q) * (dk ** -0.5)
    k = chunkify(k)
    v = chunkify(v)
    g = chunkify_s(g)
    beta = chunkify_s(beta)
    gc = jnp.cumsum(g, axis=-1)
    incl = jnp.tril(jnp.ones((CHUNK, CHUNK), dtype=bool))
    strict = jnp.tril(jnp.ones((CHUNK, CHUNK), dtype=bool), -1)
    L = jnp.exp(jnp.where(incl, gc[..., :, None] - gc[..., None, :], -jnp.inf))
    kb = k * beta[..., None]
    A = jnp.where(strict, jnp.einsum('bhnid,bhnjd->bhnij', kb, k) * L, 0.0)
    eye = jnp.eye(CHUNK, dtype=A.dtype)
    Tm = lax.linalg.triangular_solve(eye + A, jnp.broadcast_to(eye, A.shape),
                                     left_side=True, lower=True, unit_diagonal=True)
    u = jnp.einsum('bhnij,bhnje->bhnie', Tm, v * beta[..., None])
    w = jnp.einsum('bhnij,bhnjd->bhnid', Tm, kb * jnp.exp(gc)[..., None])
    att = jnp.einsum('bhnid,bhnjd->bhnij', q, k) * L
    q_dec = q * jnp.exp(gc)[..., None]
    k_dec = k * jnp.exp(gc[..., -1:] - gc)[..., None]
    g_last = jnp.exp(gc[..., -1])

    def step(S, xs):
        u_n, w_n, qd_n, kd_n, att_n, gl_n = xs
        v_new = u_n - jnp.einsum('bhcd,bhde->bhce', w_n, S)
        o = jnp.einsum('bhcd,bhde->bhce', qd_n, S) + jnp.einsum('bhij,bhje->bhie', att_n, v_new)
        S = S * gl_n[..., None, None] + jnp.einsum('bhcd,bhce->bhde', kd_n, v_new)
        return S, o

    def n_first(a):
        return jnp.moveaxis(a, 2, 0)

    S0 = jnp.zeros((B, H, dk, dv), dtype=jnp.float32)
    _, o = lax.scan(step, S0, (n_first(u), n_first(w), n_first(q_dec), n_first(k_dec),
                               n_first(att), n_first(g_last)))
    o = jnp.transpose(o, (1, 0, 3, 2, 4)).reshape(B, N * CHUNK, H, dv)
    return o[:, pad:]


def hybrid_layer(x, rel_bias, norm_w, w_in, q_norm_w, k_norm_w, conv_w, a_log, dt_bias,
                 gdn_norm_w, w_proj_a, w_proj_b, w_out):
    B, T, _ = x.shape
    h = rms_norm(x, norm_w)
    p = h @ w_in.astype(x.dtype)
    (qa, ka, va, za, qi, ki, wi, qb, kb, vb, zb, bb, ab, ga, gb) = split_columns(p)

    ya = sparse_attention(qa.reshape(B, T, H_A, DH_A), ka.reshape(B, T, H_A, DH_A),
                          va.reshape(B, T, H_A, DH_A), qi.reshape(B, T, H_IDX, D_IDX), ki, wi,
                          rel_bias, q_norm_w, k_norm_w)
    ya = (ya.reshape(B, T, W_A) * jax.nn.silu(za)) @ w_proj_a.astype(x.dtype)

    qkv = causal_depthwise_conv_silu(jnp.concatenate([qb, kb, vb], axis=-1), conv_w)
    qg, kg, vg = jnp.split(qkv, [H_B * DK_B, 2 * H_B * DK_B], axis=-1)
    q_b = l2_norm(qg.reshape(B, T, H_B, DK_B))
    k_b = l2_norm(kg.reshape(B, T, H_B, DK_B))
    v_b = vg.reshape(B, T, H_B, DV_B).astype(jnp.float32)
    beta = jax.nn.sigmoid(bb.astype(jnp.float32))
    g = -jnp.exp(a_log.astype(jnp.float32)) * jax.nn.softplus(ab.astype(jnp.float32) + dt_bias.astype(jnp.float32))
    o_b = gated_delta_rule(q_b, k_b, v_b, g, beta)
    o_b = rms_norm(o_b, gdn_norm_w).reshape(B, T, W_B).astype(x.dtype)
    yb = (o_b * jax.nn.silu(zb)) @ w_proj_b.astype(x.dtype)

    m = jax.nn.sigmoid(ga) * ya + jax.nn.sigmoid(gb) * yb
    return x + m @ w_out.astype(x.dtype)


def setup_inputs(seed: int = 0) -> dict:
    key = jax.random.key(seed)
    ks = jax.random.split(key, 16)
    f32 = jnp.float32
    x = jax.random.normal(ks[0], (BATCH, SEQ, D_MODEL), f32)
    meta_tokens = jax.random.normal(ks[1], (N_META, D_MODEL), f32)
    rel_bias = 0.5 * jax.random.normal(ks[2], (N_BUCKETS, H_A), f32)
    norm_w = 1.0 + 0.02 * jax.random.normal(ks[3], (DEPTH, D_MODEL), f32)
    w_in = jax.random.normal(ks[4], (DEPTH, D_MODEL, N_IN), f32) * (D_MODEL ** -0.5)
    q_norm_w = 1.0 + 0.02 * jax.random.normal(ks[5], (DEPTH, DH_A), f32)
    k_norm_w = 1.0 + 0.02 * jax.random.normal(ks[6], (DEPTH, DH_A), f32)
    conv_w = jax.random.normal(ks[7], (DEPTH, CONV_K, 2 * H_B * DK_B + W_B), f32) * (CONV_K ** -0.5)
    a_log = jnp.log(jax.random.uniform(ks[8], (DEPTH, H_B), f32, 1.0, 16.0))
    dt = jnp.exp(jax.random.uniform(ks[9], (DEPTH, H_B), f32, math.log(1e-3), math.log(1e-1)))
    dt_bias = dt + jnp.log(-jnp.expm1(-dt))
    gdn_norm_w = 1.0 + 0.02 * jax.random.normal(ks[10], (DEPTH, DV_B), f32)
    w_proj_a = jax.random.normal(ks[11], (DEPTH, W_A, D_MODEL), f32) * (W_A ** -0.5)
    w_proj_b = jax.random.normal(ks[12], (DEPTH, W_B, D_MODEL), f32) * (W_B ** -0.5)
    w_out = jax.random.normal(ks[13], (DEPTH, D_MODEL, D_MODEL), f32) * (D_MODEL ** -0.5)
    return {'x': x, 'meta_tokens': meta_tokens, 'rel_bias': rel_bias, 'norm_w': norm_w,
            'w_in': w_in, 'q_norm_w': q_norm_w, 'k_norm_w': k_norm_w, 'conv_w': conv_w,
            'a_log': a_log, 'dt_bias': dt_bias, 'gdn_norm_w': gdn_norm_w,
            'w_proj_a': w_proj_a, 'w_proj_b': w_proj_b, 'w_out': w_out}


def reference(x, meta_tokens, rel_bias, norm_w, w_in, q_norm_w, k_norm_w, conv_w, a_log, dt_bias,
              gdn_norm_w, w_proj_a, w_proj_b, w_out):
    B = x.shape[0]
    meta = jnp.broadcast_to(meta_tokens[None].astype(x.dtype), (B, N_META, D_MODEL))
    hs = jnp.concatenate([meta, x], axis=1)
    for l in range(DEPTH):
        hs = hybrid_layer(hs, rel_bias, norm_w[l], w_in[l], q_norm_w[l], k_norm_w[l], conv_w[l],
                          a_log[l], dt_bias[l], gdn_norm_w[l], w_proj_a[l], w_proj_b[l], w_out[l])
    return hs[:, N_META:]
```

```python
import functools
import math

import jax
import jax.numpy as jnp
from jax import lax
from jax.experimental import pallas as pl
from jax.experimental.pallas import tpu as pltpu

F32 = jnp.float32
BF16 = jnp.bfloat16
I32 = jnp.int32

D_MODEL = 1024
N_META = 16
H_A = 8
DH_A = 64
W_A = H_A * DH_A
H_IDX = 8
D_IDX = 64
TOPK = 256
N_BUCKETS = 32
MAX_DISTANCE = 128
H_B = 8
DK_B = 128
DV_B = 128
W_B = H_B * DV_B
CONV_K = 4
CHUNK = 64
EPS = 1e-6

C_QKV_B = 0
C_ZB = 3 * 1024
C_GA = 4 * 1024
C_GB = 5 * 1024
C_QA = 6 * 1024
C_KA = C_QA + 512
C_VA = C_KA + 512
C_ZA = C_VA + 512
C_QI = C_ZA + 512
C_SMALL = C_QI + 512
N_P = C_SMALL + 128
S_KI, S_WI, S_BB, S_AB = 0, 64, 72, 80

META_BLK = 128
META_PAD = META_BLK - N_META
QB = 256
KC = 256
NEAR = QB + 128
INT_MIN = -2 ** 31
NEG = -1e30
VMEM_LIMIT = 56 * 1024 * 1024
HI = lax.Precision.HIGHEST


def _cparams(sem):
    return pltpu.CompilerParams(dimension_semantics=sem, vmem_limit_bytes=VMEM_LIMIT)


def _inproj_kernel(x_ref, nw_ref, w_ref, o_ref):
    x = x_ref[...]
    ms = jnp.mean(x * x, axis=-1, keepdims=True)
    h = (x * lax.rsqrt(ms + EPS) * nw_ref[...]).astype(BF16)
    o_ref[...] = jnp.dot(h, w_ref[...], preferred_element_type=F32)


def _inproj(x2d, norm_w_row, w_bf16, tm):
    m = x2d.shape[0]
    tn = N_P // 3
    return pl.pallas_call(
        _inproj_kernel,
        grid=(3, m // tm),
        in_specs=[pl.BlockSpec((tm, D_MODEL), lambda j, i: (i, 0)),
                  pl.BlockSpec((1, D_MODEL), lambda j, i: (0, 0)),
                  pl.BlockSpec((D_MODEL, tn), lambda j, i: (0, j))],
        out_specs=pl.BlockSpec((tm, tn), lambda j, i: (i, j)),
        out_shape=jax.ShapeDtypeStruct((m, N_P), F32),
        compiler_params=_cparams(("arbitrary", "arbitrary")),
        name="inproj",
    )(x2d, norm_w_row, w_bf16)


def _head_rms(x, bd, w_row):
    sq = x * x
    hi = sq.astype(BF16)
    lo = (sq - hi.astype(F32)).astype(BF16)
    ss = (jnp.dot(hi, bd, preferred_element_type=F32) + jnp.dot(lo, bd, preferred_element_type=F32))
    return x * lax.rsqrt(ss * (1.0 / DH_A) + EPS) * w_row


def _prep_k_kernel(kx, vx, sx, km, vm, sm, knw, bd, k_o, vt_o, ki_o):
    is_meta = pl.program_id(1) == 0
    k = jnp.where(is_meta, km[...], kx[...])
    v = jnp.where(is_meta, vm[...], vx[...])
    s = jnp.where(is_meta, sm[...], sx[...])
    k_o[...] = _head_rms(k, bd[...], knw[...]).astype(BF16)
    for j in range(W_A // 128):
        vt_o[j * 128:(j + 1) * 128, :] = v[:, j * 128:(j + 1) * 128].T.astype(BF16)
    ki_o[...] = s[:, S_KI:S_KI + D_IDX].astype(BF16)


def _prep_k(p_x, p_m, knw_row, bd):
    b, t, _ = p_x.shape
    nblk = t // META_BLK + 1
    tk = nblk * META_BLK

    def xmap(col):
        return lambda bi, i: (bi, jnp.maximum(i - 1, 0), col)

    return pl.pallas_call(
        _prep_k_kernel,
        grid=(b, nblk),
        in_specs=[pl.BlockSpec((None, META_BLK, 512), xmap(C_KA // 512)),
                  pl.BlockSpec((None, META_BLK, 512), xmap(C_VA // 512)),
                  pl.BlockSpec((None, META_BLK, 128), xmap(C_SMALL // 128)),
                  pl.BlockSpec((META_BLK, 512), lambda bi, i: (0, C_KA // 512)),
                  pl.BlockSpec((META_BLK, 512), lambda bi, i: (0, C_VA // 512)),
                  pl.BlockSpec((META_BLK, 128), lambda bi, i: (0, C_SMALL // 128)),
                  pl.BlockSpec((1, 512), lambda bi, i: (0, 0)),
                  pl.BlockSpec((512, 512), lambda bi, i: (0, 0))],
        out_specs=[pl.BlockSpec((None, META_BLK, 512), lambda bi, i: (bi, i, 0)),
                   pl.BlockSpec((None, 512, META_BLK), lambda bi, i: (bi, 0, i)),
                   pl.BlockSpec((None, META_BLK, D_IDX), lambda bi, i: (bi, i, 0))],
        out_shape=[jax.ShapeDtypeStruct((b, tk, 512), BF16),
                   jax.ShapeDtypeStruct((b, 512, tk), BF16),
                   jax.ShapeDtypeStruct((b, tk, D_IDX), BF16)],
        compiler_params=_cparams(("arbitrary", "arbitrary")),
        name="prep_k",
    )(p_x, p_x, p_x, p_m, p_m, p_m, knw_row, bd)


def _prep_q_kernel(qx, qix, sx, qnw, bd, qtz_o, qit_o, wt_o):
    qn = _head_rms(qx[...], bd[...], qnw[...]) * (DH_A ** -0.5)
    qi = qix[...]
    zeros = jnp.zeros((64, QB), BF16)
    for j in range(W_A // 128):
        t = qn[:, j * 128:(j + 1) * 128].T.astype(BF16)
        base = 2 * j * 128
        qtz_o[base:base + 64, :] = t[0:64]
        qtz_o[base + 64:base + 128, :] = zeros
        qtz_o[base + 128:base + 192, :] = zeros
        qtz_o[base + 192:base + 256, :] = t[64:128]
        qit_o[j * 128:(j + 1) * 128, :] = qi[:, j * 128:(j + 1) * 128].T.astype(BF16)
    st = sx[...].T
    wt_o[...] = st[S_WI:S_WI + H_IDX, :] * ((H_IDX ** -0.5) * (D_IDX ** -0.5))


def _prep_q(p_x, qnw_row, bd):
    b, t, _ = p_x.shape
    return pl.pallas_call(
        _prep_q_kernel,
        grid=(b, t // QB),
        in_specs=[pl.BlockSpec((None, QB, 512), lambda bi, i: (bi, i, C_QA // 512)),
                  pl.BlockSpec((None, QB, 512), lambda bi, i: (bi, i, C_QI // 512)),
                  pl.BlockSpec((None, QB, 128), lambda bi, i: (bi, i, C_SMALL // 128)),
                  pl.BlockSpec((1, 512), lambda bi, i: (0, 0)),
                  pl.BlockSpec((512, 512), lambda bi, i: (0, 0))],
        out_specs=[pl.BlockSpec((None, 2 * 512, QB), lambda bi, i: (bi, 0, i)),
                   pl.BlockSpec((None, 512, QB), lambda bi, i: (bi, 0, i)),
                   pl.BlockSpec((None, H_IDX, QB), lambda bi, i: (bi, 0, i))],
        out_shape=[jax.ShapeDtypeStruct((b, 2 * 512, t), BF16),
                   jax.ShapeDtypeStruct((b, 512, t), BF16),
                   jax.ShapeDtypeStruct((b, H_IDX, t), F32)],
        compiler_params=_cparams(("arbitrary", "arbitrary")),
        name="prep_q",
    )(p_x, p_x, p_x, qnw_row, bd)


def _bias_kernel(rb_ref, o_ref):
    a = lax.broadcasted_iota(I32, (NEAR, QB), 1)
    bk = lax.broadcasted_iota(I32, (NEAR, QB), 0)
    d = jnp.maximum(a - bk + (NEAR - QB), 0)
    max_exact = N_BUCKETS // 2
    ratio = jnp.maximum(d, max_exact).astype(F32) / max_exact
    large = max_exact + (jnp.log(ratio) / math.log(MAX_DISTANCE / max_exact)
                         * (N_BUCKETS - max_exact)).astype(I32)
    large = jnp.minimum(large, N_BUCKETS - 1)
    bucket = jnp.where(d < max_exact, d, large)
    for h in range(H_A):
        acc = jnp.zeros((NEAR, QB), F32)
        for bb in range(N_BUCKETS):
            acc = jnp.where(bucket == bb, rb_ref[bb, h], acc)
        o_ref[h] = acc - rb_ref[N_BUCKETS - 1, h]


def _bias_tile(rel_bias):
    return pl.pallas_call(
        _bias_kernel,
        in_specs=[pl.BlockSpec(memory_space=pltpu.SMEM)],
        out_specs=pl.BlockSpec((H_A, NEAR, QB), lambda: (0, 0, 0)),
        out_shape=jax.ShapeDtypeStruct((H_A, NEAR, QB), F32),
        name="bias_tile",
    )(rel_bias)


def _attn_kernel(qtz_ref, qit_ref, wt_ref, k_ref, vt_ref, ki_ref, bias_ref, o_ref,
                 keys_ref, m_ref, l_ref, acc_ref, j_ref):
    iq = pl.program_id(1)
    near0 = pl.multiple_of(iq * QB, QB)
    q_pos = META_BLK + iq * QB + lax.broadcasted_iota(I32, (1, QB), 1)

    def score_chunk(start, size):
        kic = ki_ref[pl.ds(start, size), :]
        acc = jnp.zeros((size, QB), F32)
        for h in range(H_IDX):
            r = jnp.dot(kic, qit_ref[h * D_IDX:(h + 1) * D_IDX, :], preferred_element_type=F32)
            acc = acc + jnp.maximum(r, 0.0) * wt_ref[h:h + 1, :]
        acc = acc + 0.0
        bits = pltpu.bitcast(acc, I32)
        key = jnp.where(bits < 0, bits ^ jnp.int32(0x7FFFFFFF), bits)
        pos = start + lax.broadcasted_iota(I32, (size, QB), 0)
        key = jnp.where(pos >= META_PAD, jnp.where(pos <= q_pos, key, INT_MIN), INT_MIN)
        keys_ref[pl.ds(start, size), :] = key

    def far_scores(c, carry):
        score_chunk(pl.multiple_of(c * KC, KC), KC)
        return carry

    lax.fori_loop(0, iq, far_scores, 0)
    score_chunk(near0, NEAR)

    n128 = 2 * iq + NEAR // 128

    def count(pred):
        def body(c, acc):
            start = pl.multiple_of(c * 128, 128)
            blk = keys_ref[pl.ds(start, 128), :]
            return acc + jnp.sum(pred(blk, start).reshape(16, 8, QB), axis=0)
        acc = lax.fori_loop(0, n128, body, jnp.zeros((8, QB), I32))
        return jnp.sum(acc, axis=0, keepdims=True)

    def bit_step(it, tau):
        cand = tau + lax.shift_left(jnp.int32(1), 31 - it)
        cnt = count(lambda blk, start: jnp.where(blk >= cand, 1, 0))
        return jnp.where(cnt >= TOPK, cand, tau)

    tau = lax.fori_loop(0, 32, bit_step, jnp.full((1, QB), INT_MIN, I32))
    cnt_ge = count(lambda blk, start: jnp.where(blk >= tau, 1, 0))
    excess = jnp.where(tau > INT_MIN, jnp.where(cnt_ge > TOPK, 1, 0), 0).astype(I32)
    j_ref[...] = jnp.where(tau > INT_MIN, jnp.int32(2 ** 30), jnp.int32(-1))

    @pl.when(jnp.max(excess) > 0)
    def _():
        need = TOPK - count(lambda blk, start: jnp.where(blk > tau, 1, 0))

        def pos_step(it, lo):
            cand = lo + lax.shift_left(jnp.int32(1), 12 - it)

            def pred(blk, start):
                pos = start + lax.broadcasted_iota(I32, (128, QB), 0)
                return jnp.where(pos < cand, jnp.where(blk == tau, 1, 0), 0)
            return jnp.where(count(pred) < need, cand, lo)

        lo = lax.fori_loop(0, 13, pos_step, jnp.zeros((1, QB), I32))
        j_ref[...] = jnp.where(excess > 0, lo, j_ref[...])

    j_last = j_ref[...]

    m_ref[...] = jnp.full(m_ref.shape, NEG, F32)
    l_ref[...] = jnp.zeros(l_ref.shape, F32)
    acc_ref[...] = jnp.zeros(acc_ref.shape, F32)

    def attend(start, size, near):
        key = keys_ref[pl.ds(start, size), :]
        pos = start + lax.broadcasted_iota(I32, (size, QB), 0)
        tie = jnp.where(pos <= j_last, 0.0, NEG)
        madd = jnp.where(key > tau, 0.0, jnp.where(key == tau, tie, NEG))
        for h in range(H_A):
            kz = k_ref[pl.ds(start, size), (h // 2) * 128:(h // 2 + 1) * 128]
            s = jnp.dot(kz, qtz_ref[h * 128:(h + 1) * 128, :], preferred_element_type=F32) + madd
            if near:
                s = s + bias_ref[h]
            m_old = m_ref[h:h + 1, :]
            m_new = jnp.maximum(m_old, jnp.max(s, axis=0, keepdims=True))
            alpha = jnp.exp(m_old - m_new)
            p = jnp.exp(s - m_new)
            l_ref[h:h + 1, :] = alpha * l_ref[h:h + 1, :] + jnp.sum(p, axis=0, keepdims=True)
            m_ref[h:h + 1, :] = m_new
            pv = jnp.dot(vt_ref[h * DH_A:(h + 1) * DH_A, pl.ds(start, size)], p.astype(BF16),
                         preferred_element_type=F32)
            acc_ref[h * DH_A:(h + 1) * DH_A, :] = alpha * acc_ref[h * DH_A:(h + 1) * DH_A, :] + pv

    def far_attend(c, carry):
        attend(pl.multiple_of(c * KC, KC), KC, False)
        return carry

    lax.fori_loop(0, iq, far_attend, 0)
    attend(near0, NEAR, True)

    for j in range(W_A // 128):
        parts = []
        for h in (2 * j, 2 * j + 1):
            inv = 1.0 / l_ref[h:h + 1, :]
            parts.append(acc_ref[h * DH_A:(h + 1) * DH_A, :] * inv)
        o_ref[:, j * 128:(j + 1) * 128] = jnp.concatenate(parts, axis=0).T


def _attention(qtz, qit, wt, k, vt, ki, bias):
    b, _, t = qit.shape
    tk = k.shape[1]
    return pl.pallas_call(
        _attn_kernel,
        grid=(b, t // QB),
        in_specs=[pl.BlockSpec((None, 2 * 512, QB), lambda bi, i: (bi, 0, i)),
                  pl.BlockSpec((None, 512, QB), lambda bi, i: (bi, 0, i)),
                  pl.BlockSpec((None, H_IDX, QB), lambda bi, i: (bi, 0, i)),
                  pl.BlockSpec((None, tk, 512), lambda bi, i: (bi, 0, 0)),
                  pl.BlockSpec((None, 512, tk), lambda bi, i: (bi, 0, 0)),
                  pl.BlockSpec((None, tk, D_IDX), lambda bi, i: (bi, 0, 0)),
                  pl.BlockSpec((H_A, NEAR, QB), lambda bi, i: (0, 0, 0))],
        out_specs=pl.BlockSpec((None, QB, W_A), lambda bi, i: (bi, i, 0)),
        out_shape=jax.ShapeDtypeStruct((b, t, W_A), F32),
        scratch_shapes=[pltpu.VMEM((tk, QB), I32),
                        pltpu.VMEM((H_A, QB), F32),
                        pltpu.VMEM((H_A, QB), F32),
                        pltpu.VMEM((W_A, QB), F32),
                        pltpu.VMEM((1, QB), I32)],
        compiler_params=_cparams(("arbitrary", "arbitrary")),
        name="attn",
    )(qtz, qit, wt, k, vt, ki, bias)


def _dot(a, b):
    return jnp.dot(a, b, precision=HI, preferred_element_type=F32)


def _dot_nt(a, b):
    return lax.dot_general(a, b, (((1,), (1,)), ((), ())), precision=HI, preferred_element_type=F32)


def _dot_tn(a, b):
    return lax.dot_general(a, b, (((0,), (0,)), ((), ())), precision=HI, preferred_element_type=F32)


def _gdn_kernel(qkv_x, zb_x, sm_x, qkv_m, sm_m, convw, alog, dtb, gnw, o_ref, xbuf, s_ref):
    n = pl.program_id(1)
    is_meta = n == 0

    @pl.when(is_meta)
    def _():
        xbuf[0:8, :] = jnp.zeros((8, 3 * W_B), F32)
        s_ref[...] = jnp.zeros(s_ref.shape, F32)

    x_in = jnp.where(is_meta, qkv_m[...], qkv_x[...])
    sm = jnp.where(is_meta, sm_m[...], sm_x[...])
    xbuf[8:8 + CHUNK, :] = x_in
    y = convw[CONV_K - 1:CONV_K, :] * x_in
    for i in range(CONV_K - 1):
        off = 8 - (CONV_K - 1) + i
        y = y + convw[i:i + 1, :] * xbuf[off:off + CHUNK, :]
    y = y * jax.nn.sigmoid(y)
    xbuf[0:8, :] = x_in[CHUNK - 8:CHUNK, :]

    row = lax.broadcasted_iota(I32, (CHUNK, 128), 0)
    live = row >= jnp.where(is_meta, CHUNK - N_META, 0)
    z = sm + dtb[...]
    softplus = jnp.maximum(z, 0.0) + jnp.log(1.0 + jnp.exp(-jnp.abs(z)))
    g_all = jnp.where(live, -jnp.exp(alog[...]) * softplus, 0.0)
    beta_all = jnp.where(live, jax.nn.sigmoid(sm), 0.0)

    ri = lax.broadcasted_iota(I32, (CHUNK, CHUNK), 0)
    ci = lax.broadcasted_iota(I32, (CHUNK, CHUNK), 1)
    incl = ri >= ci
    strict = ri > ci
    tri = jnp.where(incl, 1.0, 0.0).astype(F32)
    tri_t = jnp.where(ri <= ci, 1.0, 0.0).astype(F32)
    eye = jnp.where(ri == ci, 1.0, 0.0).astype(F32)
    gc_col_all = _dot(tri, g_all)
    gc_row_all = _dot(g_all.T, tri_t)

    for h in range(H_B):
        q = y[:, h * DK_B:(h + 1) * DK_B]
        k = y[:, W_B + h * DK_B:W_B + (h + 1) * DK_B]
        v = y[:, 2 * W_B + h * DV_B:2 * W_B + (h + 1) * DV_B]
        q = q * lax.rsqrt(jnp.sum(q * q, axis=-1, keepdims=True) + EPS) * (DK_B ** -0.5)
        k = k * lax.rsqrt(jnp.sum(k * k, axis=-1, keepdims=True) + EPS)
        gcol = gc_col_all[:, S_AB + h:S_AB + h + 1]
        grow = gc_row_all[S_AB + h:S_AB + h + 1, :]
        bcol = beta_all[:, S_BB + h:S_BB + h + 1]
        decay = jnp.exp(jnp.where(incl, gcol - grow, -jnp.inf))
        kb = k * bcol
        a = jnp.where(strict, _dot_nt(kb, k) * decay, 0.0)
        tm = eye - a
        pw = a
        for _ in range(5):
            pw = _dot(pw, pw)
            tm = tm + _dot(tm, pw)
        egc = jnp.exp(gcol)
        u = _dot(tm, v * bcol)
        w = _dot(tm, kb * egc)
        att = _dot_nt(q, k) * decay
        g_last = gcol[CHUNK - 1:CHUNK, :]
        q_dec = q * egc
        k_dec = k * jnp.exp(g_last - gcol)
        s_old = s_ref[h]
        v_new = u - _dot(w, s_old)
        o = _dot(q_dec, s_old) + _dot(att, v_new)
        s_ref[h] = s_old * jnp.exp(g_last) + _dot_tn(k_dec, v_new)

        @pl.when(n > 0)
        def _():
            on = o * lax.rsqrt(jnp.mean(o * o, axis=-1, keepdims=True) + EPS) * gnw[...]
            zb = zb_x[:, h * DV_B:(h + 1) * DV_B]
            o_ref[:, h * DV_B:(h + 1) * DV_B] = (on * (zb * jax.nn.sigmoid(zb))).astype(BF16)


def _gdn(p_x, p_m, convw, alog_row, dtb_row, gnw_row):
    b, t, _ = p_x.shape
    nchunk = t // CHUNK + 1

    def xmap(col):
        return lambda bi, n: (bi, jnp.maximum(n - 1, 0), col)

    return pl.pallas_call(
        _gdn_kernel,
        grid=(b, nchunk),
        in_specs=[pl.BlockSpec((None, CHUNK, 3 * W_B), xmap(0)),
                  pl.BlockSpec((None, CHUNK, W_B), xmap(C_ZB // W_B)),
                  pl.BlockSpec((None, CHUNK, 128), xmap(C_SMALL // 128)),
                  pl.BlockSpec((CHUNK, 3 * W_B), lambda bi, n: (1, 0)),
                  pl.BlockSpec((CHUNK, 128), lambda bi, n: (1, C_SMALL // 128)),
                  pl.BlockSpec((CONV_K, 3 * W_B), lambda bi, n: (0, 0)),
                  pl.BlockSpec((1, 128), lambda bi, n: (0, 0)),
                  pl.BlockSpec((1, 128), lambda bi, n: (0, 0)),
                  pl.BlockSpec((1, DV_B), lambda bi, n: (0, 0))],
        out_specs=pl.BlockSpec((None, CHUNK, W_B), lambda bi, n: (bi, jnp.maximum(n - 1, 0), 0)),
        out_shape=jax.ShapeDtypeStruct((b, t, W_B), BF16),
        scratch_shapes=[pltpu.VMEM((8 + CHUNK, 3 * W_B), F32),
                        pltpu.VMEM((H_B, DK_B, DV_B), F32)],
        compiler_params=_cparams(("arbitrary", "arbitrary")),
        name="gdn",
    )(p_x, p_x, p_x, p_m, p_m, convw, alog_row, dtb_row, gnw_row)


def _merge_kernel(ya_ref, za_ref, og_ref, ga_ref, gb_ref, x_ref, wpa, wpb, wout, o_ref):
    za = za_ref[...]
    ya = (ya_ref[...] * (za * jax.nn.sigmoid(za))).astype(BF16)
    ya = jnp.dot(ya, wpa[...], preferred_element_type=F32)
    yb = jnp.dot(og_ref[...], wpb[...], preferred_element_type=F32)
    m = jax.nn.sigmoid(ga_ref[...]) * ya + jax.nn.sigmoid(gb_ref[...]) * yb
    o_ref[...] = x_ref[...] + jnp.dot(m.astype(BF16), wout[...], preferred_element_type=F32)


def _merge(ya, p2d, og, x2d, wpa, wpb, wout, tm):
    m = x2d.shape[0]
    return pl.pallas_call(
        _merge_kernel,
        grid=(m // tm,),
        in_specs=[pl.BlockSpec((tm, W_A), lambda i: (i, 0)),
                  pl.BlockSpec((tm, W_A), lambda i: (i, C_ZA // W_A)),
                  pl.BlockSpec((tm, W_B), lambda i: (i, 0)),
                  pl.BlockSpec((tm, D_MODEL), lambda i: (i, C_GA // D_MODEL)),
                  pl.BlockSpec((tm, D_MODEL), lambda i: (i, C_GB // D_MODEL)),
                  pl.BlockSpec((tm, D_MODEL), lambda i: (i, 0)),
                  pl.BlockSpec((W_A, D_MODEL), lambda i: (0, 0)),
                  pl.BlockSpec((W_B, D_MODEL), lambda i: (0, 0)),
                  pl.BlockSpec((D_MODEL, D_MODEL), lambda i: (0, 0))],
        out_specs=pl.BlockSpec((tm, D_MODEL), lambda i: (i, 0)),
        out_shape=jax.ShapeDtypeStruct((m, D_MODEL), F32),
        compiler_params=_cparams(("arbitrary",)),
        name="merge",
    )(ya, p2d, og, p2d, p2d, x2d, wpa, wpb, wout)


def _permute_w_in(w):
    sizes = (W_A, W_A, W_A, W_A, H_IDX * D_IDX, D_IDX, H_IDX,
             H_B * DK_B, H_B * DK_B, W_B, W_B, H_B, H_B, D_MODEL, D_MODEL)
    offs = [0]
    for s in sizes:
        offs.append(offs[-1] + s)
    (qa, ka, va, za, qi, ki, wi, qb, kb, vb, zb, bb, ab, ga, gb) = [
        w[:, offs[i]:offs[i + 1]] for i in range(len(sizes))]
    pad = jnp.zeros((w.shape[0], 128 - (D_IDX + H_IDX + 2 * H_B)), w.dtype)
    return jnp.concatenate([qb, kb, vb, zb, ga, gb, qa, ka, va, za, qi, ki, wi, bb, ab, pad], axis=1)


def _lane_row(vals, offset):
    return jnp.zeros((1, 128), F32).at[0, offset:offset + vals.shape[0]].set(vals.astype(F32))


def _layer(x, meta_tokens, rel_bias, norm_w, w_in, q_norm_w, k_norm_w, conv_w, a_log, dt_bias,
           gdn_norm_w, w_proj_a, w_proj_b, w_out):
    b, t, _ = x.shape
    w_p = _permute_w_in(w_in).astype(BF16)
    nw_row = norm_w.reshape(1, D_MODEL)
    x2d = x.reshape(b * t, D_MODEL)
    meta_pad = jnp.zeros((META_BLK, D_MODEL), F32).at[META_PAD:].set(meta_tokens)

    p_x2d = _inproj(x2d, nw_row, w_p, 512)
    p_m = _inproj(meta_pad, nw_row, w_p, META_BLK)
    p_x = p_x2d.reshape(b, t, N_P)

    hid = lax.broadcasted_iota(I32, (W_A, W_A), 0) // DH_A
    bd = (hid == hid.T).astype(BF16)
    k_n, v_t, k_i = _prep_k(p_x, p_m, jnp.tile(k_norm_w, H_A).reshape(1, W_A), bd)
    q_tz, q_it, w_t = _prep_q(p_x, jnp.tile(q_norm_w, H_A).reshape(1, W_A), bd)
    bias = _bias_tile(rel_bias)
    ya = _attention(q_tz, q_it, w_t, k_n, v_t, k_i, bias)

    og = _gdn(p_x, p_m, conv_w, _lane_row(a_log, S_AB), _lane_row(dt_bias, S_AB),
              gdn_norm_w.reshape(1, DV_B))

    out = _merge(ya.reshape(b * t, W_A), p_x2d, og.reshape(b * t, W_B), x2d,
                 w_proj_a.astype(BF16), w_proj_b.astype(BF16), w_out.astype(BF16), 512)
    return out.reshape(b, t, D_MODEL)


def kernel(x, meta_tokens, rel_bias, norm_w, w_in, q_norm_w, k_norm_w, conv_w, a_log, dt_bias,
           gdn_norm_w, w_proj_a, w_proj_b, w_out):
    depth = norm_w.shape[0]
    assert depth == 1, "meta rows are dropped after the layer; deeper stacks need them carried"
    return _layer(x, meta_tokens, rel_bias, norm_w[0], w_in[0], q_norm_w[0], k_norm_w[0], conv_w[0],
                  a_log[0], dt_bias[0], gdn_norm_w[0], w_proj_a[0], w_proj_b[0], w_out[0])
```

```python
import functools
import math

import jax
import jax.numpy as jnp
from jax import lax
from jax.experimental import pallas as pl
from jax.experimental.pallas import tpu as pltpu

F32 = jnp.float32
BF16 = jnp.bfloat16
I32 = jnp.int32

D_MODEL = 1024
N_META = 16
H_A = 8
DH_A = 64
W_A = H_A * DH_A
H_IDX = 8
D_IDX = 64
TOPK = 256
N_BUCKETS = 32
MAX_DISTANCE = 128
H_B = 8
DK_B = 128
DV_B = 128
W_B = H_B * DV_B
CONV_K = 4
CHUNK = 64
EPS = 1e-6

C_QKV_B = 0
C_ZB = 3 * 1024
C_GA = 4 * 1024
C_GB = 5 * 1024
C_QA = 6 * 1024
C_KA = C_QA + 512
C_VA = C_KA + 512
C_ZA = C_VA + 512
C_QI = C_ZA + 512
C_SMALL = C_QI + 512
N_P = C_SMALL + 128
S_KI, S_WI, S_BB, S_AB = 0, 64, 72, 80

META_BLK = 128
META_PAD = META_BLK - N_META
QB = 256
KC = 256
NEAR = QB + 128
INT_MIN = -2 ** 31
NEG = -1e30
VMEM_LIMIT = 56 * 1024 * 1024
HI = lax.Precision.HIGHEST
GDN_CHUNKS_PER_STEP = 2


def _cparams(sem):
    return pltpu.CompilerParams(dimension_semantics=sem, vmem_limit_bytes=VMEM_LIMIT)


def _inproj_kernel(x_ref, nw_ref, w_ref, o_ref):
    x = x_ref[...]
    ms = jnp.mean(x * x, axis=-1, keepdims=True)
    h = (x * lax.rsqrt(ms + EPS) * nw_ref[...]).astype(BF16)
    o_ref[...] = jnp.dot(h, w_ref[...], preferred_element_type=F32)


def _inproj(x2d, norm_w_row, w_bf16, tm):
    m = x2d.shape[0]
    tn = N_P // 3
    return pl.pallas_call(
        _inproj_kernel,
        grid=(3, m // tm),
        in_specs=[pl.BlockSpec((tm, D_MODEL), lambda j, i: (i, 0)),
                  pl.BlockSpec((1, D_MODEL), lambda j, i: (0, 0)),
                  pl.BlockSpec((D_MODEL, tn), lambda j, i: (0, j))],
        out_specs=pl.BlockSpec((tm, tn), lambda j, i: (i, j)),
        out_shape=jax.ShapeDtypeStruct((m, N_P), F32),
        compiler_params=_cparams(("arbitrary", "arbitrary")),
        name="inproj",
    )(x2d, norm_w_row, w_bf16)


def _head_rms(x, bd, w_row):
    sq = x * x
    hi = sq.astype(BF16)
    lo = (sq - hi.astype(F32)).astype(BF16)
    ss = (jnp.dot(hi, bd, preferred_element_type=F32) + jnp.dot(lo, bd, preferred_element_type=F32))
    return x * lax.rsqrt(ss * (1.0 / DH_A) + EPS) * w_row


def _prep_k_kernel(kx, vx, sx, km, vm, sm, knw, bd, k_o, vt_o, ki_o):
    is_meta = pl.program_id(1) == 0
    k = jnp.where(is_meta, km[...], kx[...])
    v = jnp.where(is_meta, vm[...], vx[...])
    s = jnp.where(is_meta, sm[...], sx[...])
    k_o[...] = _head_rms(k, bd[...], knw[...]).astype(BF16)
    for j in range(W_A // 128):
        vt_o[j * 128:(j + 1) * 128, :] = v[:, j * 128:(j + 1) * 128].T.astype(BF16)
    ki_o[...] = s[:, S_KI:S_KI + D_IDX].astype(BF16)


def _prep_k(p_x, p_m, knw_row, bd):
    b, t, _ = p_x.shape
    nblk = t // META_BLK + 1
    tk = nblk * META_BLK

    def xmap(col):
        return lambda bi, i: (bi, jnp.maximum(i - 1, 0), col)

    return pl.pallas_call(
        _prep_k_kernel,
        grid=(b, nblk),
        in_specs=[pl.BlockSpec((None, META_BLK, 512), xmap(C_KA // 512)),
                  pl.BlockSpec((None, META_BLK, 512), xmap(C_VA // 512)),
                  pl.BlockSpec((None, META_BLK, 128), xmap(C_SMALL // 128)),
                  pl.BlockSpec((META_BLK, 512), lambda bi, i: (0, C_KA // 512)),
                  pl.BlockSpec((META_BLK, 512), lambda bi, i: (0, C_VA // 512)),
                  pl.BlockSpec((META_BLK, 128), lambda bi, i: (0, C_SMALL // 128)),
                  pl.BlockSpec((1, 512), lambda bi, i: (0, 0)),
                  pl.BlockSpec((512, 512), lambda bi, i: (0, 0))],
        out_specs=[pl.BlockSpec((None, META_BLK, 512), lambda bi, i: (bi, i, 0)),
                   pl.BlockSpec((None, 512, META_BLK), lambda bi, i: (bi, 0, i)),
                   pl.BlockSpec((None, META_BLK, D_IDX), lambda bi, i: (bi, i, 0))],
        out_shape=[jax.ShapeDtypeStruct((b, tk, 512), BF16),
                   jax.ShapeDtypeStruct((b, 512, tk), BF16),
                   jax.ShapeDtypeStruct((b, tk, D_IDX), BF16)],
        compiler_params=_cparams(("arbitrary", "arbitrary")),
        name="prep_k",
    )(p_x, p_x, p_x, p_m, p_m, p_m, knw_row, bd)


def _prep_q_kernel(qx, qix, sx, qnw, bd, qtz_o, qit_o, wt_o):
    qn = _head_rms(qx[...], bd[...], qnw[...]) * (DH_A ** -0.5)
    qi = qix[...]
    zeros = jnp.zeros((64, QB), BF16)
    for j in range(W_A // 128):
        t = qn[:, j * 128:(j + 1) * 128].T.astype(BF16)
        base = 2 * j * 128
        qtz_o[base:base + 64, :] = t[0:64]
        qtz_o[base + 64:base + 128, :] = zeros
        qtz_o[base + 128:base + 192, :] = zeros
        qtz_o[base + 192:base + 256, :] = t[64:128]
        qit_o[j * 128:(j + 1) * 128, :] = qi[:, j * 128:(j + 1) * 128].T.astype(BF16)
    st = sx[...].T
    wt_o[...] = st[S_WI:S_WI + H_IDX, :] * ((H_IDX ** -0.5) * (D_IDX ** -0.5))


def _prep_q(p_x, qnw_row, bd):
    b, t, _ = p_x.shape
    return pl.pallas_call(
        _prep_q_kernel,
        grid=(b, t // QB),
        in_specs=[pl.BlockSpec((None, QB, 512), lambda bi, i: (bi, i, C_QA // 512)),
                  pl.BlockSpec((None, QB, 512), lambda bi, i: (bi, i, C_QI // 512)),
                  pl.BlockSpec((None, QB, 128), lambda bi, i: (bi, i, C_SMALL // 128)),
                  pl.BlockSpec((1, 512), lambda bi, i: (0, 0)),
                  pl.BlockSpec((512, 512), lambda bi, i: (0, 0))],
        out_specs=[pl.BlockSpec((None, 2 * 512, QB), lambda bi, i: (bi, 0, i)),
                   pl.BlockSpec((None, 512, QB), lambda bi, i: (bi, 0, i)),
                   pl.BlockSpec((None, H_IDX, QB), lambda bi, i: (bi, 0, i))],
        out_shape=[jax.ShapeDtypeStruct((b, 2 * 512, t), BF16),
                   jax.ShapeDtypeStruct((b, 512, t), BF16),
                   jax.ShapeDtypeStruct((b, H_IDX, t), F32)],
        compiler_params=_cparams(("arbitrary", "arbitrary")),
        name="prep_q",
    )(p_x, p_x, p_x, qnw_row, bd)


def _bias_kernel(rb_ref, o_ref):
    a = lax.broadcasted_iota(I32, (NEAR, QB), 1)
    bk = lax.broadcasted_iota(I32, (NEAR, QB), 0)
    d = jnp.maximum(a - bk + (NEAR - QB), 0)
    max_exact = N_BUCKETS // 2
    ratio = jnp.maximum(d, max_exact).astype(F32) / max_exact
    large = max_exact + (jnp.log(ratio) / math.log(MAX_DISTANCE / max_exact)
                         * (N_BUCKETS - max_exact)).astype(I32)
    large = jnp.minimum(large, N_BUCKETS - 1)
    bucket = jnp.where(d < max_exact, d, large)
    for h in range(H_A):
        acc = jnp.zeros((NEAR, QB), F32)
        for bb in range(N_BUCKETS):
            acc = jnp.where(bucket == bb, rb_ref[bb, h], acc)
        o_ref[h] = acc - rb_ref[N_BUCKETS - 1, h]


def _bias_tile(rel_bias):
    return pl.pallas_call(
        _bias_kernel,
        in_specs=[pl.BlockSpec(memory_space=pltpu.SMEM)],
        out_specs=pl.BlockSpec((H_A, NEAR, QB), lambda: (0, 0, 0)),
        out_shape=jax.ShapeDtypeStruct((H_A, NEAR, QB), F32),
        name="bias_tile",
    )(rel_bias)


def _attn_kernel(qtz_ref, qit_ref, wt_ref, k_ref, vt_ref, ki_ref, bias_ref, o_ref,
                 keys_ref, m_ref, l_ref, acc_ref, j_ref):
    iq = pl.program_id(1)
    near0 = pl.multiple_of(iq * QB, QB)
    q_pos = META_BLK + iq * QB + lax.broadcasted_iota(I32, (1, QB), 1)

    def score_chunk(start, size):
        kic = ki_ref[pl.ds(start, size), :]
        acc = jnp.zeros((size, QB), F32)
        for h in range(H_IDX):
            r = jnp.dot(kic, qit_ref[h * D_IDX:(h + 1) * D_IDX, :], preferred_element_type=F32)
            acc = acc + jnp.maximum(r, 0.0) * wt_ref[h:h + 1, :]
        acc = acc + 0.0
        bits = pltpu.bitcast(acc, I32)
        key = jnp.where(bits < 0, bits ^ jnp.int32(0x7FFFFFFF), bits)
        pos = start + lax.broadcasted_iota(I32, (size, QB), 0)
        key = jnp.where(pos >= META_PAD, jnp.where(pos <= q_pos, key, INT_MIN), INT_MIN)
        keys_ref[pl.ds(start, size), :] = key

    def far_scores(c, carry):
        score_chunk(pl.multiple_of(c * KC, KC), KC)
        return carry

    lax.fori_loop(0, iq, far_scores, 0)
    score_chunk(near0, NEAR)

    n128 = 2 * iq + NEAR // 128

    def count(pred):
        def body(c, acc):
            start = pl.multiple_of(c * 128, 128)
            blk = keys_ref[pl.ds(start, 128), :]
            return acc + jnp.sum(pred(blk, start).reshape(16, 8, QB), axis=0)
        acc = lax.fori_loop(0, n128, body, jnp.zeros((8, QB), I32))
        return jnp.sum(acc, axis=0, keepdims=True)

    def bit_step(it, tau):
        cand = tau + lax.shift_left(jnp.int32(1), 31 - it)
        cnt = count(lambda blk, start: jnp.where(blk >= cand, 1, 0))
        return jnp.where(cnt >= TOPK, cand, tau)

    tau = lax.fori_loop(0, 32, bit_step, jnp.full((1, QB), INT_MIN, I32))
    cnt_ge = count(lambda blk, start: jnp.where(blk >= tau, 1, 0))
    excess = jnp.where(tau > INT_MIN, jnp.where(cnt_ge > TOPK, 1, 0), 0).astype(I32)
    j_ref[...] = jnp.where(tau > INT_MIN, jnp.int32(2 ** 30), jnp.int32(-1))

    @pl.when(jnp.max(excess) > 0)
    def _():
        need = TOPK - count(lambda blk, start: jnp.where(blk > tau, 1, 0))

        def pos_step(it, lo):
            cand = lo + lax.shift_left(jnp.int32(1), 12 - it)

            def pred(blk, start):
                pos = start + lax.broadcasted_iota(I32, (128, QB), 0)
                return jnp.where(pos < cand, jnp.where(blk == tau, 1, 0), 0)
            return jnp.where(count(pred) < need, cand, lo)

        lo = lax.fori_loop(0, 13, pos_step, jnp.zeros((1, QB), I32))
        j_ref[...] = jnp.where(excess > 0, lo, j_ref[...])

    j_last = j_ref[...]

    m_ref[...] = jnp.full(m_ref.shape, NEG, F32)
    l_ref[...] = jnp.zeros(l_ref.shape, F32)
    acc_ref[...] = jnp.zeros(acc_ref.shape, F32)

    def attend(start, size, near):
        key = keys_ref[pl.ds(start, size), :]
        pos = start + lax.broadcasted_iota(I32, (size, QB), 0)
        tie = jnp.where(pos <= j_last, 0.0, NEG)
        madd = jnp.where(key > tau, 0.0, jnp.where(key == tau, tie, NEG))
        for h in range(H_A):
            kz = k_ref[pl.ds(start, size), (h // 2) * 128:(h // 2 + 1) * 128]
            s = jnp.dot(kz, qtz_ref[h * 128:(h + 1) * 128, :], preferred_element_type=F32) + madd
            if near:
                s = s + bias_ref[h]
            m_old = m_ref[h:h + 1, :]
            m_new = jnp.maximum(m_old, jnp.max(s, axis=0, keepdims=True))
            alpha = jnp.exp(m_old - m_new)
            p = jnp.exp(s - m_new)
            l_ref[h:h + 1, :] = alpha * l_ref[h:h + 1, :] + jnp.sum(p, axis=0, keepdims=True)
            m_ref[h:h + 1, :] = m_new
            pv = jnp.dot(vt_ref[h * DH_A:(h + 1) * DH_A, pl.ds(start, size)], p.astype(BF16),
                         preferred_element_type=F32)
            acc_ref[h * DH_A:(h + 1) * DH_A, :] = alpha * acc_ref[h * DH_A:(h + 1) * DH_A, :] + pv

    def far_attend(c, carry):
        attend(pl.multiple_of(c * KC, KC), KC, False)
        return carry

    lax.fori_loop(0, iq, far_attend, 0)
    attend(near0, NEAR, True)

    for j in range(W_A // 128):
        parts = []
        for h in (2 * j, 2 * j + 1):
            inv = 1.0 / l_ref[h:h + 1, :]
            parts.append(acc_ref[h * DH_A:(h + 1) * DH_A, :] * inv)
        o_ref[:, j * 128:(j + 1) * 128] = jnp.concatenate(parts, axis=0).T


def _attention(qtz, qit, wt, k, vt, ki, bias):
    b, _, t = qit.shape
    tk = k.shape[1]
    return pl.pallas_call(
        _attn_kernel,
        grid=(b, t // QB),
        in_specs=[pl.BlockSpec((None, 2 * 512, QB), lambda bi, i: (bi, 0, i)),
                  pl.BlockSpec((None, 512, QB), lambda bi, i: (bi, 0, i)),
                  pl.BlockSpec((None, H_IDX, QB), lambda bi, i: (bi, 0, i)),
                  pl.BlockSpec((None, tk, 512), lambda bi, i: (bi, 0, 0)),
                  pl.BlockSpec((None, 512, tk), lambda bi, i: (bi, 0, 0)),
                  pl.BlockSpec((None, tk, D_IDX), lambda bi, i: (bi, 0, 0)),
                  pl.BlockSpec((H_A, NEAR, QB), lambda bi, i: (0, 0, 0))],
        out_specs=pl.BlockSpec((None, QB, W_A), lambda bi, i: (bi, i, 0)),
        out_shape=jax.ShapeDtypeStruct((b, t, W_A), F32),
        scratch_shapes=[pltpu.VMEM((tk, QB), I32),
                        pltpu.VMEM((H_A, QB), F32),
                        pltpu.VMEM((H_A, QB), F32),
                        pltpu.VMEM((W_A, QB), F32),
                        pltpu.VMEM((1, QB), I32)],
        compiler_params=_cparams(("arbitrary", "arbitrary")),
        name="attn",
    )(qtz, qit, wt, k, vt, ki, bias)


def _dot_hi(a, b):
    return jnp.dot(a, b, precision=HI, preferred_element_type=F32)


def _mm(a, b):
    return jnp.dot(a.astype(BF16), b.astype(BF16), preferred_element_type=F32)


def _conv_silu(x_in, xbuf, convw):
    r = x_in.shape[0]
    xbuf[8:8 + r, :] = x_in
    y = convw[CONV_K - 1:CONV_K, :] * x_in
    for i in range(CONV_K - 1):
        off = 8 - (CONV_K - 1) + i
        y = y + convw[i:i + 1, :] * xbuf[off:off + r, :]
    return y * jax.nn.sigmoid(y)


def _gates(sm, alog, dtb):
    z = sm + dtb
    softplus = jnp.maximum(z, 0.0) + jnp.log(1.0 + jnp.exp(-jnp.abs(z)))
    return -jnp.exp(alog) * softplus, jax.nn.sigmoid(sm)


def _gdn_local(y, g, beta, n_chunks):
    ri = lax.broadcasted_iota(I32, (CHUNK, CHUNK), 0)
    ci = lax.broadcasted_iota(I32, (CHUNK, CHUNK), 1)
    incl = ri >= ci
    strict = ri > ci
    tri = jnp.where(incl, 1.0, 0.0).astype(F32)
    tri_t = jnp.where(ri <= ci, 1.0, 0.0).astype(F32)
    eye = jnp.where(ri == ci, 1.0, 0.0).astype(F32)
    pairs = [(c, h) for c in range(n_chunks) for h in range(H_B)]
    idx = range(len(pairs))
    gc_col, gc_row = [], []
    for c in range(n_chunks):
        g_c = g[c * CHUNK:(c + 1) * CHUNK]
        gc_col.append(_dot_hi(tri, g_c))
        gc_row.append(_dot_hi(g_c.T, tri_t))
    q, kt, vb, qe, decay, kdt, glast = [], [], [], [], [], [], []
    for c, h in pairs:
        yc = y[c * CHUNK:(c + 1) * CHUNK]
        qh = yc[:, h * DK_B:(h + 1) * DK_B]
        kh = yc[:, W_B + h * DK_B:W_B + (h + 1) * DK_B]
        vh = yc[:, 2 * W_B + h * DV_B:2 * W_B + (h + 1) * DV_B]
        qh = qh * lax.rsqrt(jnp.sum(qh * qh, axis=-1, keepdims=True) + EPS) * (DK_B ** -0.5)
        kh = kh * lax.rsqrt(jnp.sum(kh * kh, axis=-1, keepdims=True) + EPS)
        gcol = gc_col[c][:, S_AB + h:S_AB + h + 1]
        grow = gc_row[c][S_AB + h:S_AB + h + 1, :]
        bcol = beta[c * CHUNK:(c + 1) * CHUNK, S_BB + h:S_BB + h + 1]
        g_last = gcol[CHUNK - 1:CHUNK, :]
        egc = jnp.exp(gcol)
        kth = kh.T
        kb = kh * bcol
        q.append(jnp.concatenate([kb, qh], axis=0))
        kt.append(kth)
        vb.append(jnp.concatenate([vh * bcol, kb * egc], axis=1))
        qe.append(qh * egc)
        decay.append(jnp.exp(jnp.where(incl, gcol - grow, -jnp.inf)))
        kdt.append(kth * jnp.exp(g_last - grow))
        glast.append(jnp.exp(g_last))
    qk = [_mm(q[i], kt[i]) for i in idx]
    a = [jnp.where(strict, qk[i][0:CHUNK] * decay[i], 0.0) for i in idx]
    att = [qk[i][CHUNK:2 * CHUNK] * decay[i] for i in idx]
    tm = [eye - a[i] for i in idx]
    pw = a
    for _ in range(5):
        pw = [_mm(pw[i], pw[i]) for i in idx]
        tm = [tm[i] + _mm(tm[i], pw[i]) for i in idx]
    uw = [_mm(tm[i], vb[i]) for i in idx]
    return uw, qe, att, kdt, glast


def _gdn_scan(local, c, s_list):
    uw, qe, att, kdt, glast = local
    base = c * H_B
    hs = range(H_B)
    ws = [_mm(jnp.concatenate([uw[base + h][:, DV_B:], qe[base + h]], axis=0), s_list[h]) for h in hs]
    v_new = [uw[base + h][:, :DV_B] - ws[h][0:CHUNK] for h in hs]
    ov = [_mm(jnp.concatenate([att[base + h], kdt[base + h]], axis=0), v_new[h]) for h in hs]
    outs = [ws[h][CHUNK:2 * CHUNK] + ov[h][0:CHUNK] for h in hs]
    s_out = [s_list[h] * glast[base + h] + ov[h][CHUNK:CHUNK + DK_B] for h in hs]
    return outs, s_out


def _gdn_meta_kernel(qkv_m, sm_m, convw, alog, dtb, s_o, xbuf):
    xbuf[0:8, :] = jnp.zeros((8, 3 * W_B), F32)
    y = _conv_silu(qkv_m[...], xbuf, convw)
    g, beta = _gates(sm_m[...], alog[...], dtb[...])
    live = lax.broadcasted_iota(I32, (CHUNK, 128), 0) >= CHUNK - N_META
    g = jnp.where(live, g, 0.0)
    beta = jnp.where(live, beta, 0.0)
    _, s_new = _gdn_scan(_gdn_local(y, g, beta, 1), 0, [jnp.zeros((DK_B, DV_B), F32)] * H_B)
    for h in range(H_B):
        s_o[h] = s_new[h]


def _gdn_kernel(qkv_x, zb_x, sm_x, tail_m, s0, convw, alog, dtb, gnw, o_ref, xbuf, s_ref):
    @pl.when(pl.program_id(1) == 0)
    def _():
        xbuf[0:8, :] = tail_m[...]
        s_ref[...] = s0[...]

    x_in = qkv_x[...]
    r = x_in.shape[0]
    y = _conv_silu(x_in, xbuf, convw)
    xbuf[0:8, :] = x_in[r - 8:r, :]
    g, beta = _gates(sm_x[...], alog[...], dtb[...])
    local = _gdn_local(y, g, beta, r // CHUNK)
    s_list = [s_ref[h] for h in range(H_B)]
    for c in range(r // CHUNK):
        rows = slice(c * CHUNK, (c + 1) * CHUNK)
        outs, s_list = _gdn_scan(local, c, s_list)
        for h in range(H_B):
            o = outs[h]
            on = o * lax.rsqrt(jnp.mean(o * o, axis=-1, keepdims=True) + EPS) * gnw[...]
            zb = zb_x[rows, h * DV_B:(h + 1) * DV_B]
            o_ref[rows, h * DV_B:(h + 1) * DV_B] = (on * (zb * jax.nn.sigmoid(zb))).astype(BF16)
    for h in range(H_B):
        s_ref[h] = s_list[h]


def _gdn(p_x, p_m, convw, alog_row, dtb_row, gnw_row):
    b, t, _ = p_x.shape
    r = CHUNK * GDN_CHUNKS_PER_STEP
    last = META_BLK // CHUNK - 1
    s0 = pl.pallas_call(
        _gdn_meta_kernel,
        grid=(1,),
        in_specs=[pl.BlockSpec((CHUNK, 3 * W_B), lambda i: (last, 0)),
                  pl.BlockSpec((CHUNK, 128), lambda i: (last, C_SMALL // 128)),
                  pl.BlockSpec((CONV_K, 3 * W_B), lambda i: (0, 0)),
                  pl.BlockSpec((1, 128), lambda i: (0, 0)),
                  pl.BlockSpec((1, 128), lambda i: (0, 0))],
        out_specs=pl.BlockSpec((H_B, DK_B, DV_B), lambda i: (0, 0, 0)),
        out_shape=jax.ShapeDtypeStruct((H_B, DK_B, DV_B), F32),
        scratch_shapes=[pltpu.VMEM((8 + CHUNK, 3 * W_B), F32)],
        compiler_params=_cparams(("arbitrary",)),
        name="gdn_meta",
    )(p_m, p_m, convw, alog_row, dtb_row)
    return pl.pallas_call(
        _gdn_kernel,
        grid=(b, t // r),
        in_specs=[pl.BlockSpec((None, r, 3 * W_B), lambda bi, n: (bi, n, 0)),
                  pl.BlockSpec((None, r, W_B), lambda bi, n: (bi, n, C_ZB // W_B)),
                  pl.BlockSpec((None, r, 128), lambda bi, n: (bi, n, C_SMALL // 128)),
                  pl.BlockSpec((8, 3 * W_B), lambda bi, n: (META_BLK // 8 - 1, 0)),
                  pl.BlockSpec((H_B, DK_B, DV_B), lambda bi, n: (0, 0, 0)),
                  pl.BlockSpec((CONV_K, 3 * W_B), lambda bi, n: (0, 0)),
                  pl.BlockSpec((1, 128), lambda bi, n: (0, 0)),
                  pl.BlockSpec((1, 128), lambda bi, n: (0, 0)),
                  pl.BlockSpec((1, DV_B), lambda bi, n: (0, 0))],
        out_specs=pl.BlockSpec((None, r, W_B), lambda bi, n: (bi, n, 0)),
        out_shape=jax.ShapeDtypeStruct((b, t, W_B), BF16),
        scratch_shapes=[pltpu.VMEM((8 + r, 3 * W_B), F32),
                        pltpu.VMEM((H_B, DK_B, DV_B), F32)],
        compiler_params=_cparams(("arbitrary", "arbitrary")),
        name="gdn",
    )(p_x, p_x, p_x, p_m, s0, convw, alog_row, dtb_row, gnw_row)


def _merge_kernel(ya_ref, za_ref, og_ref, ga_ref, gb_ref, x_ref, wpa, wpb, wout, o_ref):
    za = za_ref[...]
    ya = (ya_ref[...] * (za * jax.nn.sigmoid(za))).astype(BF16)
    ya = jnp.dot(ya, wpa[...], preferred_element_type=F32)
    yb = jnp.dot(og_ref[...], wpb[...], preferred_element_type=F32)
    m = jax.nn.sigmoid(ga_ref[...]) * ya + jax.nn.sigmoid(gb_ref[...]) * yb
    o_ref[...] = x_ref[...] + jnp.dot(m.astype(BF16), wout[...], preferred_element_type=F32)


def _merge(ya, p2d, og, x2d, wpa, wpb, wout, tm):
    m = x2d.shape[0]
    return pl.pallas_call(
        _merge_kernel,
        grid=(m // tm,),
        in_specs=[pl.BlockSpec((tm, W_A), lambda i: (i, 0)),
                  pl.BlockSpec((tm, W_A), lambda i: (i, C_ZA // W_A)),
                  pl.BlockSpec((tm, W_B), lambda i: (i, 0)),
                  pl.BlockSpec((tm, D_MODEL), lambda i: (i, C_GA // D_MODEL)),
                  pl.BlockSpec((tm, D_MODEL), lambda i: (i, C_GB // D_MODEL)),
                  pl.BlockSpec((tm, D_MODEL), lambda i: (i, 0)),
                  pl.BlockSpec((W_A, D_MODEL), lambda i: (0, 0)),
                  pl.BlockSpec((W_B, D_MODEL), lambda i: (0, 0)),
                  pl.BlockSpec((D_MODEL, D_MODEL), lambda i: (0, 0))],
        out_specs=pl.BlockSpec((tm, D_MODEL), lambda i: (i, 0)),
        out_shape=jax.ShapeDtypeStruct((m, D_MODEL), F32),
        compiler_params=_cparams(("arbitrary",)),
        name="merge",
    )(ya, p2d, og, p2d, p2d, x2d, wpa, wpb, wout)


def _permute_w_in(w):
    sizes = (W_A, W_A, W_A, W_A, H_IDX * D_IDX, D_IDX, H_IDX,
             H_B * DK_B, H_B * DK_B, W_B, W_B, H_B, H_B, D_MODEL, D_MODEL)
    offs = [0]
    for s in sizes:
        offs.append(offs[-1] + s)
    (qa, ka, va, za, qi, ki, wi, qb, kb, vb, zb, bb, ab, ga, gb) = [
        w[:, offs[i]:offs[i + 1]] for i in range(len(sizes))]
    pad = jnp.zeros((w.shape[0], 128 - (D_IDX + H_IDX + 2 * H_B)), w.dtype)
    return jnp.concatenate([qb, kb, vb, zb, ga, gb, qa, ka, va, za, qi, ki, wi, bb, ab, pad], axis=1)


def _lane_row(vals, offset):
    return jnp.zeros((1, 128), F32).at[0, offset:offset + vals.shape[0]].set(vals.astype(F32))


def _layer(x, meta_tokens, rel_bias, norm_w, w_in, q_norm_w, k_norm_w, conv_w, a_log, dt_bias,
           gdn_norm_w, w_proj_a, w_proj_b, w_out):
    b, t, _ = x.shape
    w_p = _permute_w_in(w_in).astype(BF16)
    nw_row = norm_w.reshape(1, D_MODEL)
    x2d = x.reshape(b * t, D_MODEL)
    meta_pad = jnp.zeros((META_BLK, D_MODEL), F32).at[META_PAD:].set(meta_tokens)

    p_x2d = _inproj(x2d, nw_row, w_p, 512)
    p_m = _inproj(meta_pad, nw_row, w_p, META_BLK)
    p_x = p_x2d.reshape(b, t, N_P)

    hid = lax.broadcasted_iota(I32, (W_A, W_A), 0) // DH_A
    bd = (hid == hid.T).astype(BF16)
    k_n, v_t, k_i = _prep_k(p_x, p_m, jnp.tile(k_norm_w, H_A).reshape(1, W_A), bd)
    q_tz, q_it, w_t = _prep_q(p_x, jnp.tile(q_norm_w, H_A).reshape(1, W_A), bd)
    bias = _bias_tile(rel_bias)
    ya = _attention(q_tz, q_it, w_t, k_n, v_t, k_i, bias)

    og = _gdn(p_x, p_m, conv_w, _lane_row(a_log, S_AB), _lane_row(dt_bias, S_AB),
              gdn_norm_w.reshape(1, DV_B))

    out = _merge(ya.reshape(b * t, W_A), p_x2d, og.reshape(b * t, W_B), x2d,
                 w_proj_a.astype(BF16), w_proj_b.astype(BF16), w_out.astype(BF16), 512)
    return out.reshape(b, t, D_MODEL)


def kernel(x, meta_tokens, rel_bias, norm_w, w_in, q_norm_w, k_norm_w, conv_w, a_log, dt_bias,
           gdn_norm_w, w_proj_a, w_proj_b, w_out):
    depth = norm_w.shape[0]
    assert depth == 1, "meta rows are dropped after the layer; deeper stacks need them carried"
    return _layer(x, meta_tokens, rel_bias, norm_w[0], w_in[0], q_norm_w[0], k_norm_w[0], conv_w[0],
                  a_log[0], dt_bias[0], gdn_norm_w[0], w_proj_a[0], w_proj_b[0], w_out[0])
```

```python
import functools
import math

import jax
import jax.numpy as jnp
from jax import lax
from jax.experimental import pallas as pl
from jax.experimental.pallas import tpu as pltpu

F32 = jnp.float32
BF16 = jnp.bfloat16
I32 = jnp.int32

D_MODEL = 1024
N_META = 16
H_A = 8
DH_A = 64
W_A = H_A * DH_A
H_IDX = 8
D_IDX = 64
TOPK = 256
N_BUCKETS = 32
MAX_DISTANCE = 128
H_B = 8
DK_B = 128
DV_B = 128
W_B = H_B * DV_B
CONV_K = 4
CHUNK = 64
EPS = 1e-6

C_QKV_B = 0
C_ZB = 3 * 1024
C_GA = 4 * 1024
C_GB = 5 * 1024
C_QA = 6 * 1024
C_KA = C_QA + 512
C_VA = C_KA + 512
C_ZA = C_VA + 512
C_QI = C_ZA + 512
C_SMALL = C_QI + 512
N_P = C_SMALL + 128
S_KI, S_WI, S_BB, S_AB = 0, 64, 72, 80

META_BLK = 128
META_PAD = META_BLK - N_META
QB = 256
KC = 256
NEAR = QB + 128
VROWS = DH_A + 16
LOG2E = math.log2(math.e)
INT_MIN = -2 ** 31
NEG = -1e30
VMEM_LIMIT = 56 * 1024 * 1024
HI = lax.Precision.HIGHEST
GDN_CHUNKS_PER_STEP = 4


def _cparams(sem):
    return pltpu.CompilerParams(dimension_semantics=sem, vmem_limit_bytes=VMEM_LIMIT)


def _inproj_kernel(x_ref, nw_ref, w_ref, o_ref):
    x = x_ref[...]
    ms = jnp.mean(x * x, axis=-1, keepdims=True)
    h = (x * lax.rsqrt(ms + EPS) * nw_ref[...]).astype(BF16)
    o_ref[...] = jnp.dot(h, w_ref[...], preferred_element_type=F32)


def _inproj(x2d, norm_w_row, w_bf16, tm):
    m = x2d.shape[0]
    tn = N_P // 3
    return pl.pallas_call(
        _inproj_kernel,
        grid=(3, m // tm),
        in_specs=[pl.BlockSpec((tm, D_MODEL), lambda j, i: (i, 0)),
                  pl.BlockSpec((1, D_MODEL), lambda j, i: (0, 0)),
                  pl.BlockSpec((D_MODEL, tn), lambda j, i: (0, j))],
        out_specs=pl.BlockSpec((tm, tn), lambda j, i: (i, j)),
        out_shape=jax.ShapeDtypeStruct((m, N_P), F32),
        compiler_params=_cparams(("arbitrary", "arbitrary")),
        name="inproj",
    )(x2d, norm_w_row, w_bf16)


def _head_rms(x, bd, w_row):
    sq = x * x
    hi = sq.astype(BF16)
    lo = (sq - hi.astype(F32)).astype(BF16)
    ss = (jnp.dot(hi, bd, preferred_element_type=F32) + jnp.dot(lo, bd, preferred_element_type=F32))
    return x * lax.rsqrt(ss * (1.0 / DH_A) + EPS) * w_row


def _prep_k_kernel(kx, vx, sx, km, vm, sm, knw, bd, k_o, vt_o, ki_o):
    is_meta = pl.program_id(1) == 0
    k = jnp.where(is_meta, km[...], kx[...])
    v = jnp.where(is_meta, vm[...], vx[...])
    s = jnp.where(is_meta, sm[...], sx[...])
    k_o[...] = _head_rms(k, bd[...], knw[...]).astype(BF16)
    ones_rows = jnp.where(lax.broadcasted_iota(I32, (VROWS - DH_A, META_BLK), 0) == 0, 1.0, 0.0).astype(BF16)
    for j in range(W_A // 128):
        t = v[:, j * 128:(j + 1) * 128].T.astype(BF16)
        for r in range(2):
            base = (2 * j + r) * VROWS
            vt_o[base:base + DH_A, :] = t[r * DH_A:(r + 1) * DH_A]
            vt_o[base + DH_A:base + VROWS, :] = ones_rows
    ki_o[...] = s[:, S_KI:S_KI + D_IDX].astype(BF16)


def _prep_k(p_x, p_m, knw_row, bd):
    b, t, _ = p_x.shape
    nblk = t // META_BLK + 1
    tk = nblk * META_BLK

    def xmap(col):
        return lambda bi, i: (bi, jnp.maximum(i - 1, 0), col)

    return pl.pallas_call(
        _prep_k_kernel,
        grid=(b, nblk),
        in_specs=[pl.BlockSpec((None, META_BLK, 512), xmap(C_KA // 512)),
                  pl.BlockSpec((None, META_BLK, 512), xmap(C_VA // 512)),
                  pl.BlockSpec((None, META_BLK, 128), xmap(C_SMALL // 128)),
                  pl.BlockSpec((META_BLK, 512), lambda bi, i: (0, C_KA // 512)),
                  pl.BlockSpec((META_BLK, 512), lambda bi, i: (0, C_VA // 512)),
                  pl.BlockSpec((META_BLK, 128), lambda bi, i: (0, C_SMALL // 128)),
                  pl.BlockSpec((1, 512), lambda bi, i: (0, 0)),
                  pl.BlockSpec((512, 512), lambda bi, i: (0, 0))],
        out_specs=[pl.BlockSpec((None, META_BLK, 512), lambda bi, i: (bi, i, 0)),
                   pl.BlockSpec((None, H_A * VROWS, META_BLK), lambda bi, i: (bi, 0, i)),
                   pl.BlockSpec((None, META_BLK, D_IDX), lambda bi, i: (bi, i, 0))],
        out_shape=[jax.ShapeDtypeStruct((b, tk, 512), BF16),
                   jax.ShapeDtypeStruct((b, H_A * VROWS, tk), BF16),
                   jax.ShapeDtypeStruct((b, tk, D_IDX), BF16)],
        compiler_params=_cparams(("arbitrary", "arbitrary")),
        name="prep_k",
    )(p_x, p_x, p_x, p_m, p_m, p_m, knw_row, bd)


def _prep_q_kernel(qx, qix, sx, qnw, bd, qtz_o, qit_o, wt_o):
    qn = _head_rms(qx[...], bd[...], qnw[...]) * (DH_A ** -0.5 * LOG2E)
    qi = qix[...]
    zeros = jnp.zeros((64, QB), BF16)
    for j in range(W_A // 128):
        t = qn[:, j * 128:(j + 1) * 128].T.astype(BF16)
        base = 2 * j * 128
        qtz_o[base:base + 64, :] = t[0:64]
        qtz_o[base + 64:base + 128, :] = zeros
        qtz_o[base + 128:base + 192, :] = zeros
        qtz_o[base + 192:base + 256, :] = t[64:128]
        qit_o[j * 128:(j + 1) * 128, :] = qi[:, j * 128:(j + 1) * 128].T.astype(BF16)
    st = sx[...].T
    wt_o[...] = st[S_WI:S_WI + H_IDX, :] * ((H_IDX ** -0.5) * (D_IDX ** -0.5))


def _prep_q(p_x, qnw_row, bd):
    b, t, _ = p_x.shape
    return pl.pallas_call(
        _prep_q_kernel,
        grid=(b, t // QB),
        in_specs=[pl.BlockSpec((None, QB, 512), lambda bi, i: (bi, i, C_QA // 512)),
                  pl.BlockSpec((None, QB, 512), lambda bi, i: (bi, i, C_QI // 512)),
                  pl.BlockSpec((None, QB, 128), lambda bi, i: (bi, i, C_SMALL // 128)),
                  pl.BlockSpec((1, 512), lambda bi, i: (0, 0)),
                  pl.BlockSpec((512, 512), lambda bi, i: (0, 0))],
        out_specs=[pl.BlockSpec((None, 2 * 512, QB), lambda bi, i: (bi, 0, i)),
                   pl.BlockSpec((None, 512, QB), lambda bi, i: (bi, 0, i)),
                   pl.BlockSpec((None, H_IDX, QB), lambda bi, i: (bi, 0, i))],
        out_shape=[jax.ShapeDtypeStruct((b, 2 * 512, t), BF16),
                   jax.ShapeDtypeStruct((b, 512, t), BF16),
                   jax.ShapeDtypeStruct((b, H_IDX, t), F32)],
        compiler_params=_cparams(("arbitrary", "arbitrary")),
        name="prep_q",
    )(p_x, p_x, p_x, qnw_row, bd)


def _bias_kernel(rb_ref, o_ref):
    a = lax.broadcasted_iota(I32, (NEAR, QB), 1)
    bk = lax.broadcasted_iota(I32, (NEAR, QB), 0)
    d = jnp.maximum(a - bk + (NEAR - QB), 0)
    max_exact = N_BUCKETS // 2
    ratio = jnp.maximum(d, max_exact).astype(F32) / max_exact
    large = max_exact + (jnp.log(ratio) / math.log(MAX_DISTANCE / max_exact)
                         * (N_BUCKETS - max_exact)).astype(I32)
    large = jnp.minimum(large, N_BUCKETS - 1)
    bucket = jnp.where(d < max_exact, d, large)
    for h in range(H_A):
        acc = jnp.zeros((NEAR, QB), F32)
        for bb in range(N_BUCKETS):
            acc = jnp.where(bucket == bb, rb_ref[bb, h], acc)
        o_ref[h] = (acc - rb_ref[N_BUCKETS - 1, h]) * LOG2E


def _bias_tile(rel_bias):
    return pl.pallas_call(
        _bias_kernel,
        in_specs=[pl.BlockSpec(memory_space=pltpu.SMEM)],
        out_specs=pl.BlockSpec((H_A, NEAR, QB), lambda: (0, 0, 0)),
        out_shape=jax.ShapeDtypeStruct((H_A, NEAR, QB), F32),
        name="bias_tile",
    )(rel_bias)


def _attn_kernel(qtz_ref, qit_ref, wt_ref, k_ref, vt_ref, ki_ref, bias_ref, o_ref,
                 keys_ref, m_ref, acc_ref, j_ref):
    iq = pl.program_id(1)
    near0 = pl.multiple_of(iq * QB, QB)
    q_pos = META_BLK + iq * QB + lax.broadcasted_iota(I32, (1, QB), 1)

    def score_chunk(start, size, mask_pad, mask_causal):
        kic = ki_ref[pl.ds(start, size), :]
        acc = jnp.zeros((size, QB), F32)
        for h in range(H_IDX):
            r = jnp.dot(kic, qit_ref[h * D_IDX:(h + 1) * D_IDX, :], preferred_element_type=F32)
            acc = acc + jnp.maximum(r, 0.0) * wt_ref[h:h + 1, :]
        acc = acc + 0.0
        bits = pltpu.bitcast(acc, I32)
        key = jnp.where(bits < 0, bits ^ jnp.int32(0x7FFFFFFF), bits)
        if mask_pad or mask_causal:
            pos = start + lax.broadcasted_iota(I32, (size, QB), 0)
            if mask_causal:
                key = jnp.where(pos <= q_pos, key, INT_MIN)
            if mask_pad:
                key = jnp.where(pos >= META_PAD, key, INT_MIN)
        keys_ref[pl.ds(start, size), :] = key

    @pl.when(iq > 0)
    def _():
        score_chunk(0, KC, True, False)

    def far_scores(c, carry):
        score_chunk(pl.multiple_of(c * KC, KC), KC, False, False)
        return carry

    lax.fori_loop(1, iq, far_scores, 0)
    score_chunk(near0, NEAR, True, True)

    n256 = iq + NEAR // 256
    tail0 = pl.multiple_of(n256 * 256, 128)

    def count(pred):
        def part(start, size):
            blk = keys_ref[pl.ds(start, size), :]
            return jnp.sum(pred(blk, start).reshape(size // 32, 32, QB), axis=0)

        def body(c, acc):
            return acc + part(pl.multiple_of(c * 256, 256), 256)
        acc = lax.fori_loop(0, n256, body, jnp.zeros((32, QB), I32))
        return jnp.sum(acc + part(tail0, NEAR % 256), axis=0, keepdims=True)

    def bit_step(it, tau):
        cand = tau + lax.shift_left(jnp.int32(1), 31 - it)
        cnt = count(lambda blk, start: jnp.where(blk >= cand, 1, 0))
        return jnp.where(cnt >= TOPK, cand, tau)

    tau = lax.fori_loop(0, 32, bit_step, jnp.full((1, QB), INT_MIN, I32))
    cnt_ge = count(lambda blk, start: jnp.where(blk >= tau, 1, 0))
    excess = jnp.where(tau > INT_MIN, jnp.where(cnt_ge > TOPK, 1, 0), 0).astype(I32)
    j_ref[...] = jnp.where(tau > INT_MIN, jnp.int32(2 ** 30), jnp.int32(-1))

    @pl.when(jnp.max(excess) > 0)
    def _():
        need = TOPK - count(lambda blk, start: jnp.where(blk > tau, 1, 0))

        def pos_step(it, lo):
            cand = lo + lax.shift_left(jnp.int32(1), 12 - it)

            def pred(blk, start):
                pos = start + lax.broadcasted_iota(I32, blk.shape, 0)
                return jnp.where(pos < cand, jnp.where(blk == tau, 1, 0), 0)
            return jnp.where(count(pred) < need, cand, lo)

        lo = lax.fori_loop(0, 13, pos_step, jnp.zeros((1, QB), I32))
        j_ref[...] = jnp.where(excess > 0, lo, j_ref[...])

    j_last = j_ref[...]

    m_ref[...] = jnp.full(m_ref.shape, NEG, F32)
    acc_ref[...] = jnp.zeros(acc_ref.shape, F32)

    def attend(start, size, near):
        key = keys_ref[pl.ds(start, size), :]
        pos = start + lax.broadcasted_iota(I32, (size, QB), 0)
        tie = jnp.where(pos <= j_last, 0.0, NEG)
        madd = jnp.where(key > tau, 0.0, jnp.where(key == tau, tie, NEG))
        hs = range(H_A)
        s = [jnp.dot(k_ref[pl.ds(start, size), (h // 2) * 128:(h // 2 + 1) * 128],
                     qtz_ref[h * 128:(h + 1) * 128, :], preferred_element_type=F32) + madd for h in hs]
        if near:
            s = [s[h] + bias_ref[h] for h in hs]
        m_all = m_ref[...]
        m_new = [jnp.maximum(m_all[h:h + 1, :], jnp.max(s[h], axis=0, keepdims=True)) for h in hs]
        m_new_all = jnp.concatenate(m_new, axis=0)
        alpha_all = jnp.exp2(m_all - m_new_all)
        m_ref[...] = m_new_all
        p = [jnp.exp2(s[h] - m_new[h]).astype(BF16) for h in hs]
        pv = [jnp.dot(vt_ref[h * VROWS:(h + 1) * VROWS, pl.ds(start, size)], p[h],
                      preferred_element_type=F32) for h in hs]
        for h in hs:
            acc_ref[h * VROWS:(h + 1) * VROWS, :] = (alpha_all[h:h + 1, :] * acc_ref[h * VROWS:(h + 1) * VROWS, :]
                                                     + pv[h])

    def far_attend(c, carry):
        attend(pl.multiple_of(c * KC, KC), KC, False)
        return carry

    lax.fori_loop(0, iq, far_attend, 0)
    attend(near0, NEAR, True)

    for j in range(W_A // 128):
        parts = []
        for h in (2 * j, 2 * j + 1):
            inv = 1.0 / acc_ref[h * VROWS + DH_A:h * VROWS + DH_A + 1, :]
            parts.append(acc_ref[h * VROWS:h * VROWS + DH_A, :] * inv)
        o_ref[:, j * 128:(j + 1) * 128] = jnp.concatenate(parts, axis=0).T


def _attention(qtz, qit, wt, k, vt, ki, bias):
    b, _, t = qit.shape
    tk = k.shape[1]
    return pl.pallas_call(
        _attn_kernel,
        grid=(b, t // QB),
        in_specs=[pl.BlockSpec((None, 2 * 512, QB), lambda bi, i: (bi, 0, i)),
                  pl.BlockSpec((None, 512, QB), lambda bi, i: (bi, 0, i)),
                  pl.BlockSpec((None, H_IDX, QB), lambda bi, i: (bi, 0, i)),
                  pl.BlockSpec((None, tk, 512), lambda bi, i: (bi, 0, 0)),
                  pl.BlockSpec((None, H_A * VROWS, tk), lambda bi, i: (bi, 0, 0)),
                  pl.BlockSpec((None, tk, D_IDX), lambda bi, i: (bi, 0, 0)),
                  pl.BlockSpec((H_A, NEAR, QB), lambda bi, i: (0, 0, 0))],
        out_specs=pl.BlockSpec((None, QB, W_A), lambda bi, i: (bi, i, 0)),
        out_shape=jax.ShapeDtypeStruct((b, t, W_A), F32),
        scratch_shapes=[pltpu.VMEM((tk, QB), I32),
                        pltpu.VMEM((H_A, QB), F32),
                        pltpu.VMEM((H_A * VROWS, QB), F32),
                        pltpu.VMEM((1, QB), I32)],
        compiler_params=_cparams(("arbitrary", "arbitrary")),
        name="attn",
    )(qtz, qit, wt, k, vt, ki, bias)


def _dot_hi(a, b):
    return jnp.dot(a, b, precision=HI, preferred_element_type=F32)


def _mm(a, b):
    return jnp.dot(a.astype(BF16), b.astype(BF16), preferred_element_type=F32)


def _conv_silu(x_in, xbuf, convw):
    r = x_in.shape[0]
    xbuf[8:8 + r, :] = x_in
    y = convw[CONV_K - 1:CONV_K, :] * x_in
    for i in range(CONV_K - 1):
        off = 8 - (CONV_K - 1) + i
        y = y + convw[i:i + 1, :] * xbuf[off:off + r, :]
    return y * jax.nn.sigmoid(y)


def _gates(sm, alog, dtb):
    z = sm + dtb
    softplus = jnp.maximum(z, 0.0) + jnp.log(1.0 + jnp.exp(-jnp.abs(z)))
    return -jnp.exp(alog) * softplus, jax.nn.sigmoid(sm)


def _gdn_local(y, g, beta, n_chunks):
    ri = lax.broadcasted_iota(I32, (CHUNK, CHUNK), 0)
    ci = lax.broadcasted_iota(I32, (CHUNK, CHUNK), 1)
    incl = ri >= ci
    strict = ri > ci
    tri = jnp.where(incl, 1.0, 0.0).astype(F32)
    tri_t = jnp.where(ri <= ci, 1.0, 0.0).astype(F32)
    eye = jnp.where(ri == ci, 1.0, 0.0).astype(F32)
    pairs = [(c, h) for c in range(n_chunks) for h in range(H_B)]
    idx = range(len(pairs))
    gc_col, gc_row = [], []
    for c in range(n_chunks):
        g_c = g[c * CHUNK:(c + 1) * CHUNK]
        gc_col.append(_dot_hi(tri, g_c))
        gc_row.append(_dot_hi(g_c.T, tri_t))
    q, kt, vb, qe, decay, kdt, glast = [], [], [], [], [], [], []
    for c, h in pairs:
        yc = y[c * CHUNK:(c + 1) * CHUNK]
        qh = yc[:, h * DK_B:(h + 1) * DK_B]
        kh = yc[:, W_B + h * DK_B:W_B + (h + 1) * DK_B]
        vh = yc[:, 2 * W_B + h * DV_B:2 * W_B + (h + 1) * DV_B]
        qh = qh * lax.rsqrt(jnp.sum(qh * qh, axis=-1, keepdims=True) + EPS) * (DK_B ** -0.5)
        kh = kh * lax.rsqrt(jnp.sum(kh * kh, axis=-1, keepdims=True) + EPS)
        gcol = gc_col[c][:, S_AB + h:S_AB + h + 1]
        grow = gc_row[c][S_AB + h:S_AB + h + 1, :]
        bcol = beta[c * CHUNK:(c + 1) * CHUNK, S_BB + h:S_BB + h + 1]
        g_last = gcol[CHUNK - 1:CHUNK, :]
        egc = jnp.exp(gcol)
        kth = kh.T
        kb = kh * bcol
        q.append(jnp.concatenate([kb, qh], axis=0))
        kt.append(kth)
        vb.append(jnp.concatenate([vh * bcol, kb * egc], axis=1))
        qe.append(qh * egc)
        decay.append(jnp.exp(jnp.where(incl, gcol - grow, -jnp.inf)))
        kdt.append(kth * jnp.exp(g_last - grow))
        glast.append(jnp.exp(g_last))
    qk = [_mm(q[i], kt[i]) for i in idx]
    a = [jnp.where(strict, qk[i][0:CHUNK] * decay[i], 0.0) for i in idx]
    att = [qk[i][CHUNK:2 * CHUNK] * decay[i] for i in idx]
    tm = [eye - a[i] for i in idx]
    pw = a
    for _ in range(5):
        pw = [_mm(pw[i], pw[i]) for i in idx]
        tm = [tm[i] + _mm(tm[i], pw[i]) for i in idx]
    uw = [_mm(tm[i], vb[i]) for i in idx]
    return uw, qe, att, kdt, glast


def _gdn_scan(local, c, s_list):
    uw, qe, att, kdt, glast = local
    base = c * H_B
    hs = range(H_B)
    ws = [_mm(jnp.concatenate([uw[base + h][:, DV_B:], qe[base + h]], axis=0), s_list[h]) for h in hs]
    v_new = [uw[base + h][:, :DV_B] - ws[h][0:CHUNK] for h in hs]
    ov = [_mm(jnp.concatenate([att[base + h], kdt[base + h]], axis=0), v_new[h]) for h in hs]
    outs = [ws[h][CHUNK:2 * CHUNK] + ov[h][0:CHUNK] for h in hs]
    s_out = [s_list[h] * glast[base + h] + ov[h][CHUNK:CHUNK + DK_B] for h in hs]
    return outs, s_out


def _gdn_meta_kernel(qkv_m, sm_m, convw, alog, dtb, s_o, xbuf):
    xbuf[0:8, :] = jnp.zeros((8, 3 * W_B), F32)
    y = _conv_silu(qkv_m[...], xbuf, convw)
    g, beta = _gates(sm_m[...], alog[...], dtb[...])
    live = lax.broadcasted_iota(I32, (CHUNK, 128), 0) >= CHUNK - N_META
    g = jnp.where(live, g, 0.0)
    beta = jnp.where(live, beta, 0.0)
    _, s_new = _gdn_scan(_gdn_local(y, g, beta, 1), 0, [jnp.zeros((DK_B, DV_B), F32)] * H_B)
    for h in range(H_B):
        s_o[h] = s_new[h]


def _gdn_kernel(qkv_x, zb_x, sm_x, tail_m, s0, convw, alog, dtb, gnw, o_ref, xbuf, s_ref):
    @pl.when(pl.program_id(1) == 0)
    def _():
        xbuf[0:8, :] = tail_m[...]
        s_ref[...] = s0[...]

    x_in = qkv_x[...]
    r = x_in.shape[0]
    y = _conv_silu(x_in, xbuf, convw)
    xbuf[0:8, :] = x_in[r - 8:r, :]
    g, beta = _gates(sm_x[...], alog[...], dtb[...])
    local = _gdn_local(y, g, beta, r // CHUNK)
    s_list = [s_ref[h] for h in range(H_B)]
    for c in range(r // CHUNK):
        rows = slice(c * CHUNK, (c + 1) * CHUNK)
        outs, s_list = _gdn_scan(local, c, s_list)
        for h in range(H_B):
            o = outs[h]
            on = o * lax.rsqrt(jnp.mean(o * o, axis=-1, keepdims=True) + EPS) * gnw[...]
            zb = zb_x[rows, h * DV_B:(h + 1) * DV_B]
            o_ref[rows, h * DV_B:(h + 1) * DV_B] = (on * (zb * jax.nn.sigmoid(zb))).astype(BF16)
    for h in range(H_B):
        s_ref[h] = s_list[h]


def _gdn(p_x, p_m, convw, alog_row, dtb_row, gnw_row):
    b, t, _ = p_x.shape
    r = CHUNK * GDN_CHUNKS_PER_STEP
    last = META_BLK // CHUNK - 1
    s0 = pl.pallas_call(
        _gdn_meta_kernel,
        grid=(1,),
        in_specs=[pl.BlockSpec((CHUNK, 3 * W_B), lambda i: (last, 0)),
                  pl.BlockSpec((CHUNK, 128), lambda i: (last, C_SMALL // 128)),
                  pl.BlockSpec((CONV_K, 3 * W_B), lambda i: (0, 0)),
                  pl.BlockSpec((1, 128), lambda i: (0, 0)),
                  pl.BlockSpec((1, 128), lambda i: (0, 0))],
        out_specs=pl.BlockSpec((H_B, DK_B, DV_B), lambda i: (0, 0, 0)),
        out_shape=jax.ShapeDtypeStruct((H_B, DK_B, DV_B), F32),
        scratch_shapes=[pltpu.VMEM((8 + CHUNK, 3 * W_B), F32)],
        compiler_params=_cparams(("arbitrary",)),
        name="gdn_meta",
    )(p_m, p_m, convw, alog_row, dtb_row)
    return pl.pallas_call(
        _gdn_kernel,
        grid=(b, t // r),
        in_specs=[pl.BlockSpec((None, r, 3 * W_B), lambda bi, n: (bi, n, 0)),
                  pl.BlockSpec((None, r, W_B), lambda bi, n: (bi, n, C_ZB // W_B)),
                  pl.BlockSpec((None, r, 128), lambda bi, n: (bi, n, C_SMALL // 128)),
                  pl.BlockSpec((8, 3 * W_B), lambda bi, n: (META_BLK // 8 - 1, 0)),
                  pl.BlockSpec((H_B, DK_B, DV_B), lambda bi, n: (0, 0, 0)),
                  pl.BlockSpec((CONV_K, 3 * W_B), lambda bi, n: (0, 0)),
                  pl.BlockSpec((1, 128), lambda bi, n: (0, 0)),
                  pl.BlockSpec((1, 128), lambda bi, n: (0, 0)),
                  pl.BlockSpec((1, DV_B), lambda bi, n: (0, 0))],
        out_specs=pl.BlockSpec((None, r, W_B), lambda bi, n: (bi, n, 0)),
        out_shape=jax.ShapeDtypeStruct((b, t, W_B), BF16),
        scratch_shapes=[pltpu.VMEM((8 + r, 3 * W_B), F32),
                        pltpu.VMEM((H_B, DK_B, DV_B), F32)],
        compiler_params=_cparams(("arbitrary", "arbitrary")),
        name="gdn",
    )(p_x, p_x, p_x, p_m, s0, convw, alog_row, dtb_row, gnw_row)


def _merge_kernel(ya_ref, za_ref, og_ref, ga_ref, gb_ref, x_ref, wpa, wpb, wout, o_ref):
    za = za_ref[...]
    ya = (ya_ref[...] * (za * jax.nn.sigmoid(za))).astype(BF16)
    ya = jnp.dot(ya, wpa[...], preferred_element_type=F32)
    yb = jnp.dot(og_ref[...], wpb[...], preferred_element_type=F32)
    m = jax.nn.sigmoid(ga_ref[...]) * ya + jax.nn.sigmoid(gb_ref[...]) * yb
    o_ref[...] = x_ref[...] + jnp.dot(m.astype(BF16), wout[...], preferred_element_type=F32)


def _merge(ya, p2d, og, x2d, wpa, wpb, wout, tm):
    m = x2d.shape[0]
    return pl.pallas_call(
        _merge_kernel,
        grid=(m // tm,),
        in_specs=[pl.BlockSpec((tm, W_A), lambda i: (i, 0)),
                  pl.BlockSpec((tm, W_A), lambda i: (i, C_ZA // W_A)),
                  pl.BlockSpec((tm, W_B), lambda i: (i, 0)),
                  pl.BlockSpec((tm, D_MODEL), lambda i: (i, C_GA // D_MODEL)),
                  pl.BlockSpec((tm, D_MODEL), lambda i: (i, C_GB // D_MODEL)),
                  pl.BlockSpec((tm, D_MODEL), lambda i: (i, 0)),
                  pl.BlockSpec((W_A, D_MODEL), lambda i: (0, 0)),
                  pl.BlockSpec((W_B, D_MODEL), lambda i: (0, 0)),
                  pl.BlockSpec((D_MODEL, D_MODEL), lambda i: (0, 0))],
        out_specs=pl.BlockSpec((tm, D_MODEL), lambda i: (i, 0)),
        out_shape=jax.ShapeDtypeStruct((m, D_MODEL), F32),
        compiler_params=_cparams(("arbitrary",)),
        name="merge",
    )(ya, p2d, og, p2d, p2d, x2d, wpa, wpb, wout)


def _permute_w_in(w):
    sizes = (W_A, W_A, W_A, W_A, H_IDX * D_IDX, D_IDX, H_IDX,
             H_B * DK_B, H_B * DK_B, W_B, W_B, H_B, H_B, D_MODEL, D_MODEL)
    offs = [0]
    for s in sizes:
        offs.append(offs[-1] + s)
    (qa, ka, va, za, qi, ki, wi, qb, kb, vb, zb, bb, ab, ga, gb) = [
        w[:, offs[i]:offs[i + 1]] for i in range(len(sizes))]
    pad = jnp.zeros((w.shape[0], 128 - (D_IDX + H_IDX + 2 * H_B)), w.dtype)
    return jnp.concatenate([qb, kb, vb, zb, ga, gb, qa, ka, va, za, qi, ki, wi, bb, ab, pad], axis=1)


def _lane_row(vals, offset):
    return jnp.zeros((1, 128), F32).at[0, offset:offset + vals.shape[0]].set(vals.astype(F32))


def _layer(x, meta_tokens, rel_bias, norm_w, w_in, q_norm_w, k_norm_w, conv_w, a_log, dt_bias,
           gdn_norm_w, w_proj_a, w_proj_b, w_out):
    b, t, _ = x.shape
    w_p = _permute_w_in(w_in).astype(BF16)
    nw_row = norm_w.reshape(1, D_MODEL)
    x2d = x.reshape(b * t, D_MODEL)
    meta_pad = jnp.zeros((META_BLK, D_MODEL), F32).at[META_PAD:].set(meta_tokens)

    p_x2d = _inproj(x2d, nw_row, w_p, 512)
    p_m = _inproj(meta_pad, nw_row, w_p, META_BLK)
    p_x = p_x2d.reshape(b, t, N_P)

    hid = lax.broadcasted_iota(I32, (W_A, W_A), 0) // DH_A
    bd = (hid == hid.T).astype(BF16)
    k_n, v_t, k_i = _prep_k(p_x, p_m, jnp.tile(k_norm_w, H_A).reshape(1, W_A), bd)
    q_tz, q_it, w_t = _prep_q(p_x, jnp.tile(q_norm_w, H_A).reshape(1, W_A), bd)
    bias = _bias_tile(rel_bias)
    ya = _attention(q_tz, q_it, w_t, k_n, v_t, k_i, bias)

    og = _gdn(p_x, p_m, conv_w, _lane_row(a_log, S_AB), _lane_row(dt_bias, S_AB),
              gdn_norm_w.reshape(1, DV_B))

    out = _merge(ya.reshape(b * t, W_A), p_x2d, og.reshape(b * t, W_B), x2d,
                 w_proj_a.astype(BF16), w_proj_b.astype(BF16), w_out.astype(BF16), 512)
    return out.reshape(b, t, D_MODEL)


def kernel(x, meta_tokens, rel_bias, norm_w, w_in, q_norm_w, k_norm_w, conv_w, a_log, dt_bias,
           gdn_norm_w, w_proj_a, w_proj_b, w_out):
    depth = norm_w.shape[0]
    assert depth == 1, "meta rows are dropped after the layer; deeper stacks need them carried"
    return _layer(x, meta_tokens, rel_bias, norm_w[0], w_in[0], q_norm_w[0], k_norm_w[0], conv_w[0],
                  a_log[0], dt_bias[0], gdn_norm_w[0], w_proj_a[0], w_proj_b[0], w_out[0])
```

```python
import functools
import math

import jax
import jax.numpy as jnp
from jax import lax
from jax.experimental import pallas as pl
from jax.experimental.pallas import tpu as pltpu

F32 = jnp.float32
BF16 = jnp.bfloat16
I32 = jnp.int32

D_MODEL = 1024
N_META = 16
H_A = 8
DH_A = 64
W_A = H_A * DH_A
H_IDX = 8
D_IDX = 64
TOPK = 256
N_BUCKETS = 32
MAX_DISTANCE = 128
H_B = 8
DK_B = 128
DV_B = 128
W_B = H_B * DV_B
CONV_K = 4
CHUNK = 64
EPS = 1e-6

C_QKV_B = 0
C_ZB = 3 * 1024
C_GA = 4 * 1024
C_GB = 5 * 1024
C_QA = 6 * 1024
C_KA = C_QA + 512
C_VA = C_KA + 512
C_ZA = C_VA + 512
C_QI = C_ZA + 512
C_SMALL = C_QI + 512
N_P = C_SMALL + 128
S_KI, S_WI, S_BB, S_AB = 0, 64, 72, 80

META_BLK = 128
META_PAD = META_BLK - N_META
QB = 256
KC = 256
NEAR = QB + 128
VROWS = DH_A + 16
LOG2E = math.log2(math.e)
INT_MIN = -2 ** 31
NEG = -1e30
VMEM_LIMIT = 56 * 1024 * 1024
HI = lax.Precision.HIGHEST
GDN_CHUNKS_PER_STEP = 4


def _cparams(sem):
    return pltpu.CompilerParams(dimension_semantics=sem, vmem_limit_bytes=VMEM_LIMIT)


def _inproj_kernel(x_ref, nw_ref, w_ref, o_ref):
    x = x_ref[...]
    ms = jnp.mean(x * x, axis=-1, keepdims=True)
    h = (x * lax.rsqrt(ms + EPS) * nw_ref[...]).astype(BF16)
    o_ref[...] = jnp.dot(h, w_ref[...], preferred_element_type=F32)


def _inproj(x2d, norm_w_row, w_bf16, tm):
    m = x2d.shape[0]
    tn = N_P // 3
    return pl.pallas_call(
        _inproj_kernel,
        grid=(3, m // tm),
        in_specs=[pl.BlockSpec((tm, D_MODEL), lambda j, i: (i, 0)),
                  pl.BlockSpec((1, D_MODEL), lambda j, i: (0, 0)),
                  pl.BlockSpec((D_MODEL, tn), lambda j, i: (0, j))],
        out_specs=pl.BlockSpec((tm, tn), lambda j, i: (i, j)),
        out_shape=jax.ShapeDtypeStruct((m, N_P), F32),
        compiler_params=_cparams(("arbitrary", "arbitrary")),
        name="inproj",
    )(x2d, norm_w_row, w_bf16)


def _head_rms(x, bd, w_row):
    sq = x * x
    hi = sq.astype(BF16)
    lo = (sq - hi.astype(F32)).astype(BF16)
    ss = (jnp.dot(hi, bd, preferred_element_type=F32) + jnp.dot(lo, bd, preferred_element_type=F32))
    return x * lax.rsqrt(ss * (1.0 / DH_A) + EPS) * w_row


def _prep_k_kernel(kx, vx, sx, km, vm, sm, knw, bd, k_o, vt_o, ki_o):
    is_meta = pl.program_id(1) == 0
    k = jnp.where(is_meta, km[...], kx[...])
    v = jnp.where(is_meta, vm[...], vx[...])
    s = jnp.where(is_meta, sm[...], sx[...])
    k_o[...] = _head_rms(k, bd[...], knw[...]).astype(BF16)
    ones_rows = jnp.where(lax.broadcasted_iota(I32, (VROWS - DH_A, META_BLK), 0) == 0, 1.0, 0.0).astype(BF16)
    for j in range(W_A // 128):
        t = v[:, j * 128:(j + 1) * 128].T.astype(BF16)
        for r in range(2):
            base = (2 * j + r) * VROWS
            vt_o[base:base + DH_A, :] = t[r * DH_A:(r + 1) * DH_A]
            vt_o[base + DH_A:base + VROWS, :] = ones_rows
    ki_o[...] = s[:, S_KI:S_KI + D_IDX].astype(BF16)


def _prep_k(p_x, p_m, knw_row, bd):
    b, t, _ = p_x.shape
    nblk = t // META_BLK + 1
    tk = nblk * META_BLK

    def xmap(col):
        return lambda bi, i: (bi, jnp.maximum(i - 1, 0), col)

    return pl.pallas_call(
        _prep_k_kernel,
        grid=(b, nblk),
        in_specs=[pl.BlockSpec((None, META_BLK, 512), xmap(C_KA // 512)),
                  pl.BlockSpec((None, META_BLK, 512), xmap(C_VA // 512)),
                  pl.BlockSpec((None, META_BLK, 128), xmap(C_SMALL // 128)),
                  pl.BlockSpec((META_BLK, 512), lambda bi, i: (0, C_KA // 512)),
                  pl.BlockSpec((META_BLK, 512), lambda bi, i: (0, C_VA // 512)),
                  pl.BlockSpec((META_BLK, 128), lambda bi, i: (0, C_SMALL // 128)),
                  pl.BlockSpec((1, 512), lambda bi, i: (0, 0)),
                  pl.BlockSpec((512, 512), lambda bi, i: (0, 0))],
        out_specs=[pl.BlockSpec((None, META_BLK, 512), lambda bi, i: (bi, i, 0)),
                   pl.BlockSpec((None, H_A * VROWS, META_BLK), lambda bi, i: (bi, 0, i)),
                   pl.BlockSpec((None, META_BLK, D_IDX), lambda bi, i: (bi, i, 0))],
        out_shape=[jax.ShapeDtypeStruct((b, tk, 512), BF16),
                   jax.ShapeDtypeStruct((b, H_A * VROWS, tk), BF16),
                   jax.ShapeDtypeStruct((b, tk, D_IDX), BF16)],
        compiler_params=_cparams(("arbitrary", "arbitrary")),
        name="prep_k",
    )(p_x, p_x, p_x, p_m, p_m, p_m, knw_row, bd)


def _prep_q_kernel(qx, qix, sx, qnw, bd, qtz_o, qit_o, wt_o):
    qn = _head_rms(qx[...], bd[...], qnw[...]) * (DH_A ** -0.5 * LOG2E)
    qi = qix[...]
    zeros = jnp.zeros((64, QB), BF16)
    for j in range(W_A // 128):
        t = qn[:, j * 128:(j + 1) * 128].T.astype(BF16)
        base = 2 * j * 128
        qtz_o[base:base + 64, :] = t[0:64]
        qtz_o[base + 64:base + 128, :] = zeros
        qtz_o[base + 128:base + 192, :] = zeros
        qtz_o[base + 192:base + 256, :] = t[64:128]
        qit_o[j * 128:(j + 1) * 128, :] = qi[:, j * 128:(j + 1) * 128].T.astype(BF16)
    st = sx[...].T
    wt_o[...] = st[S_WI:S_WI + H_IDX, :] * ((H_IDX ** -0.5) * (D_IDX ** -0.5))


def _prep_q(p_x, qnw_row, bd):
    b, t, _ = p_x.shape
    return pl.pallas_call(
        _prep_q_kernel,
        grid=(b, t // QB),
        in_specs=[pl.BlockSpec((None, QB, 512), lambda bi, i: (bi, i, C_QA // 512)),
                  pl.BlockSpec((None, QB, 512), lambda bi, i: (bi, i, C_QI // 512)),
                  pl.BlockSpec((None, QB, 128), lambda bi, i: (bi, i, C_SMALL // 128)),
                  pl.BlockSpec((1, 512), lambda bi, i: (0, 0)),
                  pl.BlockSpec((512, 512), lambda bi, i: (0, 0))],
        out_specs=[pl.BlockSpec((None, 2 * 512, QB), lambda bi, i: (bi, 0, i)),
                   pl.BlockSpec((None, 512, QB), lambda bi, i: (bi, 0, i)),
                   pl.BlockSpec((None, H_IDX, QB), lambda bi, i: (bi, 0, i))],
        out_shape=[jax.ShapeDtypeStruct((b, 2 * 512, t), BF16),
                   jax.ShapeDtypeStruct((b, 512, t), BF16),
                   jax.ShapeDtypeStruct((b, H_IDX, t), F32)],
        compiler_params=_cparams(("arbitrary", "arbitrary")),
        name="prep_q",
    )(p_x, p_x, p_x, qnw_row, bd)


def _bias_kernel(rb_ref, o_ref):
    a = lax.broadcasted_iota(I32, (NEAR, QB), 1)
    bk = lax.broadcasted_iota(I32, (NEAR, QB), 0)
    d = jnp.maximum(a - bk + (NEAR - QB), 0)
    max_exact = N_BUCKETS // 2
    ratio = jnp.maximum(d, max_exact).astype(F32) / max_exact
    large = max_exact + (jnp.log(ratio) / math.log(MAX_DISTANCE / max_exact)
                         * (N_BUCKETS - max_exact)).astype(I32)
    large = jnp.minimum(large, N_BUCKETS - 1)
    bucket = jnp.where(d < max_exact, d, large)
    for h in range(H_A):
        acc = jnp.zeros((NEAR, QB), F32)
        for bb in range(N_BUCKETS):
            acc = jnp.where(bucket == bb, rb_ref[bb, h], acc)
        o_ref[h] = (acc - rb_ref[N_BUCKETS - 1, h]) * LOG2E


def _bias_tile(rel_bias):
    return pl.pallas_call(
        _bias_kernel,
        in_specs=[pl.BlockSpec(memory_space=pltpu.SMEM)],
        out_specs=pl.BlockSpec((H_A, NEAR, QB), lambda: (0, 0, 0)),
        out_shape=jax.ShapeDtypeStruct((H_A, NEAR, QB), F32),
        name="bias_tile",
    )(rel_bias)


def _attn_kernel(qtz_ref, qit_ref, wt_ref, k_ref, vt_ref, ki_ref, bias_ref, o_ref,
                 keys_ref, hi_ref, lo_ref, m_ref, acc_ref, j_ref):
    iq = pl.program_id(1)
    near0 = pl.multiple_of(iq * QB, QB)
    q_pos = META_BLK + iq * QB + lax.broadcasted_iota(I32, (1, QB), 1)

    def score_chunk(start, size, mask_pad, mask_causal):
        kic = ki_ref[pl.ds(start, size), :]
        acc = jnp.zeros((size, QB), F32)
        for h in range(H_IDX):
            r = jnp.dot(kic, qit_ref[h * D_IDX:(h + 1) * D_IDX, :], preferred_element_type=F32)
            acc = acc + jnp.maximum(r, 0.0) * wt_ref[h:h + 1, :]
        acc = acc + 0.0
        bits = pltpu.bitcast(acc, I32)
        key = jnp.where(bits < 0, bits ^ jnp.int32(0x7FFFFFFF), bits)
        if mask_pad or mask_causal:
            pos = start + lax.broadcasted_iota(I32, (size, QB), 0)
            if mask_causal:
                key = jnp.where(pos <= q_pos, key, INT_MIN)
            if mask_pad:
                key = jnp.where(pos >= META_PAD, key, INT_MIN)
        keys_ref[pl.ds(start, size), :] = key
        hi_ref[pl.ds(start, size), :] = lax.shift_right_arithmetic(key, 16).astype(jnp.int16)
        lo_ref[pl.ds(start, size), :] = ((key & 0xFFFF) - 2 ** 15).astype(jnp.int16)

    n_big = iq // 2
    odd0 = pl.multiple_of(n_big * 2 * KC, 2 * KC)

    @pl.when(n_big > 0)
    def _():
        score_chunk(0, 2 * KC, True, False)

    def far_scores(c, carry):
        score_chunk(pl.multiple_of(c * 2 * KC, 2 * KC), 2 * KC, False, False)
        return carry

    lax.fori_loop(1, n_big, far_scores, 0)

    @pl.when(iq % 2 == 1)
    def _():
        score_chunk(odd0, KC, True, False)

    score_chunk(near0, NEAR, True, True)

    n256 = iq + NEAR // 256
    tail0 = pl.multiple_of(n256 * 256, 128)

    def count(ref, pred):
        def part(start, size):
            o = pred(ref[pl.ds(start, size), :], start).reshape(size // 32, 32, QB)
            parts = [o[i] for i in range(size // 32)]
            while len(parts) > 1:
                parts = [parts[i] + parts[i + 1] for i in range(0, len(parts), 2)]
            return parts[0]

        def body(c, acc):
            return acc + part(pl.multiple_of(c * 256, 256), 256)
        acc = lax.fori_loop(0, n256, body, jnp.zeros((32, QB), ref.dtype))
        return jnp.sum((acc + part(tail0, NEAR % 256)).astype(I32), axis=0, keepdims=True)

    one16 = jnp.int16(1)
    zero16 = jnp.int16(0)

    def half_step(ref):
        def step(it, t):
            cand = t + lax.shift_left(jnp.int32(1), 15 - it)
            c16 = cand.astype(jnp.int16)
            cnt = count(ref, lambda blk, start: jnp.where(blk >= c16, one16, zero16))
            return jnp.where(cnt >= TOPK, cand, t)
        return step

    half_min = jnp.full((1, QB), -2 ** 15, I32)
    tau_hi = lax.fori_loop(0, 16, half_step(hi_ref), half_min)
    th16 = tau_hi.astype(jnp.int16)

    def fold_low(start, size):
        hi = hi_ref[pl.ds(start, size), :]
        lo = lo_ref[pl.ds(start, size), :]
        lo_ref[pl.ds(start, size), :] = jnp.where(hi > th16, jnp.int16(2 ** 15 - 1),
                                                  jnp.where(hi < th16, jnp.int16(-2 ** 15), lo))

    def fold_body(c, carry):
        fold_low(pl.multiple_of(c * 256, 256), 256)
        return carry

    lax.fori_loop(0, n256, fold_body, 0)
    fold_low(tail0, NEAR % 256)
    tau_lo = lax.fori_loop(0, 16, half_step(lo_ref), half_min)
    tau = lax.shift_left(tau_hi, 16) + (tau_lo + 2 ** 15)

    def count32(pred):
        return count(keys_ref, pred)

    cnt_ge = count32(lambda blk, start: jnp.where(blk >= tau, 1, 0))
    excess = jnp.where(tau > INT_MIN, jnp.where(cnt_ge > TOPK, 1, 0), 0).astype(I32)
    j_ref[...] = jnp.where(tau > INT_MIN, jnp.int32(2 ** 30), jnp.int32(-1))

    @pl.when(jnp.max(excess) > 0)
    def _():
        need = TOPK - count32(lambda blk, start: jnp.where(blk > tau, 1, 0))

        def pos_step(it, lo):
            cand = lo + lax.shift_left(jnp.int32(1), 12 - it)

            def pred(blk, start):
                pos = start + lax.broadcasted_iota(I32, blk.shape, 0)
                return jnp.where(pos < cand, jnp.where(blk == tau, 1, 0), 0)
            return jnp.where(count32(pred) < need, cand, lo)

        lo = lax.fori_loop(0, 13, pos_step, jnp.zeros((1, QB), I32))
        j_ref[...] = jnp.where(excess > 0, lo, j_ref[...])

    j_last = j_ref[...]

    m_ref[...] = jnp.full(m_ref.shape, NEG, F32)
    acc_ref[...] = jnp.zeros(acc_ref.shape, F32)

    def attend(start, size, near):
        key = keys_ref[pl.ds(start, size), :]
        pos = start + lax.broadcasted_iota(I32, (size, QB), 0)
        tie = jnp.where(pos <= j_last, 0.0, NEG)
        madd = jnp.where(key > tau, 0.0, jnp.where(key == tau, tie, NEG))
        hs = range(H_A)
        s = [jnp.dot(k_ref[pl.ds(start, size), (h // 2) * 128:(h // 2 + 1) * 128],
                     qtz_ref[h * 128:(h + 1) * 128, :], preferred_element_type=F32) + madd for h in hs]
        if near:
            s = [s[h] + bias_ref[h] for h in hs]
        m_all = m_ref[...]
        m_new = [jnp.maximum(m_all[h:h + 1, :], jnp.max(s[h], axis=0, keepdims=True)) for h in hs]
        m_new_all = jnp.concatenate(m_new, axis=0)
        alpha_all = jnp.exp2(m_all - m_new_all)
        m_ref[...] = m_new_all
        p = [jnp.exp2(s[h] - m_new[h]).astype(BF16) for h in hs]
        pv = [jnp.dot(vt_ref[h * VROWS:(h + 1) * VROWS, pl.ds(start, size)], p[h],
                      preferred_element_type=F32) for h in hs]
        for h in hs:
            acc_ref[h * VROWS:(h + 1) * VROWS, :] = (alpha_all[h:h + 1, :] * acc_ref[h * VROWS:(h + 1) * VROWS, :]
                                                     + pv[h])

    def far_attend(c, carry):
        attend(pl.multiple_of(c * 2 * KC, 2 * KC), 2 * KC, False)
        return carry

    lax.fori_loop(0, n_big, far_attend, 0)

    @pl.when(iq % 2 == 1)
    def _():
        attend(odd0, KC, False)

    attend(near0, NEAR, True)

    for j in range(W_A // 128):
        parts = []
        for h in (2 * j, 2 * j + 1):
            inv = 1.0 / acc_ref[h * VROWS + DH_A:h * VROWS + DH_A + 1, :]
            parts.append(acc_ref[h * VROWS:h * VROWS + DH_A, :] * inv)
        o_ref[:, j * 128:(j + 1) * 128] = jnp.concatenate(parts, axis=0).T


def _attention(qtz, qit, wt, k, vt, ki, bias):
    b, _, t = qit.shape
    tk = k.shape[1]
    return pl.pallas_call(
        _attn_kernel,
        grid=(b, t // QB),
        in_specs=[pl.BlockSpec((None, 2 * 512, QB), lambda bi, i: (bi, 0, i)),
                  pl.BlockSpec((None, 512, QB), lambda bi, i: (bi, 0, i)),
                  pl.BlockSpec((None, H_IDX, QB), lambda bi, i: (bi, 0, i)),
                  pl.BlockSpec((None, tk, 512), lambda bi, i: (bi, 0, 0)),
                  pl.BlockSpec((None, H_A * VROWS, tk), lambda bi, i: (bi, 0, 0)),
                  pl.BlockSpec((None, tk, D_IDX), lambda bi, i: (bi, 0, 0)),
                  pl.BlockSpec((H_A, NEAR, QB), lambda bi, i: (0, 0, 0))],
        out_specs=pl.BlockSpec((None, QB, W_A), lambda bi, i: (bi, i, 0)),
        out_shape=jax.ShapeDtypeStruct((b, t, W_A), F32),
        scratch_shapes=[pltpu.VMEM((tk, QB), I32),
                        pltpu.VMEM((tk, QB), jnp.int16),
                        pltpu.VMEM((tk, QB), jnp.int16),
                        pltpu.VMEM((H_A, QB), F32),
                        pltpu.VMEM((H_A * VROWS, QB), F32),
                        pltpu.VMEM((1, QB), I32)],
        compiler_params=_cparams(("arbitrary", "arbitrary")),
        name="attn",
    )(qtz, qit, wt, k, vt, ki, bias)


def _dot_hi(a, b):
    return jnp.dot(a, b, precision=HI, preferred_element_type=F32)


def _mm(a, b):
    return jnp.dot(a.astype(BF16), b.astype(BF16), preferred_element_type=F32)


def _conv_silu(x_in, xbuf, convw):
    r = x_in.shape[0]
    xbuf[8:8 + r, :] = x_in
    y = convw[CONV_K - 1:CONV_K, :] * x_in
    for i in range(CONV_K - 1):
        off = 8 - (CONV_K - 1) + i
        y = y + convw[i:i + 1, :] * xbuf[off:off + r, :]
    return y * jax.nn.sigmoid(y)


def _gates(sm, alog, dtb):
    z = sm + dtb
    softplus = jnp.maximum(z, 0.0) + jnp.log(1.0 + jnp.exp(-jnp.abs(z)))
    return -jnp.exp(alog) * softplus, jax.nn.sigmoid(sm)


def _gdn_local(y, g, beta, n_chunks):
    ri = lax.broadcasted_iota(I32, (CHUNK, CHUNK), 0)
    ci = lax.broadcasted_iota(I32, (CHUNK, CHUNK), 1)
    incl = ri >= ci
    strict = ri > ci
    tri = jnp.where(incl, 1.0, 0.0).astype(F32)
    tri_t = jnp.where(ri <= ci, 1.0, 0.0).astype(F32)
    eye = jnp.where(ri == ci, 1.0, 0.0).astype(F32)
    pairs = [(c, h) for c in range(n_chunks) for h in range(H_B)]
    idx = range(len(pairs))
    gc_col, gc_row = [], []
    for c in range(n_chunks):
        g_c = g[c * CHUNK:(c + 1) * CHUNK]
        gc_col.append(_dot_hi(tri, g_c))
        gc_row.append(_dot_hi(g_c.T, tri_t))
    q, kt, vb, qe, decay, kdt, glast = [], [], [], [], [], [], []
    for c, h in pairs:
        yc = y[c * CHUNK:(c + 1) * CHUNK]
        qh = yc[:, h * DK_B:(h + 1) * DK_B]
        kh = yc[:, W_B + h * DK_B:W_B + (h + 1) * DK_B]
        vh = yc[:, 2 * W_B + h * DV_B:2 * W_B + (h + 1) * DV_B]
        qh = qh * lax.rsqrt(jnp.sum(qh * qh, axis=-1, keepdims=True) + EPS) * (DK_B ** -0.5)
        kh = kh * lax.rsqrt(jnp.sum(kh * kh, axis=-1, keepdims=True) + EPS)
        gcol = gc_col[c][:, S_AB + h:S_AB + h + 1]
        grow = gc_row[c][S_AB + h:S_AB + h + 1, :]
        bcol = beta[c * CHUNK:(c + 1) * CHUNK, S_BB + h:S_BB + h + 1]
        g_last = gcol[CHUNK - 1:CHUNK, :]
        egc = jnp.exp(gcol)
        kth = kh.T
        kb = kh * bcol
        q.append(jnp.concatenate([kb, qh], axis=0))
        kt.append(kth)
        vb.append(jnp.concatenate([vh * bcol, kb * egc], axis=1))
        qe.append(qh * egc)
        decay.append(jnp.exp(jnp.where(incl, gcol - grow, -jnp.inf)))
        kdt.append(kth * jnp.exp(g_last - grow))
        glast.append(jnp.exp(g_last))
    qk = [_mm(q[i], kt[i]) for i in idx]
    a = [jnp.where(strict, qk[i][0:CHUNK] * decay[i], 0.0) for i in idx]
    att = [qk[i][CHUNK:2 * CHUNK] * decay[i] for i in idx]
    tm = [eye - a[i] for i in idx]
    pw = a
    for _ in range(5):
        pw = [_mm(pw[i], pw[i]) for i in idx]
        tm = [tm[i] + _mm(tm[i], pw[i]) for i in idx]
    uw = [_mm(tm[i], vb[i]) for i in idx]
    return uw, qe, att, kdt, glast


def _gdn_scan(local, c, s_list):
    uw, qe, att, kdt, glast = local
    base = c * H_B
    hs = range(H_B)
    ws = [_mm(jnp.concatenate([uw[base + h][:, DV_B:], qe[base + h]], axis=0), s_list[h]) for h in hs]
    v_new = [uw[base + h][:, :DV_B] - ws[h][0:CHUNK] for h in hs]
    ov = [_mm(jnp.concatenate([att[base + h], kdt[base + h]], axis=0), v_new[h]) for h in hs]
    outs = [ws[h][CHUNK:2 * CHUNK] + ov[h][0:CHUNK] for h in hs]
    s_out = [s_list[h] * glast[base + h] + ov[h][CHUNK:CHUNK + DK_B] for h in hs]
    return outs, s_out


def _gdn_meta_kernel(qkv_m, sm_m, convw, alog, dtb, s_o, xbuf):
    xbuf[0:8, :] = jnp.zeros((8, 3 * W_B), F32)
    y = _conv_silu(qkv_m[...], xbuf, convw)
    g, beta = _gates(sm_m[...], alog[...], dtb[...])
    live = lax.broadcasted_iota(I32, (CHUNK, 128), 0) >= CHUNK - N_META
    g = jnp.where(live, g, 0.0)
    beta = jnp.where(live, beta, 0.0)
    _, s_new = _gdn_scan(_gdn_local(y, g, beta, 1), 0, [jnp.zeros((DK_B, DV_B), F32)] * H_B)
    for h in range(H_B):
        s_o[h] = s_new[h]


def _gdn_kernel(qkv_x, zb_x, sm_x, tail_m, s0, convw, alog, dtb, gnw, o_ref, xbuf, s_ref):
    @pl.when(pl.program_id(1) == 0)
    def _():
        xbuf[0:8, :] = tail_m[...]
        s_ref[...] = s0[...]

    x_in = qkv_x[...]
    r = x_in.shape[0]
    y = _conv_silu(x_in, xbuf, convw)
    xbuf[0:8, :] = x_in[r - 8:r, :]
    g, beta = _gates(sm_x[...], alog[...], dtb[...])
    local = _gdn_local(y, g, beta, r // CHUNK)
    s_list = [s_ref[h] for h in range(H_B)]
    for c in range(r // CHUNK):
        rows = slice(c * CHUNK, (c + 1) * CHUNK)
        outs, s_list = _gdn_scan(local, c, s_list)
        for h in range(H_B):
            o = outs[h]
            on = o * lax.rsqrt(jnp.mean(o * o, axis=-1, keepdims=True) + EPS) * gnw[...]
            zb = zb_x[rows, h * DV_B:(h + 1) * DV_B]
            o_ref[rows, h * DV_B:(h + 1) * DV_B] = (on * (zb * jax.nn.sigmoid(zb))).astype(BF16)
    for h in range(H_B):
        s_ref[h] = s_list[h]


def _gdn(p_x, p_m, convw, alog_row, dtb_row, gnw_row):
    b, t, _ = p_x.shape
    r = CHUNK * GDN_CHUNKS_PER_STEP
    last = META_BLK // CHUNK - 1
    s0 = pl.pallas_call(
        _gdn_meta_kernel,
        grid=(1,),
        in_specs=[pl.BlockSpec((CHUNK, 3 * W_B), lambda i: (last, 0)),
                  pl.BlockSpec((CHUNK, 128), lambda i: (last, C_SMALL // 128)),
                  pl.BlockSpec((CONV_K, 3 * W_B), lambda i: (0, 0)),
                  pl.BlockSpec((1, 128), lambda i: (0, 0)),
                  pl.BlockSpec((1, 128), lambda i: (0, 0))],
        out_specs=pl.BlockSpec((H_B, DK_B, DV_B), lambda i: (0, 0, 0)),
        out_shape=jax.ShapeDtypeStruct((H_B, DK_B, DV_B), F32),
        scratch_shapes=[pltpu.VMEM((8 + CHUNK, 3 * W_B), F32)],
        compiler_params=_cparams(("arbitrary",)),
        name="gdn_meta",
    )(p_m, p_m, convw, alog_row, dtb_row)
    return pl.pallas_call(
        _gdn_kernel,
        grid=(b, t // r),
        in_specs=[pl.BlockSpec((None, r, 3 * W_B), lambda bi, n: (bi, n, 0)),
                  pl.BlockSpec((None, r, W_B), lambda bi, n: (bi, n, C_ZB // W_B)),
                  pl.BlockSpec((None, r, 128), lambda bi, n: (bi, n, C_SMALL // 128)),
                  pl.BlockSpec((8, 3 * W_B), lambda bi, n: (META_BLK // 8 - 1, 0)),
                  pl.BlockSpec((H_B, DK_B, DV_B), lambda bi, n: (0, 0, 0)),
                  pl.BlockSpec((CONV_K, 3 * W_B), lambda bi, n: (0, 0)),
                  pl.BlockSpec((1, 128), lambda bi, n: (0, 0)),
                  pl.BlockSpec((1, 128), lambda bi, n: (0, 0)),
                  pl.BlockSpec((1, DV_B), lambda bi, n: (0, 0))],
        out_specs=pl.BlockSpec((None, r, W_B), lambda bi, n: (bi, n, 0)),
        out_shape=jax.ShapeDtypeStruct((b, t, W_B), BF16),
        scratch_shapes=[pltpu.VMEM((8 + r, 3 * W_B), F32),
                        pltpu.VMEM((H_B, DK_B, DV_B), F32)],
        compiler_params=_cparams(("arbitrary", "arbitrary")),
        name="gdn",
    )(p_x, p_x, p_x, p_m, s0, convw, alog_row, dtb_row, gnw_row)


def _merge_kernel(ya_ref, za_ref, og_ref, ga_ref, gb_ref, x_ref, wpa, wpb, wout, o_ref):
    za = za_ref[...]
    ya = (ya_ref[...] * (za * jax.nn.sigmoid(za))).astype(BF16)
    ya = jnp.dot(ya, wpa[...], preferred_element_type=F32)
    yb = jnp.dot(og_ref[...], wpb[...], preferred_element_type=F32)
    m = jax.nn.sigmoid(ga_ref[...]) * ya + jax.nn.sigmoid(gb_ref[...]) * yb
    o_ref[...] = x_ref[...] + jnp.dot(m.astype(BF16), wout[...], preferred_element_type=F32)


def _merge(ya, p2d, og, x2d, wpa, wpb, wout, tm):
    m = x2d.shape[0]
    return pl.pallas_call(
        _merge_kernel,
        grid=(m // tm,),
        in_specs=[pl.BlockSpec((tm, W_A), lambda i: (i, 0)),
                  pl.BlockSpec((tm, W_A), lambda i: (i, C_ZA // W_A)),
                  pl.BlockSpec((tm, W_B), lambda i: (i, 0)),
                  pl.BlockSpec((tm, D_MODEL), lambda i: (i, C_GA // D_MODEL)),
                  pl.BlockSpec((tm, D_MODEL), lambda i: (i, C_GB // D_MODEL)),
                  pl.BlockSpec((tm, D_MODEL), lambda i: (i, 0)),
                  pl.BlockSpec((W_A, D_MODEL), lambda i: (0, 0)),
                  pl.BlockSpec((W_B, D_MODEL), lambda i: (0, 0)),
                  pl.BlockSpec((D_MODEL, D_MODEL), lambda i: (0, 0))],
        out_specs=pl.BlockSpec((tm, D_MODEL), lambda i: (i, 0)),
        out_shape=jax.ShapeDtypeStruct((m, D_MODEL), F32),
        compiler_params=_cparams(("arbitrary",)),
        name="merge",
    )(ya, p2d, og, p2d, p2d, x2d, wpa, wpb, wout)


def _permute_w_in(w):
    sizes = (W_A, W_A, W_A, W_A, H_IDX * D_IDX, D_IDX, H_IDX,
             H_B * DK_B, H_B * DK_B, W_B, W_B, H_B, H_B, D_MODEL, D_MODEL)
    offs = [0]
    for s in sizes:
        offs.append(offs[-1] + s)
    (qa, ka, va, za, qi, ki, wi, qb, kb, vb, zb, bb, ab, ga, gb) = [
        w[:, offs[i]:offs[i + 1]] for i in range(len(sizes))]
    pad = jnp.zeros((w.shape[0], 128 - (D_IDX + H_IDX + 2 * H_B)), w.dtype)
    return jnp.concatenate([qb, kb, vb, zb, ga, gb, qa, ka, va, za, qi, ki, wi, bb, ab, pad], axis=1)


def _lane_row(vals, offset):
    return jnp.zeros((1, 128), F32).at[0, offset:offset + vals.shape[0]].set(vals.astype(F32))


def _layer(x, meta_tokens, rel_bias, norm_w, w_in, q_norm_w, k_norm_w, conv_w, a_log, dt_bias,
           gdn_norm_w, w_proj_a, w_proj_b, w_out):
    b, t, _ = x.shape
    w_p = _permute_w_in(w_in).astype(BF16)
    nw_row = norm_w.reshape(1, D_MODEL)
    x2d = x.reshape(b * t, D_MODEL)
    meta_pad = jnp.zeros((META_BLK, D_MODEL), F32).at[META_PAD:].set(meta_tokens)

    p_x2d = _inproj(x2d, nw_row, w_p, 512)
    p_m = _inproj(meta_pad, nw_row, w_p, META_BLK)
    p_x = p_x2d.reshape(b, t, N_P)

    hid = lax.broadcasted_iota(I32, (W_A, W_A), 0) // DH_A
    bd = (hid == hid.T).astype(BF16)
    k_n, v_t, k_i = _prep_k(p_x, p_m, jnp.tile(k_norm_w, H_A).reshape(1, W_A), bd)
    q_tz, q_it, w_t = _prep_q(p_x, jnp.tile(q_norm_w, H_A).reshape(1, W_A), bd)
    bias = _bias_tile(rel_bias)
    ya = _attention(q_tz, q_it, w_t, k_n, v_t, k_i, bias)

    og = _gdn(p_x, p_m, conv_w, _lane_row(a_log, S_AB), _lane_row(dt_bias, S_AB),
              gdn_norm_w.reshape(1, DV_B))

    out = _merge(ya.reshape(b * t, W_A), p_x2d, og.reshape(b * t, W_B), x2d,
                 w_proj_a.astype(BF16), w_proj_b.astype(BF16), w_out.astype(BF16), 512)
    return out.reshape(b, t, D_MODEL)


def kernel(x, meta_tokens, rel_bias, norm_w, w_in, q_norm_w, k_norm_w, conv_w, a_log, dt_bias,
           gdn_norm_w, w_proj_a, w_proj_b, w_out):
    depth = norm_w.shape[0]
    assert depth == 1, "meta rows are dropped after the layer; deeper stacks need them carried"
    return _layer(x, meta_tokens, rel_bias, norm_w[0], w_in[0], q_norm_w[0], k_norm_w[0], conv_w[0],
                  a_log[0], dt_bias[0], gdn_norm_w[0], w_proj_a[0], w_proj_b[0], w_out[0])
```

```python
import math

import jax
import jax.numpy as jnp
from jax import lax
from jax.experimental import pallas as pl
from jax.experimental.pallas import tpu as pltpu

F32 = jnp.float32
BF16 = jnp.bfloat16
I32 = jnp.int32
I16 = jnp.int16

D_MODEL = 1024
N_META = 16
H_A = 8
DH_A = 64
W_A = H_A * DH_A
H_IDX = 8
D_IDX = 64
TOPK = 256
N_BUCKETS = 32
MAX_DISTANCE = 128
H_B = 8
DK_B = 128
DV_B = 128
W_B = H_B * DV_B
CONV_K = 4
CHUNK = 64
EPS = 1e-6

N_MAIN = 6 * 1024
C_ZB, C_GA, C_GB = 3, 4, 5
A_Q, A_K, A_V, A_Z, A_QI, A_S = 0, 512, 1024, 1536, 2048, 2560
N_ATTN = A_S + 128
S_KI, S_WI, S_BB, S_AB = 0, 64, 72, 80

TILE = 512
QB = 256
KC = 256
NEAR = 2 * QB
FAR0 = TILE - QB
PAD_END = TILE - N_META
VROWS = DH_A + 16
LOG2E = math.log2(math.e)
INT_MIN = -2 ** 31
NEG = -1e30
VMEM_LIMIT = 58 * 1024 * 1024
HI = lax.Precision.HIGHEST
GDN_CHUNKS_PER_STEP = 4


def _cparams(sem):
    return pltpu.CompilerParams(dimension_semantics=sem, vmem_limit_bytes=VMEM_LIMIT)


def _rms_bf16(x, nw):
    ms = jnp.mean(x * x, axis=-1, keepdims=True)
    return (x * lax.rsqrt(ms + EPS) * nw).astype(BF16)


def _inproj_kernel(x_ref, nw_ref, w_ref, o_ref):
    o_ref[...] = jnp.dot(_rms_bf16(x_ref[...], nw_ref[...]), w_ref[...], preferred_element_type=F32)


def _inproj(x2d, norm_w_row, w_bf16):
    m = x2d.shape[0]
    n = w_bf16.shape[1]
    tn = n // 2
    return pl.pallas_call(
        _inproj_kernel,
        grid=(2, m // TILE),
        in_specs=[pl.BlockSpec((TILE, D_MODEL), lambda j, i: (i, 0)),
                  pl.BlockSpec((1, D_MODEL), lambda j, i: (0, 0)),
                  pl.BlockSpec((D_MODEL, tn), lambda j, i: (0, j))],
        out_specs=pl.BlockSpec((TILE, tn), lambda j, i: (i, j)),
        out_shape=jax.ShapeDtypeStruct((m, n), F32),
        compiler_params=_cparams(("arbitrary", "arbitrary")),
        name="inproj",
    )(x2d, norm_w_row, w_bf16)


def _head_rms(x, bd, w_row):
    sq = x * x
    hi = sq.astype(BF16)
    lo = (sq - hi.astype(F32)).astype(BF16)
    ss = (jnp.dot(hi, bd, preferred_element_type=F32) + jnp.dot(lo, bd, preferred_element_type=F32))
    return x * lax.rsqrt(ss * (1.0 / DH_A) + EPS) * w_row


def _attn_proj_kernel(x_ref, xm_ref, nw_ref, w_ref, qnw, knw, bd_ref,
                      za_o, sm_o, k_o, vt_o, ki_o, qtz_o, qit_o, wt_o):
    x = jnp.where(pl.program_id(1) == 0, xm_ref[...], x_ref[...])
    h = _rms_bf16(x, nw_ref[...])
    bd = bd_ref[...]

    def proj(c0, width):
        return jnp.dot(h, w_ref[:, c0:c0 + width], preferred_element_type=F32)

    sm = proj(A_S, 128)
    sm_o[...] = sm
    ki_o[...] = sm[:, S_KI:S_KI + D_IDX].astype(BF16)
    wt_o[...] = sm.T[S_WI:S_WI + H_IDX, :] * ((H_IDX ** -0.5) * (D_IDX ** -0.5))

    qi = proj(A_QI, W_A)
    for j in range(W_A // 128):
        qit_o[j * 128:(j + 1) * 128, :] = qi[:, j * 128:(j + 1) * 128].T.astype(BF16)

    qn = _head_rms(proj(A_Q, W_A), bd, qnw[...]) * (DH_A ** -0.5 * LOG2E)
    zeros = jnp.zeros((DH_A, TILE), BF16)
    for j in range(W_A // 128):
        t = qn[:, j * 128:(j + 1) * 128].T.astype(BF16)
        base = 2 * j * 128
        qtz_o[base:base + 64, :] = t[0:64]
        qtz_o[base + 64:base + 128, :] = zeros
        qtz_o[base + 128:base + 192, :] = zeros
        qtz_o[base + 192:base + 256, :] = t[64:128]

    k_o[...] = _head_rms(proj(A_K, W_A), bd, knw[...]).astype(BF16)

    v = proj(A_V, W_A)
    ones_rows = jnp.where(lax.broadcasted_iota(I32, (VROWS - DH_A, TILE), 0) == 0, 1.0, 0.0).astype(BF16)
    for j in range(W_A // 128):
        t = v[:, j * 128:(j + 1) * 128].T.astype(BF16)
        for r in range(2):
            base = (2 * j + r) * VROWS
            vt_o[base:base + DH_A, :] = t[r * DH_A:(r + 1) * DH_A]
            vt_o[base + DH_A:base + VROWS, :] = ones_rows

    za_o[...] = proj(A_Z, W_A)


def _attn_proj(x, meta_pad, norm_w_row, w_bf16, qnw_row, knw_row, bd):
    b, t, _ = x.shape
    nt = t // TILE + 1
    tk = nt * TILE

    def rows(width, dtype):
        return (pl.BlockSpec((None, TILE, width), lambda bi, i: (bi, i, 0)),
                jax.ShapeDtypeStruct((b, tk, width), dtype))

    def cols(height, dtype):
        return (pl.BlockSpec((None, height, TILE), lambda bi, i: (bi, 0, i)),
                jax.ShapeDtypeStruct((b, height, tk), dtype))

    outs = [rows(W_A, F32), rows(128, F32), rows(W_A, BF16), cols(H_A * VROWS, BF16), rows(D_IDX, BF16),
            cols(2 * W_A, BF16), cols(W_A, BF16), cols(H_IDX, F32)]
    const = lambda bi, i: (0, 0)
    return pl.pallas_call(
        _attn_proj_kernel,
        grid=(b, nt),
        in_specs=[pl.BlockSpec((None, TILE, D_MODEL), lambda bi, i: (bi, jnp.maximum(i - 1, 0), 0)),
                  pl.BlockSpec((TILE, D_MODEL), const),
                  pl.BlockSpec((1, D_MODEL), const),
                  pl.BlockSpec((D_MODEL, N_ATTN), const),
                  pl.BlockSpec((1, W_A), const),
                  pl.BlockSpec((1, W_A), const),
                  pl.BlockSpec((W_A, W_A), const)],
        out_specs=[o[0] for o in outs],
        out_shape=[o[1] for o in outs],
        compiler_params=_cparams(("arbitrary", "arbitrary")),
        name="attn_proj",
    )(x, meta_pad, norm_w_row, w_bf16, qnw_row, knw_row, bd)


def _bias_kernel(rb_ref, o_ref):
    a = lax.broadcasted_iota(I32, (NEAR, QB), 1)
    bk = lax.broadcasted_iota(I32, (NEAR, QB), 0)
    d = jnp.maximum(a - bk + (NEAR - QB), 0)
    max_exact = N_BUCKETS // 2
    ratio = jnp.maximum(d, max_exact).astype(F32) / max_exact
    large = max_exact + (jnp.log(ratio) / math.log(MAX_DISTANCE / max_exact)
                         * (N_BUCKETS - max_exact)).astype(I32)
    large = jnp.minimum(large, N_BUCKETS - 1)
    bucket = jnp.where(d < max_exact, d, large)
    for h in range(H_A):
        acc = jnp.zeros((NEAR, QB), F32)
        for bb in range(N_BUCKETS):
            acc = jnp.where(bucket == bb, rb_ref[bb, h], acc)
        o_ref[h] = (acc - rb_ref[N_BUCKETS - 1, h]) * LOG2E


def _bias_tile(rel_bias):
    return pl.pallas_call(
        _bias_kernel,
        in_specs=[pl.BlockSpec(memory_space=pltpu.SMEM)],
        out_specs=pl.BlockSpec((H_A, NEAR, QB), lambda: (0, 0, 0)),
        out_shape=jax.ShapeDtypeStruct((H_A, NEAR, QB), F32),
        name="bias_tile",
    )(rel_bias)


def _attn_kernel(qtz_ref, qit_ref, wt_ref, k_ref, vt_ref, ki_ref, bias_ref, o_ref,
                 keys_ref, hi_ref, lo_ref, m_ref, acc_ref, j_ref):
    iq = pl.program_id(1)
    near0 = pl.multiple_of(FAR0 + iq * QB, QB)
    q_pos = TILE + iq * QB + lax.broadcasted_iota(I32, (1, QB), 1)

    def score_chunk(start, size, mask_pad, mask_causal):
        kic = ki_ref[pl.ds(start, size), :]
        acc = jnp.zeros((size, QB), F32)
        for h in range(H_IDX):
            r = jnp.dot(kic, qit_ref[h * D_IDX:(h + 1) * D_IDX, :], preferred_element_type=F32)
            acc = acc + jnp.maximum(r, 0.0) * wt_ref[h:h + 1, :]
        acc = acc + 0.0
        bits = pltpu.bitcast(acc, I32)
        key = jnp.where(bits < 0, bits ^ jnp.int32(0x7FFFFFFF), bits)
        if mask_pad or mask_causal:
            pos = start + lax.broadcasted_iota(I32, (size, QB), 0)
            if mask_causal:
                key = jnp.where(pos <= q_pos, key, INT_MIN)
            if mask_pad:
                key = jnp.where(pos >= PAD_END, key, INT_MIN)
        keys_ref[pl.ds(start, size), :] = key
        hi_ref[pl.ds(start, size), :] = lax.shift_right_arithmetic(key, 16).astype(I16)
        lo_ref[pl.ds(start, size), :] = ((key & 0xFFFF) - 2 ** 15).astype(I16)

    n_big = iq // 2
    odd0 = pl.multiple_of(FAR0 + n_big * 2 * KC, KC)

    def big_start(c):
        return pl.multiple_of(FAR0 + c * 2 * KC, KC)

    @pl.when(n_big > 0)
    def _():
        score_chunk(FAR0, 2 * KC, True, False)

    def far_scores(c, carry):
        score_chunk(big_start(c), 2 * KC, False, False)
        return carry

    lax.fori_loop(1, n_big, far_scores, 0)

    @pl.when(iq % 2 == 1)
    def _():
        score_chunk(odd0, KC, True, False)

    score_chunk(near0, NEAR, True, True)

    n256 = iq + NEAR // 256

    def blk_start(c):
        return pl.multiple_of(FAR0 + c * 256, 256)

    def count(ref, pred):
        def body(c, acc):
            start = blk_start(c)
            o = pred(ref[pl.ds(start, 256), :], start).reshape(8, 32, QB)
            parts = [o[i] for i in range(8)]
            while len(parts) > 1:
                parts = [parts[i] + parts[i + 1] for i in range(0, len(parts), 2)]
            return acc + parts[0]
        acc = lax.fori_loop(0, n256, body, jnp.zeros((32, QB), ref.dtype))
        return jnp.sum(acc.astype(I32), axis=0, keepdims=True)

    one16 = jnp.int16(1)
    zero16 = jnp.int16(0)

    def half_step(ref):
        def step(it, t):
            cand = t + lax.shift_left(jnp.int32(1), 15 - it)
            c16 = cand.astype(I16)
            cnt = count(ref, lambda blk, start: jnp.where(blk >= c16, one16, zero16))
            return jnp.where(cnt >= TOPK, cand, t)
        return step

    half_min = jnp.full((1, QB), -2 ** 15, I32)
    tau_hi = lax.fori_loop(0, 16, half_step(hi_ref), half_min)
    th16 = tau_hi.astype(I16)

    def fold_body(c, carry):
        start = blk_start(c)
        hi = hi_ref[pl.ds(start, 256), :]
        lo = lo_ref[pl.ds(start, 256), :]
        lo_ref[pl.ds(start, 256), :] = jnp.where(hi > th16, jnp.int16(2 ** 15 - 1),
                                                 jnp.where(hi < th16, jnp.int16(-2 ** 15), lo))
        return carry

    lax.fori_loop(0, n256, fold_body, 0)
    tau_lo = lax.fori_loop(0, 16, half_step(lo_ref), half_min)
    tau = lax.shift_left(tau_hi, 16) + (tau_lo + 2 ** 15)

    def count32(pred):
        return count(keys_ref, pred)

    cnt_ge = count32(lambda blk, start: jnp.where(blk >= tau, 1, 0))
    excess = jnp.where(tau > INT_MIN, jnp.where(cnt_ge > TOPK, 1, 0), 0).astype(I32)
    j_ref[...] = jnp.where(tau > INT_MIN, jnp.int32(2 ** 30), jnp.int32(-1))

    @pl.when(jnp.max(excess) > 0)
    def _():
        need = TOPK - count32(lambda blk, start: jnp.where(blk > tau, 1, 0))

        def pos_step(it, lo):
            cand = lo + lax.shift_left(jnp.int32(1), 12 - it)

            def pred(blk, start):
                pos = start + lax.broadcasted_iota(I32, blk.shape, 0)
                return jnp.where(pos < cand, jnp.where(blk == tau, 1, 0), 0)
            return jnp.where(count32(pred) < need, cand, lo)

        lo = lax.fori_loop(0, 13, pos_step, jnp.zeros((1, QB), I32))
        j_ref[...] = jnp.where(excess > 0, lo, j_ref[...])

    j_last = j_ref[...]

    m_ref[...] = jnp.full(m_ref.shape, NEG, F32)
    acc_ref[...] = jnp.zeros(acc_ref.shape, F32)

    def attend(start, size, near):
        key = keys_ref[pl.ds(start, size), :]
        pos = start + lax.broadcasted_iota(I32, (size, QB), 0)
        tie = jnp.where(pos <= j_last, 0.0, NEG)
        madd = jnp.where(key > tau, 0.0, jnp.where(key == tau, tie, NEG))
        hs = range(H_A)
        s = [jnp.dot(k_ref[pl.ds(start, size), (h // 2) * 128:(h // 2 + 1) * 128],
                     qtz_ref[h * 128:(h + 1) * 128, :], preferred_element_type=F32) + madd for h in hs]
        if near:
            s = [s[h] + bias_ref[h] for h in hs]
        m_all = m_ref[...]
        m_new = [jnp.maximum(m_all[h:h + 1, :], jnp.max(s[h], axis=0, keepdims=True)) for h in hs]
        m_new_all = jnp.concatenate(m_new, axis=0)
        alpha_all = jnp.exp2(m_all - m_new_all)
        m_ref[...] = m_new_all
        p = [jnp.exp2(s[h] - m_new[h]).astype(BF16) for h in hs]
        pv = [jnp.dot(vt_ref[h * VROWS:(h + 1) * VROWS, pl.ds(start, size)], p[h],
                      preferred_element_type=F32) for h in hs]
        for h in hs:
            acc_ref[h * VROWS:(h + 1) * VROWS, :] = (alpha_all[h:h + 1, :] * acc_ref[h * VROWS:(h + 1) * VROWS, :]
                                                     + pv[h])

    def far_attend(c, carry):
        attend(big_start(c), 2 * KC, False)
        return carry

    lax.fori_loop(0, n_big, far_attend, 0)

    @pl.when(iq % 2 == 1)
    def _():
        attend(odd0, KC, False)

    attend(near0, NEAR, True)

    for j in range(W_A // 128):
        parts = []
        for h in (2 * j, 2 * j + 1):
            inv = 1.0 / acc_ref[h * VROWS + DH_A:h * VROWS + DH_A + 1, :]
            parts.append(acc_ref[h * VROWS:h * VROWS + DH_A, :] * inv)
        o_ref[:, j * 128:(j + 1) * 128] = jnp.concatenate(parts, axis=0).T


def _attention(qtz, qit, wt, k, vt, ki, bias, t):
    b, tk, _ = k.shape
    qoff = TILE // QB
    return pl.pallas_call(
        _attn_kernel,
        grid=(b, t // QB),
        in_specs=[pl.BlockSpec((None, 2 * W_A, QB), lambda bi, i: (bi, 0, i + qoff)),
                  pl.BlockSpec((None, W_A, QB), lambda bi, i: (bi, 0, i + qoff)),
                  pl.BlockSpec((None, H_IDX, QB), lambda bi, i: (bi, 0, i + qoff)),
                  pl.BlockSpec((None, tk, W_A), lambda bi, i: (bi, 0, 0)),
                  pl.BlockSpec((None, H_A * VROWS, tk), lambda bi, i: (bi, 0, 0)),
                  pl.BlockSpec((None, tk, D_IDX), lambda bi, i: (bi, 0, 0)),
                  pl.BlockSpec((H_A, NEAR, QB), lambda bi, i: (0, 0, 0))],
        out_specs=pl.BlockSpec((None, QB, W_A), lambda bi, i: (bi, i, 0)),
        out_shape=jax.ShapeDtypeStruct((b, t, W_A), F32),
        scratch_shapes=[pltpu.VMEM((tk, QB), I32),
                        pltpu.VMEM((tk, QB), I16),
                        pltpu.VMEM((tk, QB), I16),
                        pltpu.VMEM((H_A, QB), F32),
                        pltpu.VMEM((H_A * VROWS, QB), F32),
                        pltpu.VMEM((1, QB), I32)],
        compiler_params=_cparams(("arbitrary", "arbitrary")),
        name="attn",
    )(qtz, qit, wt, k, vt, ki, bias)


def _dot_hi(a, b):
    return jnp.dot(a, b, precision=HI, preferred_element_type=F32)


def _mm(a, b):
    return jnp.dot(a.astype(BF16), b.astype(BF16), preferred_element_type=F32)


def _conv_silu(x_in, xbuf, convw):
    r = x_in.shape[0]
    xbuf[8:8 + r, :] = x_in
    y = convw[CONV_K - 1:CONV_K, :] * x_in
    for i in range(CONV_K - 1):
        off = 8 - (CONV_K - 1) + i
        y = y + convw[i:i + 1, :] * xbuf[off:off + r, :]
    return y * jax.nn.sigmoid(y)


def _gates(sm, alog, dtb):
    z = sm + dtb
    softplus = jnp.maximum(z, 0.0) + jnp.log(1.0 + jnp.exp(-jnp.abs(z)))
    return -jnp.exp(alog) * softplus, jax.nn.sigmoid(sm)


def _gdn_local(y, g, beta, n_chunks):
    ri = lax.broadcasted_iota(I32, (CHUNK, CHUNK), 0)
    ci = lax.broadcasted_iota(I32, (CHUNK, CHUNK), 1)
    incl = ri >= ci
    strict = ri > ci
    tri = jnp.where(incl, 1.0, 0.0).astype(F32)
    tri_t = jnp.where(ri <= ci, 1.0, 0.0).astype(F32)
    eye = jnp.where(ri == ci, 1.0, 0.0).astype(F32)
    pairs = [(c, h) for c in range(n_chunks) for h in range(H_B)]
    idx = range(len(pairs))
    gc_col, gc_row = [], []
    for c in range(n_chunks):
        g_c = g[c * CHUNK:(c + 1) * CHUNK]
        gc_col.append(_dot_hi(tri, g_c))
        gc_row.append(_dot_hi(g_c.T, tri_t))
    q, kt, vb, qe, decay, kdt, glast = [], [], [], [], [], [], []
    for c, h in pairs:
        yc = y[c * CHUNK:(c + 1) * CHUNK]
        qh = yc[:, h * DK_B:(h + 1) * DK_B]
        kh = yc[:, W_B + h * DK_B:W_B + (h + 1) * DK_B]
        vh = yc[:, 2 * W_B + h * DV_B:2 * W_B + (h + 1) * DV_B]
        qh = qh * lax.rsqrt(jnp.sum(qh * qh, axis=-1, keepdims=True) + EPS) * (DK_B ** -0.5)
        kh = kh * lax.rsqrt(jnp.sum(kh * kh, axis=-1, keepdims=True) + EPS)
        gcol = gc_col[c][:, S_AB + h:S_AB + h + 1]
        grow = gc_row[c][S_AB + h:S_AB + h + 1, :]
        bcol = beta[c * CHUNK:(c + 1) * CHUNK, S_BB + h:S_BB + h + 1]
        g_last = gcol[CHUNK - 1:CHUNK, :]
        egc = jnp.exp(gcol)
        kth = kh.T
        kb = kh * bcol
        q.append(jnp.concatenate([kb, qh], axis=0))
        kt.append(kth)
        vb.append(jnp.concatenate([vh * bcol, kb * egc], axis=1))
        qe.append(qh * egc)
        decay.append(jnp.exp(jnp.where(incl, gcol - grow, -jnp.inf)))
        kdt.append(kth * jnp.exp(g_last - grow))
        glast.append(jnp.exp(g_last))
    qk = [_mm(q[i], kt[i]) for i in idx]
    a = [jnp.where(strict, qk[i][0:CHUNK] * decay[i], 0.0) for i in idx]
    att = [qk[i][CHUNK:2 * CHUNK] * decay[i] for i in idx]
    tm = [eye - a[i] for i in idx]
    pw = a
    for _ in range(5):
        pw = [_mm(pw[i], pw[i]) for i in idx]
        tm = [tm[i] + _mm(tm[i], pw[i]) for i in idx]
    uw = [_mm(tm[i], vb[i]) for i in idx]
    return uw, qe, att, kdt, glast


def _gdn_scan(local, c, s_list):
    uw, qe, att, kdt, glast = local
    base = c * H_B
    hs = range(H_B)
    ws = [_mm(jnp.concatenate([uw[base + h][:, DV_B:], qe[base + h]], axis=0), s_list[h]) for h in hs]
    v_new = [uw[base + h][:, :DV_B] - ws[h][0:CHUNK] for h in hs]
    ov = [_mm(jnp.concatenate([att[base + h], kdt[base + h]], axis=0), v_new[h]) for h in hs]
    outs = [ws[h][CHUNK:2 * CHUNK] + ov[h][0:CHUNK] for h in hs]
    s_out = [s_list[h] * glast[base + h] + ov[h][CHUNK:CHUNK + DK_B] for h in hs]
    return outs, s_out


def _gdn_meta_kernel(qkv_m, sm_m, convw, alog, dtb, s_o, xbuf):
    xbuf[0:8, :] = jnp.zeros((8, 3 * W_B), F32)
    y = _conv_silu(qkv_m[...], xbuf, convw)
    g, beta = _gates(sm_m[...], alog[...], dtb[...])
    live = lax.broadcasted_iota(I32, (CHUNK, 128), 0) >= CHUNK - N_META
    g = jnp.where(live, g, 0.0)
    beta = jnp.where(live, beta, 0.0)
    _, s_new = _gdn_scan(_gdn_local(y, g, beta, 1), 0, [jnp.zeros((DK_B, DV_B), F32)] * H_B)
    for h in range(H_B):
        s_o[h] = s_new[h]


def _gdn_kernel(qkv_x, zb_x, sm_x, tail_m, s0, convw, alog, dtb, gnw, o_ref, xbuf, s_ref):
    @pl.when(pl.program_id(1) == 0)
    def _():
        xbuf[0:8, :] = tail_m[...]
        s_ref[...] = s0[...]

    x_in = qkv_x[...]
    r = x_in.shape[0]
    y = _conv_silu(x_in, xbuf, convw)
    xbuf[0:8, :] = x_in[r - 8:r, :]
    g, beta = _gates(sm_x[...], alog[...], dtb[...])
    local = _gdn_local(y, g, beta, r // CHUNK)
    s_list = [s_ref[h] for h in range(H_B)]
    for c in range(r // CHUNK):
        rows = slice(c * CHUNK, (c + 1) * CHUNK)
        outs, s_list = _gdn_scan(local, c, s_list)
        for h in range(H_B):
            o = outs[h]
            on = o * lax.rsqrt(jnp.mean(o * o, axis=-1, keepdims=True) + EPS) * gnw[...]
            zb = zb_x[rows, h * DV_B:(h + 1) * DV_B]
            o_ref[rows, h * DV_B:(h + 1) * DV_B] = (on * (zb * jax.nn.sigmoid(zb))).astype(BF16)
    for h in range(H_B):
        s_ref[h] = s_list[h]


def _gdn(p_x, p_m, sm, convw, alog_row, dtb_row, gnw_row):
    b, t, _ = p_x.shape
    r = CHUNK * GDN_CHUNKS_PER_STEP
    last = TILE // CHUNK - 1
    s0 = pl.pallas_call(
        _gdn_meta_kernel,
        grid=(1,),
        in_specs=[pl.BlockSpec((CHUNK, 3 * W_B), lambda i: (last, 0)),
                  pl.BlockSpec((None, CHUNK, 128), lambda i: (0, last, 0)),
                  pl.BlockSpec((CONV_K, 3 * W_B), lambda i: (0, 0)),
                  pl.BlockSpec((1, 128), lambda i: (0, 0)),
                  pl.BlockSpec((1, 128), lambda i: (0, 0))],
        out_specs=pl.BlockSpec((H_B, DK_B, DV_B), lambda i: (0, 0, 0)),
        out_shape=jax.ShapeDtypeStruct((H_B, DK_B, DV_B), F32),
        scratch_shapes=[pltpu.VMEM((8 + CHUNK, 3 * W_B), F32)],
        compiler_params=_cparams(("arbitrary",)),
        name="gdn_meta",
    )(p_m, sm, convw, alog_row, dtb_row)
    return pl.pallas_call(
        _gdn_kernel,
        grid=(b, t // r),
        in_specs=[pl.BlockSpec((None, r, 3 * W_B), lambda bi, n: (bi, n, 0)),
                  pl.BlockSpec((None, r, W_B), lambda bi, n: (bi, n, C_ZB)),
                  pl.BlockSpec((None, r, 128), lambda bi, n: (bi, n + TILE // r, 0)),
                  pl.BlockSpec((8, 3 * W_B), lambda bi, n: (TILE // 8 - 1, 0)),
                  pl.BlockSpec((H_B, DK_B, DV_B), lambda bi, n: (0, 0, 0)),
                  pl.BlockSpec((CONV_K, 3 * W_B), lambda bi, n: (0, 0)),
                  pl.BlockSpec((1, 128), lambda bi, n: (0, 0)),
                  pl.BlockSpec((1, 128), lambda bi, n: (0, 0)),
                  pl.BlockSpec((1, DV_B), lambda bi, n: (0, 0))],
        out_specs=pl.BlockSpec((None, r, W_B), lambda bi, n: (bi, n, 0)),
        out_shape=jax.ShapeDtypeStruct((b, t, W_B), BF16),
        scratch_shapes=[pltpu.VMEM((8 + r, 3 * W_B), F32),
                        pltpu.VMEM((H_B, DK_B, DV_B), F32)],
        compiler_params=_cparams(("arbitrary", "arbitrary")),
        name="gdn",
    )(p_x, p_x, sm, p_m, s0, convw, alog_row, dtb_row, gnw_row)


def _merge_kernel(ya_ref, za_ref, og_ref, ga_ref, gb_ref, x_ref, wpa, wpb, wout, o_ref):
    za = za_ref[...]
    ya = (ya_ref[...] * (za * jax.nn.sigmoid(za))).astype(BF16)
    ya = jnp.dot(ya, wpa[...], preferred_element_type=F32)
    yb = jnp.dot(og_ref[...], wpb[...], preferred_element_type=F32)
    m = jax.nn.sigmoid(ga_ref[...]) * ya + jax.nn.sigmoid(gb_ref[...]) * yb
    o_ref[...] = x_ref[...] + jnp.dot(m.astype(BF16), wout[...], preferred_element_type=F32)


def _merge(ya, za, og, p_x, x, wpa, wpb, wout):
    b, t, _ = x.shape
    const = lambda bi, i: (0, 0)
    return pl.pallas_call(
        _merge_kernel,
        grid=(b, t // TILE),
        in_specs=[pl.BlockSpec((None, TILE, W_A), lambda bi, i: (bi, i, 0)),
                  pl.BlockSpec((None, TILE, W_A), lambda bi, i: (bi, i + 1, 0)),
                  pl.BlockSpec((None, TILE, W_B), lambda bi, i: (bi, i, 0)),
                  pl.BlockSpec((None, TILE, D_MODEL), lambda bi, i: (bi, i, C_GA)),
                  pl.BlockSpec((None, TILE, D_MODEL), lambda bi, i: (bi, i, C_GB)),
                  pl.BlockSpec((None, TILE, D_MODEL), lambda bi, i: (bi, i, 0)),
                  pl.BlockSpec((W_A, D_MODEL), const),
                  pl.BlockSpec((W_B, D_MODEL), const),
                  pl.BlockSpec((D_MODEL, D_MODEL), const)],
        out_specs=pl.BlockSpec((None, TILE, D_MODEL), lambda bi, i: (bi, i, 0)),
        out_shape=jax.ShapeDtypeStruct((b, t, D_MODEL), F32),
        compiler_params=_cparams(("arbitrary", "arbitrary")),
        name="merge",
    )(ya, za, og, p_x, p_x, x, wpa, wpb, wout)


def _permute_w_in(w):
    sizes = (W_A, W_A, W_A, W_A, H_IDX * D_IDX, D_IDX, H_IDX,
             H_B * DK_B, H_B * DK_B, W_B, W_B, H_B, H_B, D_MODEL, D_MODEL)
    offs = [0]
    for s in sizes:
        offs.append(offs[-1] + s)
    w = w.astype(BF16)
    (qa, ka, va, za, qi, ki, wi, qb, kb, vb, zb, bb, ab, ga, gb) = [
        w[:, offs[i]:offs[i + 1]] for i in range(len(sizes))]
    pad = jnp.zeros((w.shape[0], 128 - (D_IDX + H_IDX + 2 * H_B)), w.dtype)
    return (jnp.concatenate([qb, kb, vb, zb, ga, gb], axis=1),
            jnp.concatenate([qa, ka, va, za, qi, ki, wi, bb, ab, pad], axis=1))


def _lane_row(vals, offset):
    return jnp.zeros((1, 128), F32).at[0, offset:offset + vals.shape[0]].set(vals.astype(F32))


def _layer(x, meta_tokens, rel_bias, norm_w, w_in, q_norm_w, k_norm_w, conv_w, a_log, dt_bias,
           gdn_norm_w, w_proj_a, w_proj_b, w_out):
    b, t, _ = x.shape
    w_main, w_attn = _permute_w_in(w_in)
    nw_row = norm_w.reshape(1, D_MODEL)
    meta_pad = jnp.zeros((TILE, D_MODEL), F32).at[PAD_END:].set(meta_tokens)

    p_x = _inproj(x.reshape(b * t, D_MODEL), nw_row, w_main).reshape(b, t, N_MAIN)
    p_m = _inproj(meta_pad, nw_row, w_main)

    hid = lax.broadcasted_iota(I32, (W_A, W_A), 0) // DH_A
    bd = (hid == hid.T).astype(BF16)
    za, sm, k_n, v_t, k_i, q_tz, q_it, w_t = _attn_proj(
        x, meta_pad, nw_row, w_attn, jnp.tile(q_norm_w, H_A).reshape(1, W_A),
        jnp.tile(k_norm_w, H_A).reshape(1, W_A), bd)
    ya = _attention(q_tz, q_it, w_t, k_n, v_t, k_i, _bias_tile(rel_bias), t)

    og = _gdn(p_x, p_m, sm, conv_w, _lane_row(a_log, S_AB), _lane_row(dt_bias, S_AB),
              gdn_norm_w.reshape(1, DV_B))

    return _merge(ya, za, og, p_x, x, w_proj_a.astype(BF16), w_proj_b.astype(BF16), w_out.astype(BF16))


def kernel(x, meta_tokens, rel_bias, norm_w, w_in, q_norm_w, k_norm_w, conv_w, a_log, dt_bias,
           gdn_norm_w, w_proj_a, w_proj_b, w_out):
    depth = norm_w.shape[0]
    assert depth == 1, "meta rows are dropped after the layer; deeper stacks need them carried"
    return _layer(x, meta_tokens, rel_bias, norm_w[0], w_in[0], q_norm_w[0], k_norm_w[0], conv_w[0],
                  a_log[0], dt_bias[0], gdn_norm_w[0], w_proj_a[0], w_proj_b[0], w_out[0])
```

```python
import math

import jax
import jax.numpy as jnp
from jax import lax
from jax.experimental import pallas as pl
from jax.experimental.pallas import tpu as pltpu

F32 = jnp.float32
BF16 = jnp.bfloat16
I32 = jnp.int32
I16 = jnp.int16

D_MODEL = 1024
N_META = 16
H_A = 8
DH_A = 64
W_A = H_A * DH_A
H_IDX = 8
D_IDX = 64
TOPK = 256
N_BUCKETS = 32
MAX_DISTANCE = 128
H_B = 8
DK_B = 128
DV_B = 128
W_B = H_B * DV_B
CONV_K = 4
CHUNK = 64
EPS = 1e-6

N_MAIN = 6 * 1024
C_ZB, C_GA, C_GB = 3, 4, 5
A_Q, A_K, A_V, A_Z, A_QI, A_S = 0, 512, 1024, 1536, 2048, 2560
N_ATTN = A_S + 128
S_KI, S_WI, S_BB, S_AB = 0, 64, 72, 80

TILE = 512
QB = 256
KC = 256
NEAR = QB + 128
FAR0 = TILE - 128
PAD_END = TILE - N_META
VROWS = DH_A + 16
LOG2E = math.log2(math.e)
INT_MIN = -2 ** 31
NEG = -1e30
VMEM_LIMIT = 58 * 1024 * 1024
HI = lax.Precision.HIGHEST
GDN_CHUNKS_PER_STEP = 4


def _cparams(sem):
    return pltpu.CompilerParams(dimension_semantics=sem, vmem_limit_bytes=VMEM_LIMIT)


def _rms_bf16(x, nw):
    ms = jnp.mean(x * x, axis=-1, keepdims=True)
    return (x * lax.rsqrt(ms + EPS) * nw).astype(BF16)


def _inproj_kernel(x_ref, nw_ref, w_ref, o_ref):
    o_ref[...] = jnp.dot(_rms_bf16(x_ref[...], nw_ref[...]), w_ref[...], preferred_element_type=F32)


def _inproj(x2d, norm_w_row, w_bf16):
    m = x2d.shape[0]
    n = w_bf16.shape[1]
    tn = n // 2
    return pl.pallas_call(
        _inproj_kernel,
        grid=(2, m // TILE),
        in_specs=[pl.BlockSpec((TILE, D_MODEL), lambda j, i: (i, 0)),
                  pl.BlockSpec((1, D_MODEL), lambda j, i: (0, 0)),
                  pl.BlockSpec((D_MODEL, tn), lambda j, i: (0, j))],
        out_specs=pl.BlockSpec((TILE, tn), lambda j, i: (i, j)),
        out_shape=jax.ShapeDtypeStruct((m, n), F32),
        compiler_params=_cparams(("arbitrary", "arbitrary")),
        name="inproj",
    )(x2d, norm_w_row, w_bf16)


def _head_rms(x, bd, w_row):
    sq = x * x
    hi = sq.astype(BF16)
    lo = (sq - hi.astype(F32)).astype(BF16)
    ss = (jnp.dot(hi, bd, preferred_element_type=F32) + jnp.dot(lo, bd, preferred_element_type=F32))
    return x * lax.rsqrt(ss * (1.0 / DH_A) + EPS) * w_row


def _attn_proj_kernel(x_ref, xm_ref, nw_ref, w_ref, qnw, knw, bd_ref,
                      za_o, sm_o, k_o, vt_o, ki_o, qtz_o, qit_o, wt_o):
    x = jnp.where(pl.program_id(1) == 0, xm_ref[...], x_ref[...])
    h = _rms_bf16(x, nw_ref[...])
    bd = bd_ref[...]

    def proj(c0, width):
        return jnp.dot(h, w_ref[:, c0:c0 + width], preferred_element_type=F32)

    sm = proj(A_S, 128)
    sm_o[...] = sm
    ki_o[...] = sm[:, S_KI:S_KI + D_IDX].astype(BF16)
    wt_o[...] = sm.T[S_WI:S_WI + H_IDX, :] * ((H_IDX ** -0.5) * (D_IDX ** -0.5))

    qi = proj(A_QI, W_A)
    for j in range(W_A // 128):
        qit_o[j * 128:(j + 1) * 128, :] = qi[:, j * 128:(j + 1) * 128].T.astype(BF16)

    qn = _head_rms(proj(A_Q, W_A), bd, qnw[...]) * (DH_A ** -0.5 * LOG2E)
    zeros = jnp.zeros((DH_A, TILE), BF16)
    for j in range(W_A // 128):
        t = qn[:, j * 128:(j + 1) * 128].T.astype(BF16)
        base = 2 * j * 128
        qtz_o[base:base + 64, :] = t[0:64]
        qtz_o[base + 64:base + 128, :] = zeros
        qtz_o[base + 128:base + 192, :] = zeros
        qtz_o[base + 192:base + 256, :] = t[64:128]

    k_o[...] = _head_rms(proj(A_K, W_A), bd, knw[...]).astype(BF16)

    v = proj(A_V, W_A)
    ones_rows = jnp.where(lax.broadcasted_iota(I32, (VROWS - DH_A, TILE), 0) == 0, 1.0, 0.0).astype(BF16)
    for j in range(W_A // 128):
        t = v[:, j * 128:(j + 1) * 128].T.astype(BF16)
        for r in range(2):
            base = (2 * j + r) * VROWS
            vt_o[base:base + DH_A, :] = t[r * DH_A:(r + 1) * DH_A]
            vt_o[base + DH_A:base + VROWS, :] = ones_rows

    za_o[...] = proj(A_Z, W_A)


def _attn_proj(x, meta_pad, norm_w_row, w_bf16, qnw_row, knw_row, bd):
    b, t, _ = x.shape
    nt = t // TILE + 1
    tk = nt * TILE

    def rows(width, dtype):
        return (pl.BlockSpec((None, TILE, width), lambda bi, i: (bi, i, 0)),
                jax.ShapeDtypeStruct((b, tk, width), dtype))

    def cols(height, dtype):
        return (pl.BlockSpec((None, height, TILE), lambda bi, i: (bi, 0, i)),
                jax.ShapeDtypeStruct((b, height, tk), dtype))

    outs = [rows(W_A, F32), rows(128, F32), rows(W_A, BF16), cols(H_A * VROWS, BF16), rows(D_IDX, BF16),
            cols(2 * W_A, BF16), cols(W_A, BF16), cols(H_IDX, F32)]
    const = lambda bi, i: (0, 0)
    return pl.pallas_call(
        _attn_proj_kernel,
        grid=(b, nt),
        in_specs=[pl.BlockSpec((None, TILE, D_MODEL), lambda bi, i: (bi, jnp.maximum(i - 1, 0), 0)),
                  pl.BlockSpec((TILE, D_MODEL), const),
                  pl.BlockSpec((1, D_MODEL), const),
                  pl.BlockSpec((D_MODEL, N_ATTN), const),
                  pl.BlockSpec((1, W_A), const),
                  pl.BlockSpec((1, W_A), const),
                  pl.BlockSpec((W_A, W_A), const)],
        out_specs=[o[0] for o in outs],
        out_shape=[o[1] for o in outs],
        compiler_params=_cparams(("arbitrary", "arbitrary")),
        name="attn_proj",
    )(x, meta_pad, norm_w_row, w_bf16, qnw_row, knw_row, bd)


def _bias_kernel(rb_ref, o_ref):
    a = lax.broadcasted_iota(I32, (NEAR, QB), 1)
    bk = lax.broadcasted_iota(I32, (NEAR, QB), 0)
    d = jnp.maximum(a - bk + (NEAR - QB), 0)
    max_exact = N_BUCKETS // 2
    ratio = jnp.maximum(d, max_exact).astype(F32) / max_exact
    large = max_exact + (jnp.log(ratio) / math.log(MAX_DISTANCE / max_exact)
                         * (N_BUCKETS - max_exact)).astype(I32)
    large = jnp.minimum(large, N_BUCKETS - 1)
    bucket = jnp.where(d < max_exact, d, large)
    for h in range(H_A):
        acc = jnp.zeros((NEAR, QB), F32)
        for bb in range(N_BUCKETS):
            acc = jnp.where(bucket == bb, rb_ref[bb, h], acc)
        o_ref[h] = (acc - rb_ref[N_BUCKETS - 1, h]) * LOG2E


def _bias_tile(rel_bias):
    return pl.pallas_call(
        _bias_kernel,
        in_specs=[pl.BlockSpec(memory_space=pltpu.SMEM)],
        out_specs=pl.BlockSpec((H_A, NEAR, QB), lambda: (0, 0, 0)),
        out_shape=jax.ShapeDtypeStruct((H_A, NEAR, QB), F32),
        name="bias_tile",
    )(rel_bias)


def _attn_kernel(qtz_ref, qit_ref, wt_ref, qit_nx, wt_nx, k_ref, vt_ref, ki_ref, bias_ref, o_ref,
                 keys2_ref, hi_ref, lo_ref, m_ref, acc_ref, j_ref):
    iq = pl.program_id(1)
    nq = pl.num_programs(1)
    keys_ref = keys2_ref.at[iq % 2]
    keys_nx = keys2_ref.at[(iq + 1) % 2]
    near0 = pl.multiple_of(FAR0 + iq * QB, 128)
    q_pos = TILE + iq * QB + lax.broadcasted_iota(I32, (1, QB), 1)

    def score_chunk(kref, qit, wt, start, size, mask_causal):
        kic = ki_ref[pl.ds(start, size), :]
        acc = jnp.zeros((size, QB), F32)
        for h in range(H_IDX):
            r = jnp.dot(kic, qit[h * D_IDX:(h + 1) * D_IDX, :], preferred_element_type=F32)
            acc = acc + jnp.maximum(r, 0.0) * wt[h:h + 1, :]
        acc = acc + 0.0
        bits = pltpu.bitcast(acc, I32)
        key = jnp.where(bits < 0, bits ^ jnp.int32(0x7FFFFFFF), bits)
        if mask_causal:
            pos = start + lax.broadcasted_iota(I32, (size, QB), 0)
            key = jnp.where(pos <= q_pos, key, INT_MIN)
        kref[pl.ds(start, size), :] = key
        hi_ref[pl.ds(start, size), :] = lax.shift_right_arithmetic(key, 16).astype(I16)
        lo_ref[pl.ds(start, size), :] = ((key & 0xFFFF) - 2 ** 15).astype(I16)

    n_big = iq // 2
    n_prev = jnp.maximum(iq - 1, 0) // 2
    odd0 = pl.multiple_of(FAR0 + n_big * 2 * KC, 128)

    def big_start(c):
        return pl.multiple_of(FAR0 + c * 2 * KC, 128)

    def far_scores(c, carry):
        score_chunk(keys_ref, qit_ref, wt_ref, big_start(c), 2 * KC, False)
        return carry

    lax.fori_loop(n_prev, n_big, far_scores, 0)

    @pl.when(iq % 2 == 1)
    def _():
        score_chunk(keys_ref, qit_ref, wt_ref, odd0, KC, False)

    score_chunk(keys_ref, qit_ref, wt_ref, near0, NEAR, True)
    keys_ref[FAR0:PAD_END, :] = jnp.full((PAD_END - FAR0, QB), INT_MIN, I32)
    hi_ref[FAR0:PAD_END, :] = jnp.full((PAD_END - FAR0, QB), -2 ** 15, I16)
    lo_ref[FAR0:PAD_END, :] = jnp.full((PAD_END - FAR0, QB), -2 ** 15, I16)

    n256 = iq + NEAR // 256
    tail = NEAR % 256

    def blk_start(c):
        return pl.multiple_of(FAR0 + c * 256, 128)

    def count(ref, pred):
        def part(start, size):
            o = pred(ref[pl.ds(start, size), :], start).reshape(size // 32, 32, QB)
            parts = [o[i] for i in range(size // 32)]
            while len(parts) > 1:
                parts = [parts[i] + parts[i + 1] for i in range(0, len(parts), 2)]
            return parts[0]

        def body(c, acc):
            return acc + part(blk_start(c), 256)
        acc = lax.fori_loop(0, n256, body, jnp.zeros((32, QB), ref.dtype))
        if tail:
            acc = acc + part(blk_start(n256), tail)
        return jnp.sum(acc.astype(I32), axis=0, keepdims=True)

    one16 = jnp.int16(1)
    zero16 = jnp.int16(0)

    def half_step(ref):
        def step(it, t):
            cand = t + lax.shift_left(jnp.int32(1), 15 - it)
            c16 = cand.astype(I16)
            cnt = count(ref, lambda blk, start: jnp.where(blk >= c16, one16, zero16))
            return jnp.where(cnt >= TOPK, cand, t)
        return step

    half_min = jnp.full((1, QB), -2 ** 15, I32)
    tau_hi = lax.fori_loop(0, 16, half_step(hi_ref), half_min)
    th16 = tau_hi.astype(I16)

    def fold_low(start, size):
        hi = hi_ref[pl.ds(start, size), :]
        lo = lo_ref[pl.ds(start, size), :]
        lo_ref[pl.ds(start, size), :] = jnp.where(hi > th16, jnp.int16(2 ** 15 - 1),
                                                  jnp.where(hi < th16, jnp.int16(-2 ** 15), lo))

    def fold_body(c, carry):
        fold_low(blk_start(c), 256)
        return carry

    lax.fori_loop(0, n256, fold_body, 0)
    if tail:
        fold_low(blk_start(n256), tail)
    tau_lo = lax.fori_loop(0, 16, half_step(lo_ref), half_min)
    tau = lax.shift_left(tau_hi, 16) + (tau_lo + 2 ** 15)

    def count32(pred):
        return count(keys_ref, pred)

    cnt_ge = count32(lambda blk, start: jnp.where(blk >= tau, 1, 0))
    excess = jnp.where(tau > INT_MIN, jnp.where(cnt_ge > TOPK, 1, 0), 0).astype(I32)
    j_ref[...] = jnp.where(tau > INT_MIN, jnp.int32(2 ** 30), jnp.int32(-1))

    @pl.when(jnp.max(excess) > 0)
    def _():
        need = TOPK - count32(lambda blk, start: jnp.where(blk > tau, 1, 0))

        def pos_step(it, lo):
            cand = lo + lax.shift_left(jnp.int32(1), 12 - it)

            def pred(blk, start):
                pos = start + lax.broadcasted_iota(I32, blk.shape, 0)
                return jnp.where(pos < cand, jnp.where(blk == tau, 1, 0), 0)
            return jnp.where(count32(pred) < need, cand, lo)

        lo = lax.fori_loop(0, 13, pos_step, jnp.zeros((1, QB), I32))
        j_ref[...] = jnp.where(excess > 0, lo, j_ref[...])

    j_last = j_ref[...]

    m_ref[...] = jnp.full(m_ref.shape, NEG, F32)
    acc_ref[...] = jnp.zeros(acc_ref.shape, F32)

    def attend(start, size, near, mid=None):
        key = keys_ref[pl.ds(start, size), :]
        pos = start + lax.broadcasted_iota(I32, (size, QB), 0)
        tie = jnp.where(pos <= j_last, 0.0, NEG)
        madd = jnp.where(key > tau, 0.0, jnp.where(key == tau, tie, NEG))
        hs = range(H_A)
        s = [jnp.dot(k_ref[pl.ds(start, size), (h // 2) * 128:(h // 2 + 1) * 128],
                     qtz_ref[h * 128:(h + 1) * 128, :], preferred_element_type=F32) + madd for h in hs]
        if mid is not None:
            mid()
        if near:
            s = [s[h] + bias_ref[h] for h in hs]
        m_all = m_ref[...]
        m_new = [jnp.maximum(m_all[h:h + 1, :], jnp.max(s[h], axis=0, keepdims=True)) for h in hs]
        m_new_all = jnp.concatenate(m_new, axis=0)
        alpha_all = jnp.exp2(m_all - m_new_all)
        m_ref[...] = m_new_all
        p = [jnp.exp2(s[h] - m_new[h]).astype(BF16) for h in hs]
        pv = [jnp.dot(vt_ref[h * VROWS:(h + 1) * VROWS, pl.ds(start, size)], p[h],
                      preferred_element_type=F32) for h in hs]
        for h in hs:
            acc_ref[h * VROWS:(h + 1) * VROWS, :] = (alpha_all[h:h + 1, :] * acc_ref[h * VROWS:(h + 1) * VROWS, :]
                                                     + pv[h])

    def fused_attend(c, carry):
        start = big_start(c)
        attend(start, 2 * KC, False,
               mid=lambda: score_chunk(keys_nx, qit_nx, wt_nx, start, 2 * KC, False))
        return carry

    def far_attend(c, carry):
        attend(big_start(c), 2 * KC, False)
        return carry

    n_fused = jnp.where(iq + 1 < nq, n_big, 0)
    lax.fori_loop(0, n_fused, fused_attend, 0)
    lax.fori_loop(n_fused, n_big, far_attend, 0)

    @pl.when(iq % 2 == 1)
    def _():
        attend(odd0, KC, False)

    attend(near0, NEAR, True)

    for j in range(W_A // 128):
        parts = []
        for h in (2 * j, 2 * j + 1):
            inv = 1.0 / acc_ref[h * VROWS + DH_A:h * VROWS + DH_A + 1, :]
            parts.append(acc_ref[h * VROWS:h * VROWS + DH_A, :] * inv)
        o_ref[:, j * 128:(j + 1) * 128] = jnp.concatenate(parts, axis=0).T


def _attention(qtz, qit, wt, k, vt, ki, bias, t):
    b, tk, _ = k.shape
    qoff = TILE // QB
    nq = t // QB

    def nxt(bi, i):
        return (bi, 0, jnp.minimum(i + 1, nq - 1) + qoff)

    return pl.pallas_call(
        _attn_kernel,
        grid=(b, nq),
        in_specs=[pl.BlockSpec((None, 2 * W_A, QB), lambda bi, i: (bi, 0, i + qoff)),
                  pl.BlockSpec((None, W_A, QB), lambda bi, i: (bi, 0, i + qoff)),
                  pl.BlockSpec((None, H_IDX, QB), lambda bi, i: (bi, 0, i + qoff)),
                  pl.BlockSpec((None, W_A, QB), nxt),
                  pl.BlockSpec((None, H_IDX, QB), nxt),
                  pl.BlockSpec((None, tk, W_A), lambda bi, i: (bi, 0, 0)),
                  pl.BlockSpec((None, H_A * VROWS, tk), lambda bi, i: (bi, 0, 0)),
                  pl.BlockSpec((None, tk, D_IDX), lambda bi, i: (bi, 0, 0)),
                  pl.BlockSpec((H_A, NEAR, QB), lambda bi, i: (0, 0, 0))],
        out_specs=pl.BlockSpec((None, QB, W_A), lambda bi, i: (bi, i, 0)),
        out_shape=jax.ShapeDtypeStruct((b, t, W_A), F32),
        scratch_shapes=[pltpu.VMEM((2, tk, QB), I32),
                        pltpu.VMEM((tk, QB), I16),
                        pltpu.VMEM((tk, QB), I16),
                        pltpu.VMEM((H_A, QB), F32),
                        pltpu.VMEM((H_A * VROWS, QB), F32),
                        pltpu.VMEM((1, QB), I32)],
        compiler_params=_cparams(("arbitrary", "arbitrary")),
        name="attn",
    )(qtz, qit, wt, qit, wt, k, vt, ki, bias)


def _dot_hi(a, b):
    return jnp.dot(a, b, precision=HI, preferred_element_type=F32)


def _mm(a, b):
    return jnp.dot(a.astype(BF16), b.astype(BF16), preferred_element_type=F32)


def _conv_silu(x_in, xbuf, convw):
    r = x_in.shape[0]
    xbuf[8:8 + r, :] = x_in
    y = convw[CONV_K - 1:CONV_K, :] * x_in
    for i in range(CONV_K - 1):
        off = 8 - (CONV_K - 1) + i
        y = y + convw[i:i + 1, :] * xbuf[off:off + r, :]
    return y * jax.nn.sigmoid(y)


def _gates(sm, alog, dtb):
    z = sm + dtb
    softplus = jnp.maximum(z, 0.0) + jnp.log(1.0 + jnp.exp(-jnp.abs(z)))
    return -jnp.exp(alog) * softplus, jax.nn.sigmoid(sm)


def _gdn_local(y, g, beta, n_chunks):
    ri = lax.broadcasted_iota(I32, (CHUNK, CHUNK), 0)
    ci = lax.broadcasted_iota(I32, (CHUNK, CHUNK), 1)
    incl = ri >= ci
    strict = ri > ci
    tri = jnp.where(incl, 1.0, 0.0).astype(F32)
    tri_t = jnp.where(ri <= ci, 1.0, 0.0).astype(F32)
    eye = jnp.where(ri == ci, 1.0, 0.0).astype(F32)
    pairs = [(c, h) for c in range(n_chunks) for h in range(H_B)]
    idx = range(len(pairs))
    gc_col, gc_row = [], []
    for c in range(n_chunks):
        g_c = g[c * CHUNK:(c + 1) * CHUNK]
        gc_col.append(_dot_hi(tri, g_c))
        gc_row.append(_dot_hi(g_c.T, tri_t))
    q, kt, vb, qe, decay, kdt, glast = [], [], [], [], [], [], []
    for c, h in pairs:
        yc = y[c * CHUNK:(c + 1) * CHUNK]
        qh = yc[:, h * DK_B:(h + 1) * DK_B]
        kh = yc[:, W_B + h * DK_B:W_B + (h + 1) * DK_B]
        vh = yc[:, 2 * W_B + h * DV_B:2 * W_B + (h + 1) * DV_B]
        qh = qh * lax.rsqrt(jnp.sum(qh * qh, axis=-1, keepdims=True) + EPS) * (DK_B ** -0.5)
        kh = kh * lax.rsqrt(jnp.sum(kh * kh, axis=-1, keepdims=True) + EPS)
        gcol = gc_col[c][:, S_AB + h:S_AB + h + 1]
        grow = gc_row[c][S_AB + h:S_AB + h + 1, :]
        bcol = beta[c * CHUNK:(c + 1) * CHUNK, S_BB + h:S_BB + h + 1]
        g_last = gcol[CHUNK - 1:CHUNK, :]
        egc = jnp.exp(gcol)
        kth = kh.T
        kb = kh * bcol
        q.append(jnp.concatenate([kb, qh], axis=0))
        kt.append(kth)
        vb.append(jnp.concatenate([vh * bcol, kb * egc], axis=1))
        qe.append(qh * egc)
        decay.append(jnp.exp(jnp.where(incl, gcol - grow, -jnp.inf)))
        kdt.append(kth * jnp.exp(g_last - grow))
        glast.append(jnp.exp(g_last))
    qk = [_mm(q[i], kt[i]) for i in idx]
    a = [jnp.where(strict, qk[i][0:CHUNK] * decay[i], 0.0) for i in idx]
    att = [qk[i][CHUNK:2 * CHUNK] * decay[i] for i in idx]
    tm = [eye - a[i] for i in idx]
    pw = a
    for _ in range(5):
        pw = [_mm(pw[i], pw[i]) for i in idx]
        tm = [tm[i] + _mm(tm[i], pw[i]) for i in idx]
    uw = [_mm(tm[i], vb[i]) for i in idx]
    return uw, qe, att, kdt, glast


def _gdn_scan(local, c, s_list):
    uw, qe, att, kdt, glast = local
    base = c * H_B
    hs = range(H_B)
    ws = [_mm(jnp.concatenate([uw[base + h][:, DV_B:], qe[base + h]], axis=0), s_list[h]) for h in hs]
    v_new = [uw[base + h][:, :DV_B] - ws[h][0:CHUNK] for h in hs]
    ov = [_mm(jnp.concatenate([att[base + h], kdt[base + h]], axis=0), v_new[h]) for h in hs]
    outs = [ws[h][CHUNK:2 * CHUNK] + ov[h][0:CHUNK] for h in hs]
    s_out = [s_list[h] * glast[base + h] + ov[h][CHUNK:CHUNK + DK_B] for h in hs]
    return outs, s_out


def _gdn_meta_kernel(qkv_m, sm_m, convw, alog, dtb, s_o, xbuf):
    xbuf[0:8, :] = jnp.zeros((8, 3 * W_B), F32)
    y = _conv_silu(qkv_m[...], xbuf, convw)
    g, beta = _gates(sm_m[...], alog[...], dtb[...])
    live = lax.broadcasted_iota(I32, (CHUNK, 128), 0) >= CHUNK - N_META
    g = jnp.where(live, g, 0.0)
    beta = jnp.where(live, beta, 0.0)
    _, s_new = _gdn_scan(_gdn_local(y, g, beta, 1), 0, [jnp.zeros((DK_B, DV_B), F32)] * H_B)
    for h in range(H_B):
        s_o[h] = s_new[h]


def _gdn_kernel(qkv_x, zb_x, sm_x, tail_m, s0, convw, alog, dtb, gnw, o_ref, xbuf, s_ref):
    @pl.when(pl.program_id(1) == 0)
    def _():
        xbuf[0:8, :] = tail_m[...]
        s_ref[...] = s0[...]

    x_in = qkv_x[...]
    r = x_in.shape[0]
    y = _conv_silu(x_in, xbuf, convw)
    xbuf[0:8, :] = x_in[r - 8:r, :]
    g, beta = _gates(sm_x[...], alog[...], dtb[...])
    local = _gdn_local(y, g, beta, r // CHUNK)
    s_list = [s_ref[h] for h in range(H_B)]
    for c in range(r // CHUNK):
        rows = slice(c * CHUNK, (c + 1) * CHUNK)
        outs, s_list = _gdn_scan(local, c, s_list)
        for h in range(H_B):
            o = outs[h]
            on = o * lax.rsqrt(jnp.mean(o * o, axis=-1, keepdims=True) + EPS) * gnw[...]
            zb = zb_x[rows, h * DV_B:(h + 1) * DV_B]
            o_ref[rows, h * DV_B:(h + 1) * DV_B] = (on * (zb * jax.nn.sigmoid(zb))).astype(BF16)
    for h in range(H_B):
        s_ref[h] = s_list[h]


def _gdn(p_x, p_m, sm, convw, alog_row, dtb_row, gnw_row):
    b, t, _ = p_x.shape
    r = CHUNK * GDN_CHUNKS_PER_STEP
    last = TILE // CHUNK - 1
    s0 = pl.pallas_call(
        _gdn_meta_kernel,
        grid=(1,),
        in_specs=[pl.BlockSpec((CHUNK, 3 * W_B), lambda i: (last, 0)),
                  pl.BlockSpec((None, CHUNK, 128), lambda i: (0, last, 0)),
                  pl.BlockSpec((CONV_K, 3 * W_B), lambda i: (0, 0)),
                  pl.BlockSpec((1, 128), lambda i: (0, 0)),
                  pl.BlockSpec((1, 128), lambda i: (0, 0))],
        out_specs=pl.BlockSpec((H_B, DK_B, DV_B), lambda i: (0, 0, 0)),
        out_shape=jax.ShapeDtypeStruct((H_B, DK_B, DV_B), F32),
        scratch_shapes=[pltpu.VMEM((8 + CHUNK, 3 * W_B), F32)],
        compiler_params=_cparams(("arbitrary",)),
        name="gdn_meta",
    )(p_m, sm, convw, alog_row, dtb_row)
    return pl.pallas_call(
        _gdn_kernel,
        grid=(b, t // r),
        in_specs=[pl.BlockSpec((None, r, 3 * W_B), lambda bi, n: (bi, n, 0)),
                  pl.BlockSpec((None, r, W_B), lambda bi, n: (bi, n, C_ZB)),
                  pl.BlockSpec((None, r, 128), lambda bi, n: (bi, n + TILE // r, 0)),
                  pl.BlockSpec((8, 3 * W_B), lambda bi, n: (TILE // 8 - 1, 0)),
                  pl.BlockSpec((H_B, DK_B, DV_B), lambda bi, n: (0, 0, 0)),
                  pl.BlockSpec((CONV_K, 3 * W_B), lambda bi, n: (0, 0)),
                  pl.BlockSpec((1, 128), lambda bi, n: (0, 0)),
                  pl.BlockSpec((1, 128), lambda bi, n: (0, 0)),
                  pl.BlockSpec((1, DV_B), lambda bi, n: (0, 0))],
        out_specs=pl.BlockSpec((None, r, W_B), lambda bi, n: (bi, n, 0)),
        out_shape=jax.ShapeDtypeStruct((b, t, W_B), BF16),
        scratch_shapes=[pltpu.VMEM((8 + r, 3 * W_B), F32),
                        pltpu.VMEM((H_B, DK_B, DV_B), F32)],
        compiler_params=_cparams(("arbitrary", "arbitrary")),
        name="gdn",
    )(p_x, p_x, sm, p_m, s0, convw, alog_row, dtb_row, gnw_row)


def _merge_kernel(ya_ref, za_ref, og_ref, ga_ref, gb_ref, x_ref, wpa, wpb, wout, o_ref):
    za = za_ref[...]
    ya = (ya_ref[...] * (za * jax.nn.sigmoid(za))).astype(BF16)
    ya = jnp.dot(ya, wpa[...], preferred_element_type=F32)
    yb = jnp.dot(og_ref[...], wpb[...], preferred_element_type=F32)
    m = jax.nn.sigmoid(ga_ref[...]) * ya + jax.nn.sigmoid(gb_ref[...]) * yb
    o_ref[...] = x_ref[...] + jnp.dot(m.astype(BF16), wout[...], preferred_element_type=F32)


def _merge(ya, za, og, p_x, x, wpa, wpb, wout):
    b, t, _ = x.shape
    const = lambda bi, i: (0, 0)
    return pl.pallas_call(
        _merge_kernel,
        grid=(b, t // TILE),
        in_specs=[pl.BlockSpec((None, TILE, W_A), lambda bi, i: (bi, i, 0)),
                  pl.BlockSpec((None, TILE, W_A), lambda bi, i: (bi, i + 1, 0)),
                  pl.BlockSpec((None, TILE, W_B), lambda bi, i: (bi, i, 0)),
                  pl.BlockSpec((None, TILE, D_MODEL), lambda bi, i: (bi, i, C_GA)),
                  pl.BlockSpec((None, TILE, D_MODEL), lambda bi, i: (bi, i, C_GB)),
                  pl.BlockSpec((None, TILE, D_MODEL), lambda bi, i: (bi, i, 0)),
                  pl.BlockSpec((W_A, D_MODEL), const),
                  pl.BlockSpec((W_B, D_MODEL), const),
                  pl.BlockSpec((D_MODEL, D_MODEL), const)],
        out_specs=pl.BlockSpec((None, TILE, D_MODEL), lambda bi, i: (bi, i, 0)),
        out_shape=jax.ShapeDtypeStruct((b, t, D_MODEL), F32),
        compiler_params=_cparams(("arbitrary", "arbitrary")),
        name="merge",
    )(ya, za, og, p_x, p_x, x, wpa, wpb, wout)


def _permute_w_in(w):
    sizes = (W_A, W_A, W_A, W_A, H_IDX * D_IDX, D_IDX, H_IDX,
             H_B * DK_B, H_B * DK_B, W_B, W_B, H_B, H_B, D_MODEL, D_MODEL)
    offs = [0]
    for s in sizes:
        offs.append(offs[-1] + s)
    w = w.astype(BF16)
    (qa, ka, va, za, qi, ki, wi, qb, kb, vb, zb, bb, ab, ga, gb) = [
        w[:, offs[i]:offs[i + 1]] for i in range(len(sizes))]
    pad = jnp.zeros((w.shape[0], 128 - (D_IDX + H_IDX + 2 * H_B)), w.dtype)
    return (jnp.concatenate([qb, kb, vb, zb, ga, gb], axis=1),
            jnp.concatenate([qa, ka, va, za, qi, ki, wi, bb, ab, pad], axis=1))


def _lane_row(vals, offset):
    return jnp.zeros((1, 128), F32).at[0, offset:offset + vals.shape[0]].set(vals.astype(F32))


def _layer(x, meta_tokens, rel_bias, norm_w, w_in, q_norm_w, k_norm_w, conv_w, a_log, dt_bias,
           gdn_norm_w, w_proj_a, w_proj_b, w_out):
    b, t, _ = x.shape
    w_main, w_attn = _permute_w_in(w_in)
    nw_row = norm_w.reshape(1, D_MODEL)
    meta_pad = jnp.zeros((TILE, D_MODEL), F32).at[PAD_END:].set(meta_tokens)

    p_x = _inproj(x.reshape(b * t, D_MODEL), nw_row, w_main).reshape(b, t, N_MAIN)
    p_m = _inproj(meta_pad, nw_row, w_main)

    hid = lax.broadcasted_iota(I32, (W_A, W_A), 0) // DH_A
    bd = (hid == hid.T).astype(BF16)
    za, sm, k_n, v_t, k_i, q_tz, q_it, w_t = _attn_proj(
        x, meta_pad, nw_row, w_attn, jnp.tile(q_norm_w, H_A).reshape(1, W_A),
        jnp.tile(k_norm_w, H_A).reshape(1, W_A), bd)
    ya = _attention(q_tz, q_it, w_t, k_n, v_t, k_i, _bias_tile(rel_bias), t)

    og = _gdn(p_x, p_m, sm, conv_w, _lane_row(a_log, S_AB), _lane_row(dt_bias, S_AB),
              gdn_norm_w.reshape(1, DV_B))

    return _merge(ya, za, og, p_x, x, w_proj_a.astype(BF16), w_proj_b.astype(BF16), w_out.astype(BF16))


def kernel(x, meta_tokens, rel_bias, norm_w, w_in, q_norm_w, k_norm_w, conv_w, a_log, dt_bias,
           gdn_norm_w, w_proj_a, w_proj_b, w_out):
    depth = norm_w.shape[0]
    assert depth == 1, "meta rows are dropped after the layer; deeper stacks need them carried"
    return _layer(x, meta_tokens, rel_bias, norm_w[0], w_in[0], q_norm_w[0], k_norm_w[0], conv_w[0],
                  a_log[0], dt_bias[0], gdn_norm_w[0], w_proj_a[0], w_proj_b[0], w_out[0])
```

```python
import functools
import math

import jax
import jax.numpy as jnp
from jax import lax
from jax.experimental import pallas as pl
from jax.experimental.pallas import tpu as pltpu

F32 = jnp.float32
BF16 = jnp.bfloat16
I32 = jnp.int32
I16 = jnp.int16

D_MODEL = 1024
N_META = 16
H_A = 8
DH_A = 64
W_A = H_A * DH_A
H_IDX = 8
D_IDX = 64
TOPK = 256
N_BUCKETS = 32
MAX_DISTANCE = 128
H_B = 8
DK_B = 128
DV_B = 128
W_B = H_B * DV_B
CONV_K = 4
CHUNK = 64
EPS = 1e-6

G_ZB, G_GA, G_GB = 0, 1, 2
A_Q, A_K, A_V, A_Z, A_QI, A_S = 0, 512, 1024, 1536, 2048, 2560
N_ATTN = A_S + 128
S_KI, S_WI, S_BB, S_AB = 0, 64, 72, 80

TILE = 512
QB = 256
KC = 256
NEAR = QB + 128
FAR0 = TILE - 128
PAD_END = TILE - N_META
VROWS = DH_A + 16
LOG2E = math.log2(math.e)
INT_MIN = -2 ** 31
NEG = -1e30
VMEM_LIMIT = 58 * 1024 * 1024
HI = lax.Precision.HIGHEST
GDN_CHUNKS_PER_STEP = 4


def _cparams(sem):
    return pltpu.CompilerParams(dimension_semantics=sem, vmem_limit_bytes=VMEM_LIMIT)


def _rms_bf16(x, nw):
    ms = jnp.mean(x * x, axis=-1, keepdims=True)
    return (x * lax.rsqrt(ms + EPS) * nw).astype(BF16)


def _qkv_proj_kernel(x_ref, nw_ref, w_ref, wg_ref, convw, tail_in, y_o, tail_o, g_o, xb0, xb1, xb2,
                     *, tiles_per_seq):
    xbufs = (xb0, xb1, xb2)

    def cols(g):
        return slice(g * W_B, (g + 1) * W_B)

    @pl.when(pl.program_id(0) % tiles_per_seq == 0)
    def _():
        for g in range(3):
            xbufs[g][0:8, :] = tail_in[:, cols(g)]

    h = _rms_bf16(x_ref[...], nw_ref[...])

    def project(g):
        xbufs[g][8:8 + TILE, :] = jnp.dot(h, w_ref[:, cols(g)], preferred_element_type=F32)

    def gate(g):
        g_o[:, cols(g)] = jnp.dot(h, wg_ref[:, cols(g)], preferred_element_type=F32)

    def epilogue(g):
        cs = cols(g)
        xbuf = xbufs[g]
        y = convw[CONV_K - 1:CONV_K, cs] * xbuf[8:8 + TILE, :]
        for i in range(CONV_K - 1):
            off = 8 - (CONV_K - 1) + i
            y = y + convw[i:i + 1, cs] * xbuf[off:off + TILE, :]
        y = y * jax.nn.sigmoid(y)
        tail = xbuf[TILE:TILE + 8, :]
        tail_o[:, cs] = tail
        xbuf[0:8, :] = tail
        if g == 2:
            y_o[:, cs] = y
        else:
            scale = DK_B ** -0.5 if g == 0 else 1.0
            for hh in range(H_B):
                yh = y[:, hh * DK_B:(hh + 1) * DK_B]
                inv = lax.rsqrt(jnp.sum(yh * yh, axis=-1, keepdims=True) + EPS) * scale
                y_o[:, g * W_B + hh * DK_B:g * W_B + (hh + 1) * DK_B] = yh * inv

    project(0)
    gate(0)
    project(1)
    epilogue(0)
    gate(1)
    project(2)
    epilogue(1)
    gate(2)
    epilogue(2)


def _qkv_proj(x2d, norm_w_row, w_bf16, wg_bf16, convw, tail_in, tiles_per_seq):
    m = x2d.shape[0]
    n = 3 * W_B
    once = pl.Buffered(1)
    return pl.pallas_call(
        functools.partial(_qkv_proj_kernel, tiles_per_seq=tiles_per_seq),
        grid=(m // TILE,),
        in_specs=[pl.BlockSpec((TILE, D_MODEL), lambda i: (i, 0)),
                  pl.BlockSpec((1, D_MODEL), lambda i: (0, 0)),
                  pl.BlockSpec((D_MODEL, n), lambda i: (0, 0), pipeline_mode=once),
                  pl.BlockSpec((D_MODEL, n), lambda i: (0, 0), pipeline_mode=once),
                  pl.BlockSpec((CONV_K, n), lambda i: (0, 0)),
                  pl.BlockSpec((8, n), lambda i: (0, 0))],
        out_specs=[pl.BlockSpec((TILE, n), lambda i: (i, 0)),
                   pl.BlockSpec((8, n), lambda i: (i, 0)),
                   pl.BlockSpec((TILE, n), lambda i: (i, 0))],
        out_shape=[jax.ShapeDtypeStruct((m, n), F32),
                   jax.ShapeDtypeStruct((m // TILE * 8, n), F32),
                   jax.ShapeDtypeStruct((m, n), F32)],
        scratch_shapes=[pltpu.VMEM((8 + TILE, W_B), F32)] * 3,
        compiler_params=_cparams(("arbitrary",)),
        name="qkv_proj",
    )(x2d, norm_w_row, w_bf16, wg_bf16, convw, tail_in)


def _head_rms(x, bd, w_row):
    sq = x * x
    hi = sq.astype(BF16)
    lo = (sq - hi.astype(F32)).astype(BF16)
    ss = (jnp.dot(hi, bd, preferred_element_type=F32) + jnp.dot(lo, bd, preferred_element_type=F32))
    return x * lax.rsqrt(ss * (1.0 / DH_A) + EPS) * w_row


def _attn_proj_kernel(x_ref, xm_ref, nw_ref, w_ref, qnw, knw, bd_ref,
                      za_o, sm_o, k_o, vt_o, ki_o, qtz_o, qit_o, wt_o):
    x = jnp.where(pl.program_id(1) == 0, xm_ref[...], x_ref[...])
    h = _rms_bf16(x, nw_ref[...])
    bd = bd_ref[...]

    def proj(c0, width):
        return jnp.dot(h, w_ref[:, c0:c0 + width], preferred_element_type=F32)

    sm = proj(A_S, 128)
    sm_o[...] = sm
    ki_o[...] = sm[:, S_KI:S_KI + D_IDX].astype(BF16)
    wt_o[...] = sm.T[S_WI:S_WI + H_IDX, :] * ((H_IDX ** -0.5) * (D_IDX ** -0.5))

    qi = proj(A_QI, W_A)
    for j in range(W_A // 128):
        qit_o[j * 128:(j + 1) * 128, :] = qi[:, j * 128:(j + 1) * 128].T.astype(BF16)

    qn = _head_rms(proj(A_Q, W_A), bd, qnw[...]) * (DH_A ** -0.5 * LOG2E)
    zeros = jnp.zeros((DH_A, TILE), BF16)
    for j in range(W_A // 128):
        t = qn[:, j * 128:(j + 1) * 128].T.astype(BF16)
        base = 2 * j * 128
        qtz_o[base:base + 64, :] = t[0:64]
        qtz_o[base + 64:base + 128, :] = zeros
        qtz_o[base + 128:base + 192, :] = zeros
        qtz_o[base + 192:base + 256, :] = t[64:128]

    k_o[...] = _head_rms(proj(A_K, W_A), bd, knw[...]).astype(BF16)

    v = proj(A_V, W_A)
    ones_rows = jnp.where(lax.broadcasted_iota(I32, (VROWS - DH_A, TILE), 0) == 0, 1.0, 0.0).astype(BF16)
    for j in range(W_A // 128):
        t = v[:, j * 128:(j + 1) * 128].T.astype(BF16)
        for r in range(2):
            base = (2 * j + r) * VROWS
            vt_o[base:base + DH_A, :] = t[r * DH_A:(r + 1) * DH_A]
            vt_o[base + DH_A:base + VROWS, :] = ones_rows

    za_o[...] = proj(A_Z, W_A)


def _attn_proj(x, meta_pad, norm_w_row, w_bf16, qnw_row, knw_row, bd):
    b, t, _ = x.shape
    nt = t // TILE + 1
    tk = nt * TILE

    def rows(width, dtype):
        return (pl.BlockSpec((None, TILE, width), lambda bi, i: (bi, i, 0)),
                jax.ShapeDtypeStruct((b, tk, width), dtype))

    def cols(height, dtype):
        return (pl.BlockSpec((None, height, TILE), lambda bi, i: (bi, 0, i)),
                jax.ShapeDtypeStruct((b, height, tk), dtype))

    outs = [rows(W_A, F32), rows(128, F32), rows(W_A, BF16), cols(H_A * VROWS, BF16), rows(D_IDX, BF16),
            cols(2 * W_A, BF16), cols(W_A, BF16), cols(H_IDX, F32)]
    const = lambda bi, i: (0, 0)
    return pl.pallas_call(
        _attn_proj_kernel,
        grid=(b, nt),
        in_specs=[pl.BlockSpec((None, TILE, D_MODEL), lambda bi, i: (bi, jnp.maximum(i - 1, 0), 0)),
                  pl.BlockSpec((TILE, D_MODEL), const),
                  pl.BlockSpec((1, D_MODEL), const),
                  pl.BlockSpec((D_MODEL, N_ATTN), const),
                  pl.BlockSpec((1, W_A), const),
                  pl.BlockSpec((1, W_A), const),
                  pl.BlockSpec((W_A, W_A), const)],
        out_specs=[o[0] for o in outs],
        out_shape=[o[1] for o in outs],
        compiler_params=_cparams(("arbitrary", "arbitrary")),
        name="attn_proj",
    )(x, meta_pad, norm_w_row, w_bf16, qnw_row, knw_row, bd)


def _bias_kernel(rb_ref, o_ref):
    a = lax.broadcasted_iota(I32, (NEAR, QB), 1)
    bk = lax.broadcasted_iota(I32, (NEAR, QB), 0)
    d = jnp.maximum(a - bk + (NEAR - QB), 0)
    max_exact = N_BUCKETS // 2
    ratio = jnp.maximum(d, max_exact).astype(F32) / max_exact
    large = max_exact + (jnp.log(ratio) / math.log(MAX_DISTANCE / max_exact)
                         * (N_BUCKETS - max_exact)).astype(I32)
    large = jnp.minimum(large, N_BUCKETS - 1)
    bucket = jnp.where(d < max_exact, d, large)
    for h in range(H_A):
        acc = jnp.zeros((NEAR, QB), F32)
        for bb in range(N_BUCKETS):
            acc = jnp.where(bucket == bb, rb_ref[bb, h], acc)
        o_ref[h] = (acc - rb_ref[N_BUCKETS - 1, h]) * LOG2E


def _bias_tile(rel_bias):
    return pl.pallas_call(
        _bias_kernel,
        in_specs=[pl.BlockSpec(memory_space=pltpu.SMEM)],
        out_specs=pl.BlockSpec((H_A, NEAR, QB), lambda: (0, 0, 0)),
        out_shape=jax.ShapeDtypeStruct((H_A, NEAR, QB), F32),
        name="bias_tile",
    )(rel_bias)


def _attn_kernel(qtz_ref, qit_ref, wt_ref, qit_nx, wt_nx, k_ref, vt_ref, ki_ref, bias_ref, o_ref,
                 keys2_ref, hi_ref, lo_ref, m_ref, acc_ref, j_ref):
    iq = pl.program_id(1)
    nq = pl.num_programs(1)
    keys_ref = keys2_ref.at[iq % 2]
    keys_nx = keys2_ref.at[(iq + 1) % 2]
    near0 = pl.multiple_of(FAR0 + iq * QB, 128)
    q_pos = TILE + iq * QB + lax.broadcasted_iota(I32, (1, QB), 1)

    def score_chunk(kref, qit, wt, start, size, mask_causal):
        kic = ki_ref[pl.ds(start, size), :]
        acc = jnp.zeros((size, QB), F32)
        for h in range(H_IDX):
            r = jnp.dot(kic, qit[h * D_IDX:(h + 1) * D_IDX, :], preferred_element_type=F32)
            acc = acc + jnp.maximum(r, 0.0) * wt[h:h + 1, :]
        acc = acc + 0.0
        bits = pltpu.bitcast(acc, I32)
        key = jnp.where(bits < 0, bits ^ jnp.int32(0x7FFFFFFF), bits)
        if mask_causal:
            pos = start + lax.broadcasted_iota(I32, (size, QB), 0)
            key = jnp.where(pos <= q_pos, key, INT_MIN)
        kref[pl.ds(start, size), :] = key
        hi_ref[pl.ds(start, size), :] = lax.shift_right_arithmetic(key, 16).astype(I16)
        lo_ref[pl.ds(start, size), :] = ((key & 0xFFFF) - 2 ** 15).astype(I16)

    n_big = iq // 2
    n_prev = jnp.maximum(iq - 1, 0) // 2
    odd0 = pl.multiple_of(FAR0 + n_big * 2 * KC, 128)

    def big_start(c):
        return pl.multiple_of(FAR0 + c * 2 * KC, 128)

    def far_scores(c, carry):
        score_chunk(keys_ref, qit_ref, wt_ref, big_start(c), 2 * KC, False)
        return carry

    lax.fori_loop(n_prev, n_big, far_scores, 0)

    @pl.when(iq % 2 == 1)
    def _():
        score_chunk(keys_ref, qit_ref, wt_ref, odd0, KC, False)

    score_chunk(keys_ref, qit_ref, wt_ref, near0, NEAR, True)
    keys_ref[FAR0:PAD_END, :] = jnp.full((PAD_END - FAR0, QB), INT_MIN, I32)
    hi_ref[FAR0:PAD_END, :] = jnp.full((PAD_END - FAR0, QB), -2 ** 15, I16)
    lo_ref[FAR0:PAD_END, :] = jnp.full((PAD_END - FAR0, QB), -2 ** 15, I16)

    n256 = iq + NEAR // 256
    tail = NEAR % 256

    def blk_start(c):
        return pl.multiple_of(FAR0 + c * 256, 128)

    def count(ref, pred):
        def part(start, size):
            o = pred(ref[pl.ds(start, size), :], start).reshape(size // 32, 32, QB)
            parts = [o[i] for i in range(size // 32)]
            while len(parts) > 1:
                parts = [parts[i] + parts[i + 1] for i in range(0, len(parts), 2)]
            return parts[0]

        def body(c, acc):
            return acc + part(blk_start(c), 256)
        acc = lax.fori_loop(0, n256, body, jnp.zeros((32, QB), ref.dtype))
        if tail:
            acc = acc + part(blk_start(n256), tail)
        return jnp.sum(acc.astype(I32), axis=0, keepdims=True)

    one16 = jnp.int16(1)
    zero16 = jnp.int16(0)

    def half_step(ref):
        def step(it, t):
            cand = t + lax.shift_left(jnp.int32(1), 15 - it)
            c16 = cand.astype(I16)
            cnt = count(ref, lambda blk, start: jnp.where(blk >= c16, one16, zero16))
            return jnp.where(cnt >= TOPK, cand, t)
        return step

    half_min = jnp.full((1, QB), -2 ** 15, I32)
    tau_hi = lax.fori_loop(0, 16, half_step(hi_ref), half_min)
    th16 = tau_hi.astype(I16)

    def fold_low(start, size):
        hi = hi_ref[pl.ds(start, size), :]
        lo = lo_ref[pl.ds(start, size), :]
        lo_ref[pl.ds(start, size), :] = jnp.where(hi > th16, jnp.int16(2 ** 15 - 1),
                                                  jnp.where(hi < th16, jnp.int16(-2 ** 15), lo))

    def fold_body(c, carry):
        fold_low(blk_start(c), 256)
        return carry

    lax.fori_loop(0, n256, fold_body, 0)
    if tail:
        fold_low(blk_start(n256), tail)
    tau_lo = lax.fori_loop(0, 16, half_step(lo_ref), half_min)
    tau = lax.shift_left(tau_hi, 16) + (tau_lo + 2 ** 15)

    def count32(pred):
        return count(keys_ref, pred)

    cnt_ge = count32(lambda blk, start: jnp.where(blk >= tau, 1, 0))
    excess = jnp.where(tau > INT_MIN, jnp.where(cnt_ge > TOPK, 1, 0), 0).astype(I32)
    j_ref[...] = jnp.where(tau > INT_MIN, jnp.int32(2 ** 30), jnp.int32(-1))

    @pl.when(jnp.max(excess) > 0)
    def _():
        need = TOPK - count32(lambda blk, start: jnp.where(blk > tau, 1, 0))

        def pos_step(it, lo):
            cand = lo + lax.shift_left(jnp.int32(1), 12 - it)

            def pred(blk, start):
                pos = start + lax.broadcasted_iota(I32, blk.shape, 0)
                return jnp.where(pos < cand, jnp.where(blk == tau, 1, 0), 0)
            return jnp.where(count32(pred) < need, cand, lo)

        lo = lax.fori_loop(0, 13, pos_step, jnp.zeros((1, QB), I32))
        j_ref[...] = jnp.where(excess > 0, lo, j_ref[...])

    j_last = j_ref[...]

    m_ref[...] = jnp.full(m_ref.shape, NEG, F32)
    acc_ref[...] = jnp.zeros(acc_ref.shape, F32)

    def attend(start, size, near, mid=None):
        key = keys_ref[pl.ds(start, size), :]
        pos = start + lax.broadcasted_iota(I32, (size, QB), 0)
        tie = jnp.where(pos <= j_last, 0.0, NEG)
        madd = jnp.where(key > tau, 0.0, jnp.where(key == tau, tie, NEG))
        hs = range(H_A)
        s = [jnp.dot(k_ref[pl.ds(start, size), (h // 2) * 128:(h // 2 + 1) * 128],
                     qtz_ref[h * 128:(h + 1) * 128, :], preferred_element_type=F32) + madd for h in hs]
        if mid is not None:
            mid()
        if near:
            s = [s[h] + bias_ref[h] for h in hs]
        m_all = m_ref[...]
        m_new = [jnp.maximum(m_all[h:h + 1, :], jnp.max(s[h], axis=0, keepdims=True)) for h in hs]
        m_new_all = jnp.concatenate(m_new, axis=0)
        alpha_all = jnp.exp2(m_all - m_new_all)
        m_ref[...] = m_new_all
        p = [jnp.exp2(s[h] - m_new[h]).astype(BF16) for h in hs]
        pv = [jnp.dot(vt_ref[h * VROWS:(h + 1) * VROWS, pl.ds(start, size)], p[h],
                      preferred_element_type=F32) for h in hs]
        for h in hs:
            acc_ref[h * VROWS:(h + 1) * VROWS, :] = (alpha_all[h:h + 1, :] * acc_ref[h * VROWS:(h + 1) * VROWS, :]
                                                     + pv[h])

    def fused_attend(c, carry):
        start = big_start(c)
        attend(start, 2 * KC, False,
               mid=lambda: score_chunk(keys_nx, qit_nx, wt_nx, start, 2 * KC, False))
        return carry

    def far_attend(c, carry):
        attend(big_start(c), 2 * KC, False)
        return carry

    n_fused = jnp.where(iq + 1 < nq, n_big, 0)
    lax.fori_loop(0, n_fused, fused_attend, 0)
    lax.fori_loop(n_fused, n_big, far_attend, 0)

    @pl.when(iq % 2 == 1)
    def _():
        attend(odd0, KC, False)

    attend(near0, NEAR, True)

    for j in range(W_A // 128):
        parts = []
        for h in (2 * j, 2 * j + 1):
            inv = 1.0 / acc_ref[h * VROWS + DH_A:h * VROWS + DH_A + 1, :]
            parts.append(acc_ref[h * VROWS:h * VROWS + DH_A, :] * inv)
        o_ref[:, j * 128:(j + 1) * 128] = jnp.concatenate(parts, axis=0).T


def _attention(qtz, qit, wt, k, vt, ki, bias, t):
    b, tk, _ = k.shape
    qoff = TILE // QB
    nq = t // QB

    def nxt(bi, i):
        return (bi, 0, jnp.minimum(i + 1, nq - 1) + qoff)

    return pl.pallas_call(
        _attn_kernel,
        grid=(b, nq),
        in_specs=[pl.BlockSpec((None, 2 * W_A, QB), lambda bi, i: (bi, 0, i + qoff)),
                  pl.BlockSpec((None, W_A, QB), lambda bi, i: (bi, 0, i + qoff)),
                  pl.BlockSpec((None, H_IDX, QB), lambda bi, i: (bi, 0, i + qoff)),
                  pl.BlockSpec((None, W_A, QB), nxt),
                  pl.BlockSpec((None, H_IDX, QB), nxt),
                  pl.BlockSpec((None, tk, W_A), lambda bi, i: (bi, 0, 0)),
                  pl.BlockSpec((None, H_A * VROWS, tk), lambda bi, i: (bi, 0, 0)),
                  pl.BlockSpec((None, tk, D_IDX), lambda bi, i: (bi, 0, 0)),
                  pl.BlockSpec((H_A, NEAR, QB), lambda bi, i: (0, 0, 0))],
        out_specs=pl.BlockSpec((None, QB, W_A), lambda bi, i: (bi, i, 0)),
        out_shape=jax.ShapeDtypeStruct((b, t, W_A), F32),
        scratch_shapes=[pltpu.VMEM((2, tk, QB), I32),
                        pltpu.VMEM((tk, QB), I16),
                        pltpu.VMEM((tk, QB), I16),
                        pltpu.VMEM((H_A, QB), F32),
                        pltpu.VMEM((H_A * VROWS, QB), F32),
                        pltpu.VMEM((1, QB), I32)],
        compiler_params=_cparams(("arbitrary", "arbitrary")),
        name="attn",
    )(qtz, qit, wt, qit, wt, k, vt, ki, bias)


def _dot_hi(a, b):
    return jnp.dot(a, b, precision=HI, preferred_element_type=F32)


def _mm(a, b):
    return jnp.dot(a.astype(BF16), b.astype(BF16), preferred_element_type=F32)


def _gates(sm, alog, dtb):
    z = sm + dtb
    softplus = jnp.maximum(z, 0.0) + jnp.log(1.0 + jnp.exp(-jnp.abs(z)))
    return -jnp.exp(alog) * softplus, jax.nn.sigmoid(sm)


def _gdn_local(y, g, beta, n_chunks):
    ri = lax.broadcasted_iota(I32, (CHUNK, CHUNK), 0)
    ci = lax.broadcasted_iota(I32, (CHUNK, CHUNK), 1)
    incl = ri >= ci
    strict = ri > ci
    tri = jnp.where(incl, 1.0, 0.0).astype(F32)
    tri_t = jnp.where(ri <= ci, 1.0, 0.0).astype(F32)
    eye = jnp.where(ri == ci, 1.0, 0.0).astype(F32)
    pairs = [(c, h) for c in range(n_chunks) for h in range(H_B)]
    idx = range(len(pairs))
    gc_col, gc_row = [], []
    for c in range(n_chunks):
        g_c = g[c * CHUNK:(c + 1) * CHUNK]
        gc_col.append(_dot_hi(tri, g_c))
        gc_row.append(_dot_hi(g_c.T, tri_t))
    q, kt, vb, qe, decay, kdt, glast = [], [], [], [], [], [], []
    for c, h in pairs:
        yc = y[c * CHUNK:(c + 1) * CHUNK]
        qh = yc[:, h * DK_B:(h + 1) * DK_B]
        kh = yc[:, W_B + h * DK_B:W_B + (h + 1) * DK_B]
        vh = yc[:, 2 * W_B + h * DV_B:2 * W_B + (h + 1) * DV_B]
        gcol = gc_col[c][:, S_AB + h:S_AB + h + 1]
        grow = gc_row[c][S_AB + h:S_AB + h + 1, :]
        bcol = beta[c * CHUNK:(c + 1) * CHUNK, S_BB + h:S_BB + h + 1]
        g_last = gcol[CHUNK - 1:CHUNK, :]
        egc = jnp.exp(gcol)
        kth = kh.T
        kb = kh * bcol
        q.append(jnp.concatenate([kb, qh], axis=0))
        kt.append(kth)
        vb.append(jnp.concatenate([vh * bcol, kb * egc], axis=1))
        qe.append(qh * egc)
        decay.append(jnp.exp(jnp.where(incl, gcol - grow, -jnp.inf)))
        kdt.append(kth * jnp.exp(g_last - grow))
        glast.append(jnp.exp(g_last))
    qk = [_mm(q[i], kt[i]) for i in idx]
    a = [jnp.where(strict, qk[i][0:CHUNK] * decay[i], 0.0) for i in idx]
    att = [qk[i][CHUNK:2 * CHUNK] * decay[i] for i in idx]
    tm = [eye - a[i] for i in idx]
    pw = a
    for _ in range(5):
        pw = [_mm(pw[i], pw[i]) for i in idx]
        tm = [tm[i] + _mm(tm[i], pw[i]) for i in idx]
    uw = [_mm(tm[i], vb[i]) for i in idx]
    return uw, qe, att, kdt, glast


def _gdn_scan(local, c, s_list):
    uw, qe, att, kdt, glast = local
    base = c * H_B
    hs = range(H_B)
    ws = [_mm(jnp.concatenate([uw[base + h][:, DV_B:], qe[base + h]], axis=0), s_list[h]) for h in hs]
    v_new = [uw[base + h][:, :DV_B] - ws[h][0:CHUNK] for h in hs]
    ov = [_mm(jnp.concatenate([att[base + h], kdt[base + h]], axis=0), v_new[h]) for h in hs]
    outs = [ws[h][CHUNK:2 * CHUNK] + ov[h][0:CHUNK] for h in hs]
    s_out = [s_list[h] * glast[base + h] + ov[h][CHUNK:CHUNK + DK_B] for h in hs]
    return outs, s_out


def _gdn_meta_kernel(qkv_m, sm_m, alog, dtb, s_o):
    y = qkv_m[...]
    g, beta = _gates(sm_m[...], alog[...], dtb[...])
    live = lax.broadcasted_iota(I32, (CHUNK, 128), 0) >= CHUNK - N_META
    g = jnp.where(live, g, 0.0)
    beta = jnp.where(live, beta, 0.0)
    _, s_new = _gdn_scan(_gdn_local(y, g, beta, 1), 0, [jnp.zeros((DK_B, DV_B), F32)] * H_B)
    for h in range(H_B):
        s_o[h] = s_new[h]


def _gdn_kernel(qkv_x, zb_x, sm_x, s0, alog, dtb, gnw, o_ref, s_ref):
    @pl.when(pl.program_id(1) == 0)
    def _():
        s_ref[...] = s0[...]

    y = qkv_x[...]
    r = y.shape[0]
    g, beta = _gates(sm_x[...], alog[...], dtb[...])
    local = _gdn_local(y, g, beta, r // CHUNK)
    s_list = [s_ref[h] for h in range(H_B)]
    for c in range(r // CHUNK):
        rows = slice(c * CHUNK, (c + 1) * CHUNK)
        outs, s_list = _gdn_scan(local, c, s_list)
        for h in range(H_B):
            o = outs[h]
            on = o * lax.rsqrt(jnp.mean(o * o, axis=-1, keepdims=True) + EPS) * gnw[...]
            zb = zb_x[rows, h * DV_B:(h + 1) * DV_B]
            o_ref[rows, h * DV_B:(h + 1) * DV_B] = (on * (zb * jax.nn.sigmoid(zb))).astype(BF16)
    for h in range(H_B):
        s_ref[h] = s_list[h]


def _gdn(y_x, y_m, p_g, sm, alog_row, dtb_row, gnw_row):
    b, t, _ = y_x.shape
    r = CHUNK * GDN_CHUNKS_PER_STEP
    last = TILE // CHUNK - 1
    s0 = pl.pallas_call(
        _gdn_meta_kernel,
        grid=(1,),
        in_specs=[pl.BlockSpec((CHUNK, 3 * W_B), lambda i: (last, 0)),
                  pl.BlockSpec((None, CHUNK, 128), lambda i: (0, last, 0)),
                  pl.BlockSpec((1, 128), lambda i: (0, 0)),
                  pl.BlockSpec((1, 128), lambda i: (0, 0))],
        out_specs=pl.BlockSpec((H_B, DK_B, DV_B), lambda i: (0, 0, 0)),
        out_shape=jax.ShapeDtypeStruct((H_B, DK_B, DV_B), F32),
        compiler_params=_cparams(("arbitrary",)),
        name="gdn_meta",
    )(y_m, sm, alog_row, dtb_row)
    return pl.pallas_call(
        _gdn_kernel,
        grid=(b, t // r),
        in_specs=[pl.BlockSpec((None, r, 3 * W_B), lambda bi, n: (bi, n, 0)),
                  pl.BlockSpec((None, r, W_B), lambda bi, n: (bi, n, G_ZB)),
                  pl.BlockSpec((None, r, 128), lambda bi, n: (bi, n + TILE // r, 0)),
                  pl.BlockSpec((H_B, DK_B, DV_B), lambda bi, n: (0, 0, 0)),
                  pl.BlockSpec((1, 128), lambda bi, n: (0, 0)),
                  pl.BlockSpec((1, 128), lambda bi, n: (0, 0)),
                  pl.BlockSpec((1, DV_B), lambda bi, n: (0, 0))],
        out_specs=pl.BlockSpec((None, r, W_B), lambda bi, n: (bi, n, 0)),
        out_shape=jax.ShapeDtypeStruct((b, t, W_B), BF16),
        scratch_shapes=[pltpu.VMEM((H_B, DK_B, DV_B), F32)],
        compiler_params=_cparams(("arbitrary", "arbitrary")),
        name="gdn",
    )(y_x, p_g, sm, s0, alog_row, dtb_row, gnw_row)


def _merge_kernel(ya_ref, za_ref, og_ref, ga_ref, gb_ref, x_ref, wpa, wpb, wout, o_ref):
    za = za_ref[...]
    ya = (ya_ref[...] * (za * jax.nn.sigmoid(za))).astype(BF16)
    ya = jnp.dot(ya, wpa[...], preferred_element_type=F32)
    yb = jnp.dot(og_ref[...], wpb[...], preferred_element_type=F32)
    m = jax.nn.sigmoid(ga_ref[...]) * ya + jax.nn.sigmoid(gb_ref[...]) * yb
    o_ref[...] = x_ref[...] + jnp.dot(m.astype(BF16), wout[...], preferred_element_type=F32)


def _merge(ya, za, og, p_g, x, wpa, wpb, wout):
    b, t, _ = x.shape
    const = lambda bi, i: (0, 0)
    return pl.pallas_call(
        _merge_kernel,
        grid=(b, t // TILE),
        in_specs=[pl.BlockSpec((None, TILE, W_A), lambda bi, i: (bi, i, 0)),
                  pl.BlockSpec((None, TILE, W_A), lambda bi, i: (bi, i + 1, 0)),
                  pl.BlockSpec((None, TILE, W_B), lambda bi, i: (bi, i, 0)),
                  pl.BlockSpec((None, TILE, D_MODEL), lambda bi, i: (bi, i, G_GA)),
                  pl.BlockSpec((None, TILE, D_MODEL), lambda bi, i: (bi, i, G_GB)),
                  pl.BlockSpec((None, TILE, D_MODEL), lambda bi, i: (bi, i, 0)),
                  pl.BlockSpec((W_A, D_MODEL), const),
                  pl.BlockSpec((W_B, D_MODEL), const),
                  pl.BlockSpec((D_MODEL, D_MODEL), const)],
        out_specs=pl.BlockSpec((None, TILE, D_MODEL), lambda bi, i: (bi, i, 0)),
        out_shape=jax.ShapeDtypeStruct((b, t, D_MODEL), F32),
        compiler_params=_cparams(("arbitrary", "arbitrary")),
        name="merge",
    )(ya, za, og, p_g, p_g, x, wpa, wpb, wout)


def _permute_w_in(w):
    sizes = (W_A, W_A, W_A, W_A, H_IDX * D_IDX, D_IDX, H_IDX,
             H_B * DK_B, H_B * DK_B, W_B, W_B, H_B, H_B, D_MODEL, D_MODEL)
    offs = [0]
    for s in sizes:
        offs.append(offs[-1] + s)
    w = w.astype(BF16)
    (qa, ka, va, za, qi, ki, wi, qb, kb, vb, zb, bb, ab, ga, gb) = [
        w[:, offs[i]:offs[i + 1]] for i in range(len(sizes))]
    pad = jnp.zeros((w.shape[0], 128 - (D_IDX + H_IDX + 2 * H_B)), w.dtype)
    return (jnp.concatenate([qb, kb, vb], axis=1), jnp.concatenate([zb, ga, gb], axis=1),
            jnp.concatenate([qa, ka, va, za, qi, ki, wi, bb, ab, pad], axis=1))


def _lane_row(vals, offset):
    return jnp.zeros((1, 128), F32).at[0, offset:offset + vals.shape[0]].set(vals.astype(F32))


def _layer(x, meta_tokens, rel_bias, norm_w, w_in, q_norm_w, k_norm_w, conv_w, a_log, dt_bias,
           gdn_norm_w, w_proj_a, w_proj_b, w_out):
    b, t, _ = x.shape
    w_qkv, w_gate, w_attn = _permute_w_in(w_in)
    nw_row = norm_w.reshape(1, D_MODEL)
    meta_pad = jnp.zeros((TILE, D_MODEL), F32).at[PAD_END:].set(meta_tokens)
    x2d = x.reshape(b * t, D_MODEL)

    y_m, tail_m, _ = _qkv_proj(meta_pad, nw_row, w_qkv, w_gate, conv_w, jnp.zeros((8, 3 * W_B), F32), 1)
    y_x, _, p_g = _qkv_proj(x2d, nw_row, w_qkv, w_gate, conv_w, tail_m, t // TILE)
    p_g = p_g.reshape(b, t, 3 * W_B)

    hid = lax.broadcasted_iota(I32, (W_A, W_A), 0) // DH_A
    bd = (hid == hid.T).astype(BF16)
    za, sm, k_n, v_t, k_i, q_tz, q_it, w_t = _attn_proj(
        x, meta_pad, nw_row, w_attn, jnp.tile(q_norm_w, H_A).reshape(1, W_A),
        jnp.tile(k_norm_w, H_A).reshape(1, W_A), bd)
    ya = _attention(q_tz, q_it, w_t, k_n, v_t, k_i, _bias_tile(rel_bias), t)

    og = _gdn(y_x.reshape(b, t, 3 * W_B), y_m, p_g, sm, _lane_row(a_log, S_AB), _lane_row(dt_bias, S_AB),
              gdn_norm_w.reshape(1, DV_B))

    return _merge(ya, za, og, p_g, x, w_proj_a.astype(BF16), w_proj_b.astype(BF16), w_out.astype(BF16))


def kernel(x, meta_tokens, rel_bias, norm_w, w_in, q_norm_w, k_norm_w, conv_w, a_log, dt_bias,
           gdn_norm_w, w_proj_a, w_proj_b, w_out):
    depth = norm_w.shape[0]
    assert depth == 1, "meta rows are dropped after the layer; deeper stacks need them carried"
    return _layer(x, meta_tokens, rel_bias, norm_w[0], w_in[0], q_norm_w[0], k_norm_w[0], conv_w[0],
                  a_log[0], dt_bias[0], gdn_norm_w[0], w_proj_a[0], w_proj_b[0], w_out[0])
```

```python
import functools
import math

import jax
import jax.numpy as jnp
from jax import lax
from jax.experimental import pallas as pl
from jax.experimental.pallas import tpu as pltpu

F32 = jnp.float32
BF16 = jnp.bfloat16
I32 = jnp.int32
I16 = jnp.int16

D_MODEL = 1024
N_META = 16
H_A = 8
DH_A = 64
W_A = H_A * DH_A
H_IDX = 8
D_IDX = 64
TOPK = 256
N_BUCKETS = 32
MAX_DISTANCE = 128
H_B = 8
DK_B = 128
DV_B = 128
W_B = H_B * DV_B
CONV_K = 4
CHUNK = 64
EPS = 1e-6

G_ZB, G_GA, G_GB = 0, 1, 2
A_Q, A_K, A_V, A_Z, A_QI, A_S = 0, 512, 1024, 1536, 2048, 2560
N_ATTN = A_S + 128
S_KI, S_WI, S_BB, S_AB = 0, 64, 72, 80

TILE = 512
QB = 256
KC = 256
NEAR = QB + 128
FAR0 = TILE - 128
PAD_END = TILE - N_META
VROWS = DH_A + 16
LOG2E = math.log2(math.e)
INT_MIN = -2 ** 31
NEG = -1e30
VMEM_LIMIT = 58 * 1024 * 1024
HI = lax.Precision.HIGHEST
GDN_CHUNKS_PER_STEP = 4


def _cparams(sem):
    return pltpu.CompilerParams(dimension_semantics=sem, vmem_limit_bytes=VMEM_LIMIT)


def _rms_bf16(x, nw):
    ms = jnp.mean(x * x, axis=-1, keepdims=True)
    return (x * lax.rsqrt(ms + EPS) * nw).astype(BF16)


def _qkv_proj_kernel(x_ref, nw_ref, w_ref, wg_ref, convw, tail_in, y_o, tail_o, g_o, xb0, xb1, xb2,
                     *, tiles_per_seq):
    xbufs = (xb0, xb1, xb2)

    def cols(g):
        return slice(g * W_B, (g + 1) * W_B)

    @pl.when(pl.program_id(0) % tiles_per_seq == 0)
    def _():
        for g in range(3):
            xbufs[g][0:8, :] = tail_in[:, cols(g)]

    h = _rms_bf16(x_ref[...], nw_ref[...])

    unit = 2 * DK_B
    n_units = 3 * W_B // unit

    def where(u):
        g, lo = divmod(u * unit, W_B)
        return g, slice(lo, lo + unit), slice(u * unit, (u + 1) * unit)

    def project(u):
        g, cg, ca = where(u)
        xbufs[g][8:8 + TILE, cg] = jnp.dot(h, w_ref[:, ca], preferred_element_type=F32)

    def gate(u):
        _, _, ca = where(u)
        g_o[:, ca] = jnp.dot(h, wg_ref[:, ca], preferred_element_type=F32)

    def epilogue(u):
        g, cg, ca = where(u)
        xbuf = xbufs[g]
        y = convw[CONV_K - 1:CONV_K, ca] * xbuf[8:8 + TILE, cg]
        for i in range(CONV_K - 1):
            off = 8 - (CONV_K - 1) + i
            y = y + convw[i:i + 1, ca] * xbuf[off:off + TILE, cg]
        y = y * jax.nn.sigmoid(y)
        tail = xbuf[TILE:TILE + 8, cg]
        tail_o[:, ca] = tail
        xbuf[0:8, cg] = tail
        if g == 2:
            y_o[:, ca] = y
        else:
            scale = DK_B ** -0.5 if g == 0 else 1.0
            for hh in range(unit // DK_B):
                yh = y[:, hh * DK_B:(hh + 1) * DK_B]
                inv = lax.rsqrt(jnp.sum(yh * yh, axis=-1, keepdims=True) + EPS) * scale
                y_o[:, u * unit + hh * DK_B:u * unit + (hh + 1) * DK_B] = yh * inv

    project(0)
    for u in range(n_units):
        gate(u)
        if u + 1 < n_units:
            project(u + 1)
        epilogue(u)


def _qkv_proj(x2d, norm_w_row, w_bf16, wg_bf16, convw, tail_in, tiles_per_seq):
    m = x2d.shape[0]
    n = 3 * W_B
    once = pl.Buffered(1)
    return pl.pallas_call(
        functools.partial(_qkv_proj_kernel, tiles_per_seq=tiles_per_seq),
        grid=(m // TILE,),
        in_specs=[pl.BlockSpec((TILE, D_MODEL), lambda i: (i, 0)),
                  pl.BlockSpec((1, D_MODEL), lambda i: (0, 0)),
                  pl.BlockSpec((D_MODEL, n), lambda i: (0, 0), pipeline_mode=once),
                  pl.BlockSpec((D_MODEL, n), lambda i: (0, 0), pipeline_mode=once),
                  pl.BlockSpec((CONV_K, n), lambda i: (0, 0)),
                  pl.BlockSpec((8, n), lambda i: (0, 0))],
        out_specs=[pl.BlockSpec((TILE, n), lambda i: (i, 0)),
                   pl.BlockSpec((8, n), lambda i: (i, 0)),
                   pl.BlockSpec((TILE, n), lambda i: (i, 0))],
        out_shape=[jax.ShapeDtypeStruct((m, n), F32),
                   jax.ShapeDtypeStruct((m // TILE * 8, n), F32),
                   jax.ShapeDtypeStruct((m, n), F32)],
        scratch_shapes=[pltpu.VMEM((8 + TILE, W_B), F32)] * 3,
        compiler_params=_cparams(("arbitrary",)),
        name="qkv_proj",
    )(x2d, norm_w_row, w_bf16, wg_bf16, convw, tail_in)


def _head_rms(x, bd, w_row):
    sq = x * x
    hi = sq.astype(BF16)
    lo = (sq - hi.astype(F32)).astype(BF16)
    ss = (jnp.dot(hi, bd, preferred_element_type=F32) + jnp.dot(lo, bd, preferred_element_type=F32))
    return x * lax.rsqrt(ss * (1.0 / DH_A) + EPS) * w_row


def _attn_proj_kernel(x_ref, xm_ref, nw_ref, w_ref, qnw, knw, bd_ref,
                      za_o, sm_o, k_o, vt_o, ki_o, qtz_o, qit_o, wt_o):
    x = jnp.where(pl.program_id(1) == 0, xm_ref[...], x_ref[...])
    h = _rms_bf16(x, nw_ref[...])
    bd = bd_ref[...]

    def proj(c0, width):
        return jnp.dot(h, w_ref[:, c0:c0 + width], preferred_element_type=F32)

    sm = proj(A_S, 128)
    qi = proj(A_QI, W_A)
    sm_o[...] = sm
    ki_o[...] = sm[:, S_KI:S_KI + D_IDX].astype(BF16)
    wt_o[...] = sm.T[S_WI:S_WI + H_IDX, :] * ((H_IDX ** -0.5) * (D_IDX ** -0.5))

    q = proj(A_Q, W_A)
    for j in range(W_A // 128):
        qit_o[j * 128:(j + 1) * 128, :] = qi[:, j * 128:(j + 1) * 128].T.astype(BF16)

    k = proj(A_K, W_A)
    qn = _head_rms(q, bd, qnw[...]) * (DH_A ** -0.5 * LOG2E)
    zeros = jnp.zeros((DH_A, TILE), BF16)
    for j in range(W_A // 128):
        t = qn[:, j * 128:(j + 1) * 128].T.astype(BF16)
        base = 2 * j * 128
        qtz_o[base:base + 64, :] = t[0:64]
        qtz_o[base + 64:base + 128, :] = zeros
        qtz_o[base + 128:base + 192, :] = zeros
        qtz_o[base + 192:base + 256, :] = t[64:128]

    v = proj(A_V, W_A)
    k_o[...] = _head_rms(k, bd, knw[...]).astype(BF16)

    za = proj(A_Z, W_A)
    ones_rows = jnp.where(lax.broadcasted_iota(I32, (VROWS - DH_A, TILE), 0) == 0, 1.0, 0.0).astype(BF16)
    for j in range(W_A // 128):
        t = v[:, j * 128:(j + 1) * 128].T.astype(BF16)
        for r in range(2):
            base = (2 * j + r) * VROWS
            vt_o[base:base + DH_A, :] = t[r * DH_A:(r + 1) * DH_A]
            vt_o[base + DH_A:base + VROWS, :] = ones_rows
    za_o[...] = za


def _attn_proj(x, meta_pad, norm_w_row, w_bf16, qnw_row, knw_row, bd):
    b, t, _ = x.shape
    nt = t // TILE + 1
    tk = nt * TILE

    def rows(width, dtype):
        return (pl.BlockSpec((None, TILE, width), lambda bi, i: (bi, i, 0)),
                jax.ShapeDtypeStruct((b, tk, width), dtype))

    def cols(height, dtype):
        return (pl.BlockSpec((None, height, TILE), lambda bi, i: (bi, 0, i)),
                jax.ShapeDtypeStruct((b, height, tk), dtype))

    outs = [rows(W_A, F32), rows(128, F32), rows(W_A, BF16), cols(H_A * VROWS, BF16), rows(D_IDX, BF16),
            cols(2 * W_A, BF16), cols(W_A, BF16), cols(H_IDX, F32)]
    const = lambda bi, i: (0, 0)
    return pl.pallas_call(
        _attn_proj_kernel,
        grid=(b, nt),
        in_specs=[pl.BlockSpec((None, TILE, D_MODEL), lambda bi, i: (bi, jnp.maximum(i - 1, 0), 0)),
                  pl.BlockSpec((TILE, D_MODEL), const),
                  pl.BlockSpec((1, D_MODEL), const),
                  pl.BlockSpec((D_MODEL, N_ATTN), const),
                  pl.BlockSpec((1, W_A), const),
                  pl.BlockSpec((1, W_A), const),
                  pl.BlockSpec((W_A, W_A), const)],
        out_specs=[o[0] for o in outs],
        out_shape=[o[1] for o in outs],
        compiler_params=_cparams(("arbitrary", "arbitrary")),
        name="attn_proj",
    )(x, meta_pad, norm_w_row, w_bf16, qnw_row, knw_row, bd)


def _bias_kernel(rb_ref, o_ref):
    a = lax.broadcasted_iota(I32, (NEAR, QB), 1)
    bk = lax.broadcasted_iota(I32, (NEAR, QB), 0)
    d = jnp.maximum(a - bk + (NEAR - QB), 0)
    max_exact = N_BUCKETS // 2
    ratio = jnp.maximum(d, max_exact).astype(F32) / max_exact
    large = max_exact + (jnp.log(ratio) / math.log(MAX_DISTANCE / max_exact)
                         * (N_BUCKETS - max_exact)).astype(I32)
    large = jnp.minimum(large, N_BUCKETS - 1)
    bucket = jnp.where(d < max_exact, d, large)
    for h in range(H_A):
        acc = jnp.zeros((NEAR, QB), F32)
        for bb in range(N_BUCKETS):
            acc = jnp.where(bucket == bb, rb_ref[bb, h], acc)
        o_ref[h] = (acc - rb_ref[N_BUCKETS - 1, h]) * LOG2E


def _bias_tile(rel_bias):
    return pl.pallas_call(
        _bias_kernel,
        in_specs=[pl.BlockSpec(memory_space=pltpu.SMEM)],
        out_specs=pl.BlockSpec((H_A, NEAR, QB), lambda: (0, 0, 0)),
        out_shape=jax.ShapeDtypeStruct((H_A, NEAR, QB), F32),
        name="bias_tile",
    )(rel_bias)


def _attn_kernel(qtz_ref, qit_ref, wt_ref, qit_nx, wt_nx, k_ref, vt_ref, ki_ref, bias_ref, o_ref,
                 keys2_ref, hi_ref, lo_ref, m_ref, acc_ref, j_ref):
    iq = pl.program_id(1)
    nq = pl.num_programs(1)
    keys_ref = keys2_ref.at[iq % 2]
    keys_nx = keys2_ref.at[(iq + 1) % 2]
    near0 = pl.multiple_of(FAR0 + iq * QB, 128)
    q_pos = TILE + iq * QB + lax.broadcasted_iota(I32, (1, QB), 1)

    def score_chunk(kref, qit, wt, start, size, mask_causal):
        kic = ki_ref[pl.ds(start, size), :]
        acc = jnp.zeros((size, QB), F32)
        for h in range(H_IDX):
            r = jnp.dot(kic, qit[h * D_IDX:(h + 1) * D_IDX, :], preferred_element_type=F32)
            acc = acc + jnp.maximum(r, 0.0) * wt[h:h + 1, :]
        acc = acc + 0.0
        bits = pltpu.bitcast(acc, I32)
        key = jnp.where(bits < 0, bits ^ jnp.int32(0x7FFFFFFF), bits)
        if mask_causal:
            pos = start + lax.broadcasted_iota(I32, (size, QB), 0)
            key = jnp.where(pos <= q_pos, key, INT_MIN)
        kref[pl.ds(start, size), :] = key
        hi_ref[pl.ds(start, size), :] = lax.shift_right_arithmetic(key, 16).astype(I16)
        lo_ref[pl.ds(start, size), :] = ((key & 0xFFFF) - 2 ** 15).astype(I16)

    n_big = iq // 2
    n_prev = jnp.maximum(iq - 1, 0) // 2
    odd0 = pl.multiple_of(FAR0 + n_big * 2 * KC, 128)

    def big_start(c):
        return pl.multiple_of(FAR0 + c * 2 * KC, 128)

    def far_scores(c, carry):
        score_chunk(keys_ref, qit_ref, wt_ref, big_start(c), 2 * KC, False)
        return carry

    lax.fori_loop(n_prev, n_big, far_scores, 0)

    @pl.when(iq % 2 == 1)
    def _():
        score_chunk(keys_ref, qit_ref, wt_ref, odd0, KC, False)

    score_chunk(keys_ref, qit_ref, wt_ref, near0, NEAR, True)
    keys_ref[FAR0:PAD_END, :] = jnp.full((PAD_END - FAR0, QB), INT_MIN, I32)
    hi_ref[FAR0:PAD_END, :] = jnp.full((PAD_END - FAR0, QB), -2 ** 15, I16)
    lo_ref[FAR0:PAD_END, :] = jnp.full((PAD_END - FAR0, QB), -2 ** 15, I16)

    n256 = iq + NEAR // 256
    tail = NEAR % 256

    def blk_start(c):
        return pl.multiple_of(FAR0 + c * 256, 128)

    def count(ref, pred):
        def part(start, size):
            o = pred(ref[pl.ds(start, size), :], start).reshape(size // 32, 32, QB)
            parts = [o[i] for i in range(size // 32)]
            while len(parts) > 1:
                parts = [parts[i] + parts[i + 1] for i in range(0, len(parts), 2)]
            return parts[0]

        def body(c, acc):
            return acc + part(blk_start(c), 256)
        acc = lax.fori_loop(0, n256, body, jnp.zeros((32, QB), ref.dtype))
        if tail:
            acc = acc + part(blk_start(n256), tail)
        return jnp.sum(acc.astype(I32), axis=0, keepdims=True)

    one16 = jnp.int16(1)
    zero16 = jnp.int16(0)

    def half_step(ref):
        def step(it, t):
            cand = t + lax.shift_left(jnp.int32(1), 15 - it)
            c16 = cand.astype(I16)
            cnt = count(ref, lambda blk, start: jnp.where(blk >= c16, one16, zero16))
            return jnp.where(cnt >= TOPK, cand, t)
        return step

    half_min = jnp.full((1, QB), -2 ** 15, I32)
    tau_hi = lax.fori_loop(0, 16, half_step(hi_ref), half_min)
    th16 = tau_hi.astype(I16)

    def fold_low(start, size):
        hi = hi_ref[pl.ds(start, size), :]
        lo = lo_ref[pl.ds(start, size), :]
        lo_ref[pl.ds(start, size), :] = jnp.where(hi > th16, jnp.int16(2 ** 15 - 1),
                                                  jnp.where(hi < th16, jnp.int16(-2 ** 15), lo))

    def fold_body(c, carry):
        fold_low(blk_start(c), 256)
        return carry

    lax.fori_loop(0, n256, fold_body, 0)
    if tail:
        fold_low(blk_start(n256), tail)
    tau_lo = lax.fori_loop(0, 16, half_step(lo_ref), half_min)
    tau = lax.shift_left(tau_hi, 16) + (tau_lo + 2 ** 15)

    def count32(pred):
        return count(keys_ref, pred)

    cnt_ge = count32(lambda blk, start: jnp.where(blk >= tau, 1, 0))
    excess = jnp.where(tau > INT_MIN, jnp.where(cnt_ge > TOPK, 1, 0), 0).astype(I32)
    j_ref[...] = jnp.where(tau > INT_MIN, jnp.int32(2 ** 30), jnp.int32(-1))

    @pl.when(jnp.max(excess) > 0)
    def _():
        need = TOPK - count32(lambda blk, start: jnp.where(blk > tau, 1, 0))

        def pos_step(it, lo):
            cand = lo + lax.shift_left(jnp.int32(1), 12 - it)

            def pred(blk, start):
                pos = start + lax.broadcasted_iota(I32, blk.shape, 0)
                return jnp.where(pos < cand, jnp.where(blk == tau, 1, 0), 0)
            return jnp.where(count32(pred) < need, cand, lo)

        lo = lax.fori_loop(0, 13, pos_step, jnp.zeros((1, QB), I32))
        j_ref[...] = jnp.where(excess > 0, lo, j_ref[...])

    j_last = j_ref[...]

    m_ref[...] = jnp.full(m_ref.shape, NEG, F32)
    acc_ref[...] = jnp.zeros(acc_ref.shape, F32)

    def attend(start, size, near, mid=None):
        key = keys_ref[pl.ds(start, size), :]
        pos = start + lax.broadcasted_iota(I32, (size, QB), 0)
        tie = jnp.where(pos <= j_last, 0.0, NEG)
        madd = jnp.where(key > tau, 0.0, jnp.where(key == tau, tie, NEG))
        hs = range(H_A)
        s = [jnp.dot(k_ref[pl.ds(start, size), (h // 2) * 128:(h // 2 + 1) * 128],
                     qtz_ref[h * 128:(h + 1) * 128, :], preferred_element_type=F32) + madd for h in hs]
        if mid is not None:
            mid()
        if near:
            s = [s[h] + bias_ref[h] for h in hs]
        m_all = m_ref[...]
        m_new = [jnp.maximum(m_all[h:h + 1, :], jnp.max(s[h], axis=0, keepdims=True)) for h in hs]
        m_new_all = jnp.concatenate(m_new, axis=0)
        alpha_all = jnp.exp2(m_all - m_new_all)
        m_ref[...] = m_new_all
        p = [jnp.exp2(s[h] - m_new[h]).astype(BF16) for h in hs]
        pv = [jnp.dot(vt_ref[h * VROWS:(h + 1) * VROWS, pl.ds(start, size)], p[h],
                      preferred_element_type=F32) for h in hs]
        for h in hs:
            acc_ref[h * VROWS:(h + 1) * VROWS, :] = (alpha_all[h:h + 1, :] * acc_ref[h * VROWS:(h + 1) * VROWS, :]
                                                     + pv[h])

    def fused_attend(c, carry):
        start = big_start(c)
        attend(start, 2 * KC, False,
               mid=lambda: score_chunk(keys_nx, qit_nx, wt_nx, start, 2 * KC, False))
        return carry

    def far_attend(c, carry):
        attend(big_start(c), 2 * KC, False)
        return carry

    n_fused = jnp.where(iq + 1 < nq, n_big, 0)
    lax.fori_loop(0, n_fused, fused_attend, 0)
    lax.fori_loop(n_fused, n_big, far_attend, 0)

    @pl.when(iq % 2 == 1)
    def _():
        attend(odd0, KC, False)

    attend(near0, NEAR, True)

    for j in range(W_A // 128):
        parts = []
        for h in (2 * j, 2 * j + 1):
            inv = 1.0 / acc_ref[h * VROWS + DH_A:h * VROWS + DH_A + 1, :]
            parts.append(acc_ref[h * VROWS:h * VROWS + DH_A, :] * inv)
        o_ref[:, j * 128:(j + 1) * 128] = jnp.concatenate(parts, axis=0).T


def _attention(qtz, qit, wt, k, vt, ki, bias, t):
    b, tk, _ = k.shape
    qoff = TILE // QB
    nq = t // QB

    def nxt(bi, i):
        return (bi, 0, jnp.minimum(i + 1, nq - 1) + qoff)

    return pl.pallas_call(
        _attn_kernel,
        grid=(b, nq),
        in_specs=[pl.BlockSpec((None, 2 * W_A, QB), lambda bi, i: (bi, 0, i + qoff)),
                  pl.BlockSpec((None, W_A, QB), lambda bi, i: (bi, 0, i + qoff)),
                  pl.BlockSpec((None, H_IDX, QB), lambda bi, i: (bi, 0, i + qoff)),
                  pl.BlockSpec((None, W_A, QB), nxt),
                  pl.BlockSpec((None, H_IDX, QB), nxt),
                  pl.BlockSpec((None, tk, W_A), lambda bi, i: (bi, 0, 0)),
                  pl.BlockSpec((None, H_A * VROWS, tk), lambda bi, i: (bi, 0, 0)),
                  pl.BlockSpec((None, tk, D_IDX), lambda bi, i: (bi, 0, 0)),
                  pl.BlockSpec((H_A, NEAR, QB), lambda bi, i: (0, 0, 0))],
        out_specs=pl.BlockSpec((None, QB, W_A), lambda bi, i: (bi, i, 0)),
        out_shape=jax.ShapeDtypeStruct((b, t, W_A), F32),
        scratch_shapes=[pltpu.VMEM((2, tk, QB), I32),
                        pltpu.VMEM((tk, QB), I16),
                        pltpu.VMEM((tk, QB), I16),
                        pltpu.VMEM((H_A, QB), F32),
                        pltpu.VMEM((H_A * VROWS, QB), F32),
                        pltpu.VMEM((1, QB), I32)],
        compiler_params=_cparams(("arbitrary", "arbitrary")),
        name="attn",
    )(qtz, qit, wt, qit, wt, k, vt, ki, bias)


def _dot_hi(a, b):
    return jnp.dot(a, b, precision=HI, preferred_element_type=F32)


def _mm(a, b):
    return jnp.dot(a.astype(BF16), b.astype(BF16), preferred_element_type=F32)


def _gates(sm, alog, dtb):
    z = sm + dtb
    softplus = jnp.maximum(z, 0.0) + jnp.log(1.0 + jnp.exp(-jnp.abs(z)))
    return -jnp.exp(alog) * softplus, jax.nn.sigmoid(sm)


def _gdn_local(y, g, beta, n_chunks):
    ri = lax.broadcasted_iota(I32, (CHUNK, CHUNK), 0)
    ci = lax.broadcasted_iota(I32, (CHUNK, CHUNK), 1)
    incl = ri >= ci
    strict = ri > ci
    tri = jnp.where(incl, 1.0, 0.0).astype(F32)
    tri_t = jnp.where(ri <= ci, 1.0, 0.0).astype(F32)
    eye = jnp.where(ri == ci, 1.0, 0.0).astype(F32)
    pairs = [(c, h) for c in range(n_chunks) for h in range(H_B)]
    idx = range(len(pairs))
    gc_col, gc_row = [], []
    for c in range(n_chunks):
        g_c = g[c * CHUNK:(c + 1) * CHUNK]
        gc_col.append(_dot_hi(tri, g_c))
        gc_row.append(_dot_hi(g_c.T, tri_t))
    q, kt, vb, qe, decay, kdt, glast = [], [], [], [], [], [], []
    for c, h in pairs:
        yc = y[c * CHUNK:(c + 1) * CHUNK]
        qh = yc[:, h * DK_B:(h + 1) * DK_B]
        kh = yc[:, W_B + h * DK_B:W_B + (h + 1) * DK_B]
        vh = yc[:, 2 * W_B + h * DV_B:2 * W_B + (h + 1) * DV_B]
        gcol = gc_col[c][:, S_AB + h:S_AB + h + 1]
        grow = gc_row[c][S_AB + h:S_AB + h + 1, :]
        bcol = beta[c * CHUNK:(c + 1) * CHUNK, S_BB + h:S_BB + h + 1]
        g_last = gcol[CHUNK - 1:CHUNK, :]
        egc = jnp.exp(gcol)
        kth = kh.T
        kb = kh * bcol
        q.append(jnp.concatenate([kb, qh], axis=0))
        kt.append(kth)
        vb.append(jnp.concatenate([vh * bcol, kb * egc], axis=1))
        qe.append(qh * egc)
        decay.append(jnp.exp(jnp.where(incl, gcol - grow, -jnp.inf)))
        kdt.append(kth * jnp.exp(g_last - grow))
        glast.append(jnp.exp(g_last))
    qk = [_mm(q[i], kt[i]) for i in idx]
    a = [jnp.where(strict, qk[i][0:CHUNK] * decay[i], 0.0) for i in idx]
    att = [qk[i][CHUNK:2 * CHUNK] * decay[i] for i in idx]
    tm = [eye - a[i] for i in idx]
    pw = a
    for _ in range(5):
        pw = [_mm(pw[i], pw[i]) for i in idx]
        tm = [tm[i] + _mm(tm[i], pw[i]) for i in idx]
    uw = [_mm(tm[i], vb[i]) for i in idx]
    return uw, qe, att, kdt, glast


def _gdn_scan(local, c, s_list):
    uw, qe, att, kdt, glast = local
    base = c * H_B
    hs = range(H_B)
    ws = [_mm(jnp.concatenate([uw[base + h][:, DV_B:], qe[base + h]], axis=0), s_list[h]) for h in hs]
    v_new = [uw[base + h][:, :DV_B] - ws[h][0:CHUNK] for h in hs]
    ov = [_mm(jnp.concatenate([att[base + h], kdt[base + h]], axis=0), v_new[h]) for h in hs]
    outs = [ws[h][CHUNK:2 * CHUNK] + ov[h][0:CHUNK] for h in hs]
    s_out = [s_list[h] * glast[base + h] + ov[h][CHUNK:CHUNK + DK_B] for h in hs]
    return outs, s_out


def _gdn_meta_kernel(qkv_m, sm_m, alog, dtb, s_o):
    y = qkv_m[...]
    g, beta = _gates(sm_m[...], alog[...], dtb[...])
    live = lax.broadcasted_iota(I32, (CHUNK, 128), 0) >= CHUNK - N_META
    g = jnp.where(live, g, 0.0)
    beta = jnp.where(live, beta, 0.0)
    _, s_new = _gdn_scan(_gdn_local(y, g, beta, 1), 0, [jnp.zeros((DK_B, DV_B), F32)] * H_B)
    for h in range(H_B):
        s_o[h] = s_new[h]


def _gdn_kernel(qkv_x, zb_x, sm_x, s0, alog, dtb, gnw, o_ref, s_ref):
    @pl.when(pl.program_id(1) == 0)
    def _():
        s_ref[...] = s0[...]

    y = qkv_x[...]
    r = y.shape[0]
    g, beta = _gates(sm_x[...], alog[...], dtb[...])
    local = _gdn_local(y, g, beta, r // CHUNK)
    s_list = [s_ref[h] for h in range(H_B)]
    for c in range(r // CHUNK):
        rows = slice(c * CHUNK, (c + 1) * CHUNK)
        outs, s_list = _gdn_scan(local, c, s_list)
        for h in range(H_B):
            o = outs[h]
            on = o * lax.rsqrt(jnp.mean(o * o, axis=-1, keepdims=True) + EPS) * gnw[...]
            zb = zb_x[rows, h * DV_B:(h + 1) * DV_B]
            o_ref[rows, h * DV_B:(h + 1) * DV_B] = (on * (zb * jax.nn.sigmoid(zb))).astype(BF16)
    for h in range(H_B):
        s_ref[h] = s_list[h]


def _gdn(y_x, y_m, p_g, sm, alog_row, dtb_row, gnw_row):
    b, t, _ = y_x.shape
    r = CHUNK * GDN_CHUNKS_PER_STEP
    last = TILE // CHUNK - 1
    s0 = pl.pallas_call(
        _gdn_meta_kernel,
        grid=(1,),
        in_specs=[pl.BlockSpec((CHUNK, 3 * W_B), lambda i: (last, 0)),
                  pl.BlockSpec((None, CHUNK, 128), lambda i: (0, last, 0)),
                  pl.BlockSpec((1, 128), lambda i: (0, 0)),
                  pl.BlockSpec((1, 128), lambda i: (0, 0))],
        out_specs=pl.BlockSpec((H_B, DK_B, DV_B), lambda i: (0, 0, 0)),
        out_shape=jax.ShapeDtypeStruct((H_B, DK_B, DV_B), F32),
        compiler_params=_cparams(("arbitrary",)),
        name="gdn_meta",
    )(y_m, sm, alog_row, dtb_row)
    return pl.pallas_call(
        _gdn_kernel,
        grid=(b, t // r),
        in_specs=[pl.BlockSpec((None, r, 3 * W_B), lambda bi, n: (bi, n, 0)),
                  pl.BlockSpec((None, r, W_B), lambda bi, n: (bi, n, G_ZB)),
                  pl.BlockSpec((None, r, 128), lambda bi, n: (bi, n + TILE // r, 0)),
                  pl.BlockSpec((H_B, DK_B, DV_B), lambda bi, n: (0, 0, 0)),
                  pl.BlockSpec((1, 128), lambda bi, n: (0, 0)),
                  pl.BlockSpec((1, 128), lambda bi, n: (0, 0)),
                  pl.BlockSpec((1, DV_B), lambda bi, n: (0, 0))],
        out_specs=pl.BlockSpec((None, r, W_B), lambda bi, n: (bi, n, 0)),
        out_shape=jax.ShapeDtypeStruct((b, t, W_B), BF16),
        scratch_shapes=[pltpu.VMEM((H_B, DK_B, DV_B), F32)],
        compiler_params=_cparams(("arbitrary", "arbitrary")),
        name="gdn",
    )(y_x, p_g, sm, s0, alog_row, dtb_row, gnw_row)


def _merge_kernel(ya_ref, za_ref, og_ref, ga_ref, gb_ref, x_ref, wpa, wpb, wout, o_ref):
    za = za_ref[...]
    ya = (ya_ref[...] * (za * jax.nn.sigmoid(za))).astype(BF16)
    ya = jnp.dot(ya, wpa[...], preferred_element_type=F32)
    yb = jnp.dot(og_ref[...], wpb[...], preferred_element_type=F32)
    m = jax.nn.sigmoid(ga_ref[...]) * ya + jax.nn.sigmoid(gb_ref[...]) * yb
    o_ref[...] = x_ref[...] + jnp.dot(m.astype(BF16), wout[...], preferred_element_type=F32)


def _merge(ya, za, og, p_g, x, wpa, wpb, wout):
    b, t, _ = x.shape
    const = lambda bi, i: (0, 0)
    return pl.pallas_call(
        _merge_kernel,
        grid=(b, t // TILE),
        in_specs=[pl.BlockSpec((None, TILE, W_A), lambda bi, i: (bi, i, 0)),
                  pl.BlockSpec((None, TILE, W_A), lambda bi, i: (bi, i + 1, 0)),
                  pl.BlockSpec((None, TILE, W_B), lambda bi, i: (bi, i, 0)),
                  pl.BlockSpec((None, TILE, D_MODEL), lambda bi, i: (bi, i, G_GA)),
                  pl.BlockSpec((None, TILE, D_MODEL), lambda bi, i: (bi, i, G_GB)),
                  pl.BlockSpec((None, TILE, D_MODEL), lambda bi, i: (bi, i, 0)),
                  pl.BlockSpec((W_A, D_MODEL), const),
                  pl.BlockSpec((W_B, D_MODEL), const),
                  pl.BlockSpec((D_MODEL, D_MODEL), const)],
        out_specs=pl.BlockSpec((None, TILE, D_MODEL), lambda bi, i: (bi, i, 0)),
        out_shape=jax.ShapeDtypeStruct((b, t, D_MODEL), F32),
        compiler_params=_cparams(("arbitrary", "arbitrary")),
        name="merge",
    )(ya, za, og, p_g, p_g, x, wpa, wpb, wout)


def _permute_w_in(w):
    sizes = (W_A, W_A, W_A, W_A, H_IDX * D_IDX, D_IDX, H_IDX,
             H_B * DK_B, H_B * DK_B, W_B, W_B, H_B, H_B, D_MODEL, D_MODEL)
    offs = [0]
    for s in sizes:
        offs.append(offs[-1] + s)
    attn_end, qkv_end, zb_end, small_end = offs[7], offs[10], offs[11], offs[13]

    def run(lo, hi):
        return w[:, lo:hi].astype(BF16)

    pad = jnp.zeros((w.shape[0], 128 - (D_IDX + H_IDX + 2 * H_B)), BF16)
    return (run(attn_end, qkv_end),
            jnp.concatenate([run(qkv_end, zb_end), run(small_end, offs[15])], axis=1),
            jnp.concatenate([run(0, attn_end), run(zb_end, small_end), pad], axis=1))


def _lane_row(vals, offset):
    return jnp.zeros((1, 128), F32).at[0, offset:offset + vals.shape[0]].set(vals.astype(F32))


def _layer(x, meta_tokens, rel_bias, norm_w, w_in, q_norm_w, k_norm_w, conv_w, a_log, dt_bias,
           gdn_norm_w, w_proj_a, w_proj_b, w_out):
    b, t, _ = x.shape
    w_qkv, w_gate, w_attn = _permute_w_in(w_in)
    nw_row = norm_w.reshape(1, D_MODEL)
    meta_pad = jnp.zeros((TILE, D_MODEL), F32).at[PAD_END:].set(meta_tokens)
    x2d = x.reshape(b * t, D_MODEL)

    y_m, tail_m, _ = _qkv_proj(meta_pad, nw_row, w_qkv, w_gate, conv_w, jnp.zeros((8, 3 * W_B), F32), 1)
    y_x, _, p_g = _qkv_proj(x2d, nw_row, w_qkv, w_gate, conv_w, tail_m, t // TILE)
    p_g = p_g.reshape(b, t, 3 * W_B)

    hid = lax.broadcasted_iota(I32, (W_A, W_A), 0) // DH_A
    bd = (hid == hid.T).astype(BF16)
    za, sm, k_n, v_t, k_i, q_tz, q_it, w_t = _attn_proj(
        x, meta_pad, nw_row, w_attn, jnp.tile(q_norm_w, H_A).reshape(1, W_A),
        jnp.tile(k_norm_w, H_A).reshape(1, W_A), bd)
    ya = _attention(q_tz, q_it, w_t, k_n, v_t, k_i, _bias_tile(rel_bias), t)

    og = _gdn(y_x.reshape(b, t, 3 * W_B), y_m, p_g, sm, _lane_row(a_log, S_AB), _lane_row(dt_bias, S_AB),
              gdn_norm_w.reshape(1, DV_B))

    return _merge(ya, za, og, p_g, x, w_proj_a.astype(BF16), w_proj_b.astype(BF16), w_out.astype(BF16))


def kernel(x, meta_tokens, rel_bias, norm_w, w_in, q_norm_w, k_norm_w, conv_w, a_log, dt_bias,
           gdn_norm_w, w_proj_a, w_proj_b, w_out):
    depth = norm_w.shape[0]
    assert depth == 1, "meta rows are dropped after the layer; deeper stacks need them carried"
    return _layer(x, meta_tokens, rel_bias, norm_w[0], w_in[0], q_norm_w[0], k_norm_w[0], conv_w[0],
                  a_log[0], dt_bias[0], gdn_norm_w[0], w_proj_a[0], w_proj_b[0], w_out[0])
```

```python
import functools
import math

import jax
import jax.numpy as jnp
from jax import lax
from jax.experimental import pallas as pl
from jax.experimental.pallas import tpu as pltpu

F32 = jnp.float32
BF16 = jnp.bfloat16
I32 = jnp.int32
I16 = jnp.int16

D_MODEL = 1024
N_META = 16
H_A = 8
DH_A = 64
W_A = H_A * DH_A
H_IDX = 8
D_IDX = 64
TOPK = 256
N_BUCKETS = 32
MAX_DISTANCE = 128
H_B = 8
DK_B = 128
DV_B = 128
W_B = H_B * DV_B
CONV_K = 4
CHUNK = 64
EPS = 1e-6

G_ZB, G_GA, G_GB = 0, 1, 2
A_Q, A_K, A_V, A_Z, A_QI, A_S = 0, 512, 1024, 1536, 2048, 2560
N_ATTN = A_S + 128
S_KI, S_WI, S_BB, S_AB = 0, 64, 72, 80

TILE = 512
QB = 256
KC = 256
NEAR = QB + 128
FAR0 = TILE - 128
PAD_END = TILE - N_META
VROWS = DH_A + 16
LOG2E = math.log2(math.e)
INT_MIN = -2 ** 31
NEG = -1e30
VMEM_LIMIT = 58 * 1024 * 1024
HI = lax.Precision.HIGHEST
GDN_CHUNKS_PER_STEP = 4


def _cparams(sem):
    return pltpu.CompilerParams(dimension_semantics=sem, vmem_limit_bytes=VMEM_LIMIT)


def _rms_bf16(x, nw):
    ms = jnp.mean(x * x, axis=-1, keepdims=True)
    return (x * lax.rsqrt(ms + EPS) * nw).astype(BF16)


def _qkv_proj_kernel(x_ref, nw_ref, w_ref, wg_ref, convw, tail_in, y_o, tail_o, g_o, xb0, xb1, xb2,
                     *, tiles_per_seq):
    xbufs = (xb0, xb1, xb2)

    def cols(g):
        return slice(g * W_B, (g + 1) * W_B)

    @pl.when(pl.program_id(0) % tiles_per_seq == 0)
    def _():
        for g in range(3):
            xbufs[g][0:8, :] = tail_in[:, cols(g)]

    h = _rms_bf16(x_ref[...], nw_ref[...])

    unit = 2 * DK_B
    n_units = 3 * W_B // unit

    def where(u):
        g, lo = divmod(u * unit, W_B)
        return g, slice(lo, lo + unit), slice(u * unit, (u + 1) * unit)

    def project(u):
        g, cg, ca = where(u)
        xbufs[g][8:8 + TILE, cg] = jnp.dot(h, w_ref[:, ca], preferred_element_type=F32)

    def gate(u):
        _, _, ca = where(u)
        g_o[:, ca] = jnp.dot(h, wg_ref[:, ca], preferred_element_type=F32)

    def epilogue(u):
        g, cg, ca = where(u)
        xbuf = xbufs[g]
        y = convw[CONV_K - 1:CONV_K, ca] * xbuf[8:8 + TILE, cg]
        for i in range(CONV_K - 1):
            off = 8 - (CONV_K - 1) + i
            y = y + convw[i:i + 1, ca] * xbuf[off:off + TILE, cg]
        y = y * jax.nn.sigmoid(y)
        tail = xbuf[TILE:TILE + 8, cg]
        tail_o[:, ca] = tail
        xbuf[0:8, cg] = tail
        if g == 2:
            y_o[:, ca] = y
        else:
            scale = DK_B ** -0.5 if g == 0 else 1.0
            for hh in range(unit // DK_B):
                yh = y[:, hh * DK_B:(hh + 1) * DK_B]
                inv = lax.rsqrt(jnp.sum(yh * yh, axis=-1, keepdims=True) + EPS) * scale
                y_o[:, u * unit + hh * DK_B:u * unit + (hh + 1) * DK_B] = yh * inv

    project(0)
    for u in range(n_units):
        gate(u)
        if u + 1 < n_units:
            project(u + 1)
        epilogue(u)


def _qkv_proj(x2d, norm_w_row, w_bf16, wg_bf16, convw, tail_in, tiles_per_seq):
    m = x2d.shape[0]
    n = 3 * W_B
    once = pl.Buffered(1)
    return pl.pallas_call(
        functools.partial(_qkv_proj_kernel, tiles_per_seq=tiles_per_seq),
        grid=(m // TILE,),
        in_specs=[pl.BlockSpec((TILE, D_MODEL), lambda i: (i, 0)),
                  pl.BlockSpec((1, D_MODEL), lambda i: (0, 0)),
                  pl.BlockSpec((D_MODEL, n), lambda i: (0, 0), pipeline_mode=once),
                  pl.BlockSpec((D_MODEL, n), lambda i: (0, 0), pipeline_mode=once),
                  pl.BlockSpec((CONV_K, n), lambda i: (0, 0)),
                  pl.BlockSpec((8, n), lambda i: (0, 0))],
        out_specs=[pl.BlockSpec((TILE, n), lambda i: (i, 0)),
                   pl.BlockSpec((8, n), lambda i: (i, 0)),
                   pl.BlockSpec((TILE, n), lambda i: (i, 0))],
        out_shape=[jax.ShapeDtypeStruct((m, n), F32),
                   jax.ShapeDtypeStruct((m // TILE * 8, n), F32),
                   jax.ShapeDtypeStruct((m, n), F32)],
        scratch_shapes=[pltpu.VMEM((8 + TILE, W_B), F32)] * 3,
        compiler_params=_cparams(("arbitrary",)),
        name="qkv_proj",
    )(x2d, norm_w_row, w_bf16, wg_bf16, convw, tail_in)


def _head_rms(x, bd, w_row):
    sq = x * x
    hi = sq.astype(BF16)
    lo = (sq - hi.astype(F32)).astype(BF16)
    ss = (jnp.dot(hi, bd, preferred_element_type=F32) + jnp.dot(lo, bd, preferred_element_type=F32))
    return x * lax.rsqrt(ss * (1.0 / DH_A) + EPS) * w_row


def _attn_proj_kernel(x_ref, xm_ref, nw_ref, w_ref, qnw, knw, bd_ref,
                      za_o, sm_o, k_o, vt_o, ki_o, qtz_o, qit_o, wt_o):
    x = jnp.where(pl.program_id(1) == 0, xm_ref[...], x_ref[...])
    h = _rms_bf16(x, nw_ref[...])
    bd = bd_ref[...]

    def proj(c0, width):
        return jnp.dot(h, w_ref[:, c0:c0 + width], preferred_element_type=F32)

    sm = proj(A_S, 128)
    qi = proj(A_QI, W_A)
    sm_o[...] = sm
    ki_o[...] = sm[:, S_KI:S_KI + D_IDX].astype(BF16)
    wt_o[...] = sm.T[S_WI:S_WI + H_IDX, :] * ((H_IDX ** -0.5) * (D_IDX ** -0.5))

    q = proj(A_Q, W_A)
    for j in range(W_A // 128):
        qit_o[j * 128:(j + 1) * 128, :] = qi[:, j * 128:(j + 1) * 128].T.astype(BF16)

    k = proj(A_K, W_A)
    qn = _head_rms(q, bd, qnw[...]) * (DH_A ** -0.5 * LOG2E)
    zeros = jnp.zeros((DH_A, TILE), BF16)
    for j in range(W_A // 128):
        t = qn[:, j * 128:(j + 1) * 128].T.astype(BF16)
        base = 2 * j * 128
        qtz_o[base:base + 64, :] = t[0:64]
        qtz_o[base + 64:base + 128, :] = zeros
        qtz_o[base + 128:base + 192, :] = zeros
        qtz_o[base + 192:base + 256, :] = t[64:128]

    v = proj(A_V, W_A)
    k_o[...] = _head_rms(k, bd, knw[...]).astype(BF16)

    za = proj(A_Z, W_A)
    ones_rows = jnp.where(lax.broadcasted_iota(I32, (VROWS - DH_A, TILE), 0) == 0, 1.0, 0.0).astype(BF16)
    for j in range(W_A // 128):
        t = v[:, j * 128:(j + 1) * 128].T.astype(BF16)
        for r in range(2):
            base = (2 * j + r) * VROWS
            vt_o[base:base + DH_A, :] = t[r * DH_A:(r + 1) * DH_A]
            vt_o[base + DH_A:base + VROWS, :] = ones_rows
    za_o[...] = za


def _attn_proj(x, meta_pad, norm_w_row, w_bf16, qnw_row, knw_row, bd):
    b, t, _ = x.shape
    nt = t // TILE + 1
    tk = nt * TILE

    def rows(width, dtype):
        return (pl.BlockSpec((None, TILE, width), lambda bi, i: (bi, i, 0)),
                jax.ShapeDtypeStruct((b, tk, width), dtype))

    def cols(height, dtype):
        return (pl.BlockSpec((None, height, TILE), lambda bi, i: (bi, 0, i)),
                jax.ShapeDtypeStruct((b, height, tk), dtype))

    outs = [rows(W_A, F32), rows(128, F32), rows(W_A, BF16), cols(H_A * VROWS, BF16), rows(D_IDX, BF16),
            cols(2 * W_A, BF16), cols(W_A, BF16), cols(H_IDX, F32)]
    const = lambda bi, i: (0, 0)
    return pl.pallas_call(
        _attn_proj_kernel,
        grid=(b, nt),
        in_specs=[pl.BlockSpec((None, TILE, D_MODEL), lambda bi, i: (bi, jnp.maximum(i - 1, 0), 0)),
                  pl.BlockSpec((TILE, D_MODEL), const),
                  pl.BlockSpec((1, D_MODEL), const),
                  pl.BlockSpec((D_MODEL, N_ATTN), const),
                  pl.BlockSpec((1, W_A), const),
                  pl.BlockSpec((1, W_A), const),
                  pl.BlockSpec((W_A, W_A), const)],
        out_specs=[o[0] for o in outs],
        out_shape=[o[1] for o in outs],
        compiler_params=_cparams(("arbitrary", "arbitrary")),
        name="attn_proj",
    )(x, meta_pad, norm_w_row, w_bf16, qnw_row, knw_row, bd)


def _bias_kernel(rb_ref, o_ref):
    a = lax.broadcasted_iota(I32, (NEAR, QB), 1)
    bk = lax.broadcasted_iota(I32, (NEAR, QB), 0)
    d = jnp.maximum(a - bk + (NEAR - QB), 0)
    max_exact = N_BUCKETS // 2
    ratio = jnp.maximum(d, max_exact).astype(F32) / max_exact
    large = max_exact + (jnp.log(ratio) / math.log(MAX_DISTANCE / max_exact)
                         * (N_BUCKETS - max_exact)).astype(I32)
    large = jnp.minimum(large, N_BUCKETS - 1)
    bucket = jnp.where(d < max_exact, d, large)
    for h in range(H_A):
        acc = jnp.zeros((NEAR, QB), F32)
        for bb in range(N_BUCKETS):
            acc = jnp.where(bucket == bb, rb_ref[bb, h], acc)
        o_ref[h] = (acc - rb_ref[N_BUCKETS - 1, h]) * LOG2E


def _bias_tile(rel_bias):
    return pl.pallas_call(
        _bias_kernel,
        in_specs=[pl.BlockSpec(memory_space=pltpu.SMEM)],
        out_specs=pl.BlockSpec((H_A, NEAR, QB), lambda: (0, 0, 0)),
        out_shape=jax.ShapeDtypeStruct((H_A, NEAR, QB), F32),
        name="bias_tile",
    )(rel_bias)


def _attn_kernel(qtz_ref, qit_ref, wt_ref, qit_nx, wt_nx, k_ref, vt_ref, ki_ref, bias_ref, o_ref,
                 keys2_ref, hi_ref, lo_ref, m_ref, acc_ref, j_ref):
    iq = pl.program_id(1)
    nq = pl.num_programs(1)
    keys_ref = keys2_ref.at[iq % 2]
    keys_nx = keys2_ref.at[(iq + 1) % 2]
    near0 = pl.multiple_of(FAR0 + iq * QB, 128)
    q_pos = TILE + iq * QB + lax.broadcasted_iota(I32, (1, QB), 1)

    def score_chunk(kref, qit, wt, start, size, mask_causal):
        kic = ki_ref[pl.ds(start, size), :]
        acc = jnp.zeros((size, QB), F32)
        for h in range(H_IDX):
            r = jnp.dot(kic, qit[h * D_IDX:(h + 1) * D_IDX, :], preferred_element_type=F32)
            acc = acc + jnp.maximum(r, 0.0) * wt[h:h + 1, :]
        acc = acc + 0.0
        bits = pltpu.bitcast(acc, I32)
        key = jnp.where(bits < 0, bits ^ jnp.int32(0x7FFFFFFF), bits)
        if mask_causal:
            pos = start + lax.broadcasted_iota(I32, (size, QB), 0)
            key = jnp.where(pos <= q_pos, key, INT_MIN)
        kref[pl.ds(start, size), :] = key
        hi_ref[pl.ds(start, size), :] = lax.shift_right_arithmetic(key, 16).astype(I16)
        lo_ref[pl.ds(start, size), :] = ((key & 0xFFFF) - 2 ** 15).astype(I16)

    n_big = iq // 2
    n_prev = jnp.maximum(iq - 1, 0) // 2
    odd0 = pl.multiple_of(FAR0 + n_big * 2 * KC, 128)

    def big_start(c):
        return pl.multiple_of(FAR0 + c * 2 * KC, 128)

    def far_scores(c, carry):
        score_chunk(keys_ref, qit_ref, wt_ref, big_start(c), 2 * KC, False)
        return carry

    lax.fori_loop(n_prev, n_big, far_scores, 0)

    @pl.when(iq % 2 == 1)
    def _():
        score_chunk(keys_ref, qit_ref, wt_ref, odd0, KC, False)

    score_chunk(keys_ref, qit_ref, wt_ref, near0, NEAR, True)
    keys_ref[FAR0:PAD_END, :] = jnp.full((PAD_END - FAR0, QB), INT_MIN, I32)
    hi_ref[FAR0:PAD_END, :] = jnp.full((PAD_END - FAR0, QB), -2 ** 15, I16)
    lo_ref[FAR0:PAD_END, :] = jnp.full((PAD_END - FAR0, QB), -2 ** 15, I16)

    n256 = iq + NEAR // 256
    tail = NEAR % 256

    def blk_start(c):
        return pl.multiple_of(FAR0 + c * 256, 128)

    def count(ref, pred):
        def part(start, size):
            o = pred(ref[pl.ds(start, size), :], start).reshape(size // 32, 32, QB)
            parts = [o[i] for i in range(size // 32)]
            while len(parts) > 1:
                parts = [parts[i] + parts[i + 1] for i in range(0, len(parts), 2)]
            return parts[0]

        def body(c, acc):
            return acc + part(blk_start(c), 256)
        acc = lax.fori_loop(0, n256, body, jnp.zeros((32, QB), ref.dtype))
        if tail:
            acc = acc + part(blk_start(n256), tail)
        return jnp.sum(acc.astype(I32), axis=0, keepdims=True)

    one16 = jnp.int16(1)
    zero16 = jnp.int16(0)

    def count_ge16(ref, cand):
        c16 = cand.astype(I16)
        return count(ref, lambda blk, start: jnp.where(blk >= c16, one16, zero16))

    def half_step(ref):
        def step(it, state):
            t, c = state
            cand = t + lax.shift_left(jnp.int32(1), 15 - it)
            cnt = count_ge16(ref, cand)
            ok = cnt >= TOPK
            return jnp.where(ok, cand, t), jnp.where(ok, cnt, c)
        return step

    low16 = jnp.int16(-2 ** 15)

    def max16(x, y):
        return jnp.where(y > x, y, x)

    def group_max(c, g):
        return max16(g, hi_ref[pl.ds(blk_start(c), 256), :])

    gmax = lax.fori_loop(0, n256, group_max, jnp.full((256, QB), low16, I16))
    if tail:
        gmax = max16(gmax, jnp.concatenate(
            [hi_ref[pl.ds(blk_start(n256), tail), :], jnp.full((256 - tail, QB), low16, I16)], axis=0))
    parts = gmax.reshape(8, 32, QB)
    gmin = parts[0]
    gtop = parts[0]
    for i in range(1, 8):
        gmin = jnp.where(parts[i] < gmin, parts[i], gmin)
        gtop = max16(gtop, parts[i])
    lo_b = jnp.min(gmin.astype(I32)) + 2 ** 15
    hi_b = jnp.max(gtop.astype(I32)) + 2 ** 15
    diff = lo_b ^ hi_b
    n_fixed = jnp.int32(0)
    for kbit in range(1, 17):
        n_fixed = n_fixed + jnp.where(diff < 2 ** (16 - kbit), 1, 0)
    free_bits = 16 - n_fixed
    t0 = lax.shift_left(lax.shift_right_logical(lo_b, free_bits), free_bits) - 2 ** 15
    big = jnp.full((1, QB), 2 ** 30, I32)
    tau_hi, _ = lax.fori_loop(n_fixed, 16, half_step(hi_ref), (jnp.full((1, QB), t0, I32), big))
    th16 = tau_hi.astype(I16)
    cnt_hi = count_ge16(hi_ref, tau_hi)

    def fold_low(start, size):
        hi = hi_ref[pl.ds(start, size), :]
        lo = lo_ref[pl.ds(start, size), :]
        lo_ref[pl.ds(start, size), :] = jnp.where(hi > th16, jnp.int16(2 ** 15 - 1),
                                                  jnp.where(hi < th16, jnp.int16(-2 ** 15), lo))

    def fold_body(c, carry):
        fold_low(blk_start(c), 256)
        return carry

    lax.fori_loop(0, n256, fold_body, 0)
    if tail:
        fold_low(blk_start(n256), tail)
    def unresolved(state):
        return jnp.max(jnp.where(state[1] != TOPK, 1, 0)) > 0

    state = lax.fori_loop(0, 10, half_step(lo_ref), (jnp.full((1, QB), -2 ** 15, I32), cnt_hi))
    for lo_bit, hi_bit in ((10, 13), (13, 16)):
        end = jnp.where(unresolved(state), hi_bit, lo_bit)
        state = lax.fori_loop(lo_bit, end, half_step(lo_ref), state)
    tau_lo = state[0]
    tau = lax.shift_left(tau_hi, 16) + (tau_lo + 2 ** 15)

    def count32(pred):
        return count(keys_ref, pred)

    cnt_ge = count32(lambda blk, start: jnp.where(blk >= tau, 1, 0))
    excess = jnp.where(tau > INT_MIN, jnp.where(cnt_ge > TOPK, 1, 0), 0).astype(I32)
    j_ref[...] = jnp.where(tau > INT_MIN, jnp.int32(2 ** 30), jnp.int32(-1))

    @pl.when(jnp.max(excess) > 0)
    def _():
        need = TOPK - count32(lambda blk, start: jnp.where(blk > tau, 1, 0))

        def pos_step(it, lo):
            cand = lo + lax.shift_left(jnp.int32(1), 12 - it)

            def pred(blk, start):
                pos = start + lax.broadcasted_iota(I32, blk.shape, 0)
                return jnp.where(pos < cand, jnp.where(blk == tau, 1, 0), 0)
            return jnp.where(count32(pred) < need, cand, lo)

        lo = lax.fori_loop(0, 13, pos_step, jnp.zeros((1, QB), I32))
        j_ref[...] = jnp.where(excess > 0, lo, j_ref[...])

    j_last = j_ref[...]

    m_ref[...] = jnp.full(m_ref.shape, NEG, F32)
    acc_ref[...] = jnp.zeros(acc_ref.shape, F32)

    def attend(start, size, near, mid=None):
        key = keys_ref[pl.ds(start, size), :]
        pos = start + lax.broadcasted_iota(I32, (size, QB), 0)
        tie = jnp.where(pos <= j_last, 0.0, NEG)
        madd = jnp.where(key > tau, 0.0, jnp.where(key == tau, tie, NEG))
        hs = range(H_A)
        s = [jnp.dot(k_ref[pl.ds(start, size), (h // 2) * 128:(h // 2 + 1) * 128],
                     qtz_ref[h * 128:(h + 1) * 128, :], preferred_element_type=F32) + madd for h in hs]
        if mid is not None:
            mid()
        if near:
            s = [s[h] + bias_ref[h] for h in hs]
        m_all = m_ref[...]
        m_new = [jnp.maximum(m_all[h:h + 1, :], jnp.max(s[h], axis=0, keepdims=True)) for h in hs]
        m_new_all = jnp.concatenate(m_new, axis=0)
        alpha_all = jnp.exp2(m_all - m_new_all)
        m_ref[...] = m_new_all
        p = [jnp.exp2(s[h] - m_new[h]).astype(BF16) for h in hs]
        pv = [jnp.dot(vt_ref[h * VROWS:(h + 1) * VROWS, pl.ds(start, size)], p[h],
                      preferred_element_type=F32) for h in hs]
        for h in hs:
            acc_ref[h * VROWS:(h + 1) * VROWS, :] = (alpha_all[h:h + 1, :] * acc_ref[h * VROWS:(h + 1) * VROWS, :]
                                                     + pv[h])

    def fused_attend(c, carry):
        start = big_start(c)
        attend(start, 2 * KC, False,
               mid=lambda: score_chunk(keys_nx, qit_nx, wt_nx, start, 2 * KC, False))
        return carry

    def far_attend(c, carry):
        attend(big_start(c), 2 * KC, False)
        return carry

    n_fused = jnp.where(iq + 1 < nq, n_big, 0)
    lax.fori_loop(0, n_fused, fused_attend, 0)
    lax.fori_loop(n_fused, n_big, far_attend, 0)

    @pl.when(iq % 2 == 1)
    def _():
        attend(odd0, KC, False)

    attend(near0, NEAR, True)

    for j in range(W_A // 128):
        parts = []
        for h in (2 * j, 2 * j + 1):
            inv = 1.0 / acc_ref[h * VROWS + DH_A:h * VROWS + DH_A + 1, :]
            parts.append(acc_ref[h * VROWS:h * VROWS + DH_A, :] * inv)
        o_ref[:, j * 128:(j + 1) * 128] = jnp.concatenate(parts, axis=0).T


def _attention(qtz, qit, wt, k, vt, ki, bias, t):
    b, tk, _ = k.shape
    qoff = TILE // QB
    nq = t // QB

    def nxt(bi, i):
        return (bi, 0, jnp.minimum(i + 1, nq - 1) + qoff)

    return pl.pallas_call(
        _attn_kernel,
        grid=(b, nq),
        in_specs=[pl.BlockSpec((None, 2 * W_A, QB), lambda bi, i: (bi, 0, i + qoff)),
                  pl.BlockSpec((None, W_A, QB), lambda bi, i: (bi, 0, i + qoff)),
                  pl.BlockSpec((None, H_IDX, QB), lambda bi, i: (bi, 0, i + qoff)),
                  pl.BlockSpec((None, W_A, QB), nxt),
                  pl.BlockSpec((None, H_IDX, QB), nxt),
                  pl.BlockSpec((None, tk, W_A), lambda bi, i: (bi, 0, 0)),
                  pl.BlockSpec((None, H_A * VROWS, tk), lambda bi, i: (bi, 0, 0)),
                  pl.BlockSpec((None, tk, D_IDX), lambda bi, i: (bi, 0, 0)),
                  pl.BlockSpec((H_A, NEAR, QB), lambda bi, i: (0, 0, 0))],
        out_specs=pl.BlockSpec((None, QB, W_A), lambda bi, i: (bi, i, 0)),
        out_shape=jax.ShapeDtypeStruct((b, t, W_A), F32),
        scratch_shapes=[pltpu.VMEM((2, tk, QB), I32),
                        pltpu.VMEM((tk, QB), I16),
                        pltpu.VMEM((tk, QB), I16),
                        pltpu.VMEM((H_A, QB), F32),
                        pltpu.VMEM((H_A * VROWS, QB), F32),
                        pltpu.VMEM((1, QB), I32)],
        compiler_params=_cparams(("arbitrary", "arbitrary")),
        name="attn",
    )(qtz, qit, wt, qit, wt, k, vt, ki, bias)


def _dot_hi(a, b):
    return jnp.dot(a, b, precision=HI, preferred_element_type=F32)


def _mm(a, b):
    return jnp.dot(a.astype(BF16), b.astype(BF16), preferred_element_type=F32)


def _gates(sm, alog, dtb):
    z = sm + dtb
    softplus = jnp.maximum(z, 0.0) + jnp.log(1.0 + jnp.exp(-jnp.abs(z)))
    return -jnp.exp(alog) * softplus, jax.nn.sigmoid(sm)


def _gdn_local(y, g, beta, n_chunks):
    ri = lax.broadcasted_iota(I32, (CHUNK, CHUNK), 0)
    ci = lax.broadcasted_iota(I32, (CHUNK, CHUNK), 1)
    incl = ri >= ci
    strict = ri > ci
    tri = jnp.where(incl, 1.0, 0.0).astype(F32)
    tri_t = jnp.where(ri <= ci, 1.0, 0.0).astype(F32)
    eye = jnp.where(ri == ci, 1.0, 0.0).astype(F32)
    pairs = [(c, h) for c in range(n_chunks) for h in range(H_B)]
    idx = range(len(pairs))
    gc_col, gc_row = [], []
    for c in range(n_chunks):
        g_c = g[c * CHUNK:(c + 1) * CHUNK]
        gc_col.append(_dot_hi(tri, g_c))
        gc_row.append(_dot_hi(g_c.T, tri_t))
    q, kt, vb, qe, decay, kdt, glast = [], [], [], [], [], [], []
    for c, h in pairs:
        yc = y[c * CHUNK:(c + 1) * CHUNK]
        qh = yc[:, h * DK_B:(h + 1) * DK_B]
        kh = yc[:, W_B + h * DK_B:W_B + (h + 1) * DK_B]
        vh = yc[:, 2 * W_B + h * DV_B:2 * W_B + (h + 1) * DV_B]
        gcol = gc_col[c][:, S_AB + h:S_AB + h + 1]
        grow = gc_row[c][S_AB + h:S_AB + h + 1, :]
        bcol = beta[c * CHUNK:(c + 1) * CHUNK, S_BB + h:S_BB + h + 1]
        g_last = gcol[CHUNK - 1:CHUNK, :]
        egc = jnp.exp(gcol)
        kth = kh.T
        kb = kh * bcol
        q.append(jnp.concatenate([kb, qh], axis=0))
        kt.append(kth)
        vb.append(jnp.concatenate([vh * bcol, kb * egc], axis=1))
        qe.append(qh * egc)
        decay.append(jnp.exp(jnp.where(incl, gcol - grow, -jnp.inf)))
        kdt.append(kth * jnp.exp(g_last - grow))
        glast.append(jnp.exp(g_last))
    qk = [_mm(q[i], kt[i]) for i in idx]
    a = [jnp.where(strict, qk[i][0:CHUNK] * decay[i], 0.0) for i in idx]
    att = [qk[i][CHUNK:2 * CHUNK] * decay[i] for i in idx]
    tm = [eye - a[i] for i in idx]
    pw = a
    for _ in range(5):
        pw = [_mm(pw[i], pw[i]) for i in idx]
        tm = [tm[i] + _mm(tm[i], pw[i]) for i in idx]
    uw = [_mm(tm[i], vb[i]) for i in idx]
    return uw, qe, att, kdt, glast


def _gdn_scan(local, c, s_list):
    uw, qe, att, kdt, glast = local
    base = c * H_B
    hs = range(H_B)
    ws = [_mm(jnp.concatenate([uw[base + h][:, DV_B:], qe[base + h]], axis=0), s_list[h]) for h in hs]
    v_new = [uw[base + h][:, :DV_B] - ws[h][0:CHUNK] for h in hs]
    ov = [_mm(jnp.concatenate([att[base + h], kdt[base + h]], axis=0), v_new[h]) for h in hs]
    outs = [ws[h][CHUNK:2 * CHUNK] + ov[h][0:CHUNK] for h in hs]
    s_out = [s_list[h] * glast[base + h] + ov[h][CHUNK:CHUNK + DK_B] for h in hs]
    return outs, s_out


def _gdn_meta_kernel(qkv_m, sm_m, alog, dtb, s_o):
    y = qkv_m[...]
    g, beta = _gates(sm_m[...], alog[...], dtb[...])
    live = lax.broadcasted_iota(I32, (CHUNK, 128), 0) >= CHUNK - N_META
    g = jnp.where(live, g, 0.0)
    beta = jnp.where(live, beta, 0.0)
    _, s_new = _gdn_scan(_gdn_local(y, g, beta, 1), 0, [jnp.zeros((DK_B, DV_B), F32)] * H_B)
    for h in range(H_B):
        s_o[h] = s_new[h]


def _gdn_kernel(qkv_x, zb_x, sm_x, s0, alog, dtb, gnw, o_ref, s_ref):
    @pl.when(pl.program_id(1) == 0)
    def _():
        s_ref[...] = s0[...]

    y = qkv_x[...]
    r = y.shape[0]
    g, beta = _gates(sm_x[...], alog[...], dtb[...])
    local = _gdn_local(y, g, beta, r // CHUNK)
    s_list = [s_ref[h] for h in range(H_B)]
    for c in range(r // CHUNK):
        rows = slice(c * CHUNK, (c + 1) * CHUNK)
        outs, s_list = _gdn_scan(local, c, s_list)
        for h in range(H_B):
            o = outs[h]
            on = o * lax.rsqrt(jnp.mean(o * o, axis=-1, keepdims=True) + EPS) * gnw[...]
            zb = zb_x[rows, h * DV_B:(h + 1) * DV_B]
            o_ref[rows, h * DV_B:(h + 1) * DV_B] = (on * (zb * jax.nn.sigmoid(zb))).astype(BF16)
    for h in range(H_B):
        s_ref[h] = s_list[h]


def _gdn(y_x, y_m, p_g, sm, alog_row, dtb_row, gnw_row):
    b, t, _ = y_x.shape
    r = CHUNK * GDN_CHUNKS_PER_STEP
    last = TILE // CHUNK - 1
    s0 = pl.pallas_call(
        _gdn_meta_kernel,
        grid=(1,),
        in_specs=[pl.BlockSpec((CHUNK, 3 * W_B), lambda i: (last, 0)),
                  pl.BlockSpec((None, CHUNK, 128), lambda i: (0, last, 0)),
                  pl.BlockSpec((1, 128), lambda i: (0, 0)),
                  pl.BlockSpec((1, 128), lambda i: (0, 0))],
        out_specs=pl.BlockSpec((H_B, DK_B, DV_B), lambda i: (0, 0, 0)),
        out_shape=jax.ShapeDtypeStruct((H_B, DK_B, DV_B), F32),
        compiler_params=_cparams(("arbitrary",)),
        name="gdn_meta",
    )(y_m, sm, alog_row, dtb_row)
    return pl.pallas_call(
        _gdn_kernel,
        grid=(b, t // r),
        in_specs=[pl.BlockSpec((None, r, 3 * W_B), lambda bi, n: (bi, n, 0)),
                  pl.BlockSpec((None, r, W_B), lambda bi, n: (bi, n, G_ZB)),
                  pl.BlockSpec((None, r, 128), lambda bi, n: (bi, n + TILE // r, 0)),
                  pl.BlockSpec((H_B, DK_B, DV_B), lambda bi, n: (0, 0, 0)),
                  pl.BlockSpec((1, 128), lambda bi, n: (0, 0)),
                  pl.BlockSpec((1, 128), lambda bi, n: (0, 0)),
                  pl.BlockSpec((1, DV_B), lambda bi, n: (0, 0))],
        out_specs=pl.BlockSpec((None, r, W_B), lambda bi, n: (bi, n, 0)),
        out_shape=jax.ShapeDtypeStruct((b, t, W_B), BF16),
        scratch_shapes=[pltpu.VMEM((H_B, DK_B, DV_B), F32)],
        compiler_params=_cparams(("arbitrary", "arbitrary")),
        name="gdn",
    )(y_x, p_g, sm, s0, alog_row, dtb_row, gnw_row)


def _merge_kernel(ya_ref, za_ref, og_ref, ga_ref, gb_ref, x_ref, wpa, wpb, wout, o_ref):
    za = za_ref[...]
    ya = (ya_ref[...] * (za * jax.nn.sigmoid(za))).astype(BF16)
    ya = jnp.dot(ya, wpa[...], preferred_element_type=F32)
    yb = jnp.dot(og_ref[...], wpb[...], preferred_element_type=F32)
    m = jax.nn.sigmoid(ga_ref[...]) * ya + jax.nn.sigmoid(gb_ref[...]) * yb
    o_ref[...] = x_ref[...] + jnp.dot(m.astype(BF16), wout[...], preferred_element_type=F32)


def _merge(ya, za, og, p_g, x, wpa, wpb, wout):
    b, t, _ = x.shape
    const = lambda bi, i: (0, 0)
    return pl.pallas_call(
        _merge_kernel,
        grid=(b, t // TILE),
        in_specs=[pl.BlockSpec((None, TILE, W_A), lambda bi, i: (bi, i, 0)),
                  pl.BlockSpec((None, TILE, W_A), lambda bi, i: (bi, i + 1, 0)),
                  pl.BlockSpec((None, TILE, W_B), lambda bi, i: (bi, i, 0)),
                  pl.BlockSpec((None, TILE, D_MODEL), lambda bi, i: (bi, i, G_GA)),
                  pl.BlockSpec((None, TILE, D_MODEL), lambda bi, i: (bi, i, G_GB)),
                  pl.BlockSpec((None, TILE, D_MODEL), lambda bi, i: (bi, i, 0)),
                  pl.BlockSpec((W_A, D_MODEL), const),
                  pl.BlockSpec((W_B, D_MODEL), const),
                  pl.BlockSpec((D_MODEL, D_MODEL), const)],
        out_specs=pl.BlockSpec((None, TILE, D_MODEL), lambda bi, i: (bi, i, 0)),
        out_shape=jax.ShapeDtypeStruct((b, t, D_MODEL), F32),
        compiler_params=_cparams(("arbitrary", "arbitrary")),
        name="merge",
    )(ya, za, og, p_g, p_g, x, wpa, wpb, wout)


def _permute_w_in(w):
    sizes = (W_A, W_A, W_A, W_A, H_IDX * D_IDX, D_IDX, H_IDX,
             H_B * DK_B, H_B * DK_B, W_B, W_B, H_B, H_B, D_MODEL, D_MODEL)
    offs = [0]
    for s in sizes:
        offs.append(offs[-1] + s)
    attn_end, qkv_end, zb_end, small_end = offs[7], offs[10], offs[11], offs[13]

    def run(lo, hi):
        return w[:, lo:hi].astype(BF16)

    pad = jnp.zeros((w.shape[0], 128 - (D_IDX + H_IDX + 2 * H_B)), BF16)
    return (run(attn_end, qkv_end),
            jnp.concatenate([run(qkv_end, zb_end), run(small_end, offs[15])], axis=1),
            jnp.concatenate([run(0, attn_end), run(zb_end, small_end), pad], axis=1))


def _lane_row(vals, offset):
    return jnp.zeros((1, 128), F32).at[0, offset:offset + vals.shape[0]].set(vals.astype(F32))


def _layer(x, meta_tokens, rel_bias, norm_w, w_in, q_norm_w, k_norm_w, conv_w, a_log, dt_bias,
           gdn_norm_w, w_proj_a, w_proj_b, w_out):
    b, t, _ = x.shape
    w_qkv, w_gate, w_attn = _permute_w_in(w_in)
    nw_row = norm_w.reshape(1, D_MODEL)
    meta_pad = jnp.zeros((TILE, D_MODEL), F32).at[PAD_END:].set(meta_tokens)
    x2d = x.reshape(b * t, D_MODEL)

    y_m, tail_m, _ = _qkv_proj(meta_pad, nw_row, w_qkv, w_gate, conv_w, jnp.zeros((8, 3 * W_B), F32), 1)
    y_x, _, p_g = _qkv_proj(x2d, nw_row, w_qkv, w_gate, conv_w, tail_m, t // TILE)
    p_g = p_g.reshape(b, t, 3 * W_B)

    hid = lax.broadcasted_iota(I32, (W_A, W_A), 0) // DH_A
    bd = (hid == hid.T).astype(BF16)
    za, sm, k_n, v_t, k_i, q_tz, q_it, w_t = _attn_proj(
        x, meta_pad, nw_row, w_attn, jnp.tile(q_norm_w, H_A).reshape(1, W_A),
        jnp.tile(k_norm_w, H_A).reshape(1, W_A), bd)
    ya = _attention(q_tz, q_it, w_t, k_n, v_t, k_i, _bias_tile(rel_bias), t)

    og = _gdn(y_x.reshape(b, t, 3 * W_B), y_m, p_g, sm, _lane_row(a_log, S_AB), _lane_row(dt_bias, S_AB),
              gdn_norm_w.reshape(1, DV_B))

    return _merge(ya, za, og, p_g, x, w_proj_a.astype(BF16), w_proj_b.astype(BF16), w_out.astype(BF16))


def kernel(x, meta_tokens, rel_bias, norm_w, w_in, q_norm_w, k_norm_w, conv_w, a_log, dt_bias,
           gdn_norm_w, w_proj_a, w_proj_b, w_out):
    depth = norm_w.shape[0]
    assert depth == 1, "meta rows are dropped after the layer; deeper stacks need them carried"
    return _layer(x, meta_tokens, rel_bias, norm_w[0], w_in[0], q_norm_w[0], k_norm_w[0], conv_w[0],
                  a_log[0], dt_bias[0], gdn_norm_w[0], w_proj_a[0], w_proj_b[0], w_out[0])
```

```python
import functools
import math

import jax
import jax.numpy as jnp
from jax import lax
from jax.experimental import pallas as pl
from jax.experimental.pallas import tpu as pltpu

F32 = jnp.float32
BF16 = jnp.bfloat16
I32 = jnp.int32
I16 = jnp.int16

D_MODEL = 1024
N_META = 16
H_A = 8
DH_A = 64
W_A = H_A * DH_A
H_IDX = 8
D_IDX = 64
TOPK = 256
N_BUCKETS = 32
MAX_DISTANCE = 128
H_B = 8
DK_B = 128
DV_B = 128
W_B = H_B * DV_B
CONV_K = 4
CHUNK = 64
EPS = 1e-6

G_ZB, G_GA, G_GB = 0, 1, 2
A_Q, A_K, A_V, A_Z, A_QI, A_S = 0, 512, 1024, 1536, 2048, 2560
N_ATTN = A_S + 128
S_KI, S_WI, S_BB, S_AB = 0, 64, 72, 80

TILE = 512
QB = 256
KC = 256
NEAR = QB + 128
FAR0 = TILE - 128
PAD_END = TILE - N_META
VROWS = DH_A + 16
LOG2E = math.log2(math.e)
INT_MIN = -2 ** 31
NEG = -1e30
VMEM_LIMIT = 58 * 1024 * 1024
HI = lax.Precision.HIGHEST
GDN_CHUNKS_PER_STEP = 4


def _cparams(sem):
    return pltpu.CompilerParams(dimension_semantics=sem, vmem_limit_bytes=VMEM_LIMIT)


def _rms_bf16(x, nw):
    ms = jnp.mean(x * x, axis=-1, keepdims=True)
    return (x * lax.rsqrt(ms + EPS) * nw).astype(BF16)


def _qkv_proj_kernel(x_ref, nw_ref, w_ref, wg_ref, convw, tail_in, y_o, tail_o, g_o, xb0, xb1, xb2,
                     *, tiles_per_seq):
    xbufs = (xb0, xb1, xb2)

    def cols(g):
        return slice(g * W_B, (g + 1) * W_B)

    @pl.when(pl.program_id(0) % tiles_per_seq == 0)
    def _():
        for g in range(3):
            xbufs[g][0:8, :] = tail_in[:, cols(g)]

    h = _rms_bf16(x_ref[...], nw_ref[...])

    unit = 2 * DK_B
    n_units = 3 * W_B // unit

    def where(u):
        g, lo = divmod(u * unit, W_B)
        return g, slice(lo, lo + unit), slice(u * unit, (u + 1) * unit)

    def project(u):
        g, cg, ca = where(u)
        xbufs[g][8:8 + TILE, cg] = jnp.dot(h, w_ref[:, ca], preferred_element_type=F32)

    def gate(u):
        _, _, ca = where(u)
        g_o[:, ca] = jnp.dot(h, wg_ref[:, ca], preferred_element_type=F32)

    def epilogue(u):
        g, cg, ca = where(u)
        xbuf = xbufs[g]
        y = convw[CONV_K - 1:CONV_K, ca] * xbuf[8:8 + TILE, cg]
        for i in range(CONV_K - 1):
            off = 8 - (CONV_K - 1) + i
            y = y + convw[i:i + 1, ca] * xbuf[off:off + TILE, cg]
        y = y * jax.nn.sigmoid(y)
        tail = xbuf[TILE:TILE + 8, cg]
        tail_o[:, ca] = tail
        xbuf[0:8, cg] = tail
        if g == 2:
            y_o[:, ca] = y
        else:
            scale = DK_B ** -0.5 if g == 0 else 1.0
            for hh in range(unit // DK_B):
                yh = y[:, hh * DK_B:(hh + 1) * DK_B]
                inv = lax.rsqrt(jnp.sum(yh * yh, axis=-1, keepdims=True) + EPS) * scale
                y_o[:, u * unit + hh * DK_B:u * unit + (hh + 1) * DK_B] = yh * inv

    project(0)
    for u in range(n_units):
        gate(u)
        if u + 1 < n_units:
            project(u + 1)
        epilogue(u)


def _qkv_proj(x2d, norm_w_row, w_bf16, wg_bf16, convw, tail_in, tiles_per_seq):
    m = x2d.shape[0]
    n = 3 * W_B
    once = pl.Buffered(1)
    return pl.pallas_call(
        functools.partial(_qkv_proj_kernel, tiles_per_seq=tiles_per_seq),
        grid=(m // TILE,),
        in_specs=[pl.BlockSpec((TILE, D_MODEL), lambda i: (i, 0)),
                  pl.BlockSpec((1, D_MODEL), lambda i: (0, 0)),
                  pl.BlockSpec((D_MODEL, n), lambda i: (0, 0), pipeline_mode=once),
                  pl.BlockSpec((D_MODEL, n), lambda i: (0, 0), pipeline_mode=once),
                  pl.BlockSpec((CONV_K, n), lambda i: (0, 0)),
                  pl.BlockSpec((8, n), lambda i: (0, 0))],
        out_specs=[pl.BlockSpec((TILE, n), lambda i: (i, 0)),
                   pl.BlockSpec((8, n), lambda i: (i, 0)),
                   pl.BlockSpec((TILE, n), lambda i: (i, 0))],
        out_shape=[jax.ShapeDtypeStruct((m, n), F32),
                   jax.ShapeDtypeStruct((m // TILE * 8, n), F32),
                   jax.ShapeDtypeStruct((m, n), F32)],
        scratch_shapes=[pltpu.VMEM((8 + TILE, W_B), F32)] * 3,
        compiler_params=_cparams(("arbitrary",)),
        name="qkv_proj",
    )(x2d, norm_w_row, w_bf16, wg_bf16, convw, tail_in)


def _head_rms(x, bd, w_row):
    ss = jnp.dot((x * x).astype(BF16), bd, preferred_element_type=F32)
    return x * lax.rsqrt(ss * (1.0 / DH_A) + EPS) * w_row


def _attn_proj_kernel(x_ref, xm_ref, nw_ref, w_ref, qnw, knw, bd_ref,
                      za_o, sm_o, k_o, vt_o, ki_o, qtz_o, qit_o, wt_o):
    x = jnp.where(pl.program_id(1) == 0, xm_ref[...], x_ref[...])
    h = _rms_bf16(x, nw_ref[...])
    bd = bd_ref[...]

    def proj(c0, width):
        return jnp.dot(h, w_ref[:, c0:c0 + width], preferred_element_type=F32)

    sm = proj(A_S, 128)
    qi = proj(A_QI, W_A)
    sm_o[...] = sm
    ki_o[...] = sm[:, S_KI:S_KI + D_IDX].astype(BF16)
    wt_o[...] = sm.T[S_WI:S_WI + H_IDX, :] * ((H_IDX ** -0.5) * (D_IDX ** -0.5))

    q = proj(A_Q, W_A)
    for j in range(W_A // 128):
        qit_o[j * 128:(j + 1) * 128, :] = qi[:, j * 128:(j + 1) * 128].T.astype(BF16)

    k = proj(A_K, W_A)
    qn = _head_rms(q, bd, qnw[...]) * (DH_A ** -0.5 * LOG2E)
    zeros = jnp.zeros((DH_A, TILE), BF16)
    for j in range(W_A // 128):
        t = qn[:, j * 128:(j + 1) * 128].T.astype(BF16)
        base = 2 * j * 128
        qtz_o[base:base + 64, :] = t[0:64]
        qtz_o[base + 64:base + 128, :] = zeros
        qtz_o[base + 128:base + 192, :] = zeros
        qtz_o[base + 192:base + 256, :] = t[64:128]

    v = proj(A_V, W_A)
    k_o[...] = _head_rms(k, bd, knw[...]).astype(BF16)

    za = proj(A_Z, W_A)
    ones_rows = jnp.where(lax.broadcasted_iota(I32, (VROWS - DH_A, TILE), 0) == 0, 1.0, 0.0).astype(BF16)
    for j in range(W_A // 128):
        t = v[:, j * 128:(j + 1) * 128].T.astype(BF16)
        for r in range(2):
            base = (2 * j + r) * VROWS
            vt_o[base:base + DH_A, :] = t[r * DH_A:(r + 1) * DH_A]
            vt_o[base + DH_A:base + VROWS, :] = ones_rows
    za_o[...] = za


def _attn_proj(x, meta_pad, norm_w_row, w_bf16, qnw_row, knw_row, bd):
    b, t, _ = x.shape
    nt = t // TILE + 1
    tk = nt * TILE

    def rows(width, dtype):
        return (pl.BlockSpec((None, TILE, width), lambda bi, i: (bi, i, 0)),
                jax.ShapeDtypeStruct((b, tk, width), dtype))

    def cols(height, dtype):
        return (pl.BlockSpec((None, height, TILE), lambda bi, i: (bi, 0, i)),
                jax.ShapeDtypeStruct((b, height, tk), dtype))

    outs = [rows(W_A, F32), rows(128, F32), rows(W_A, BF16), cols(H_A * VROWS, BF16), rows(D_IDX, BF16),
            cols(2 * W_A, BF16), cols(W_A, BF16), cols(H_IDX, F32)]
    const = lambda bi, i: (0, 0)
    return pl.pallas_call(
        _attn_proj_kernel,
        grid=(b, nt),
        in_specs=[pl.BlockSpec((None, TILE, D_MODEL), lambda bi, i: (bi, jnp.maximum(i - 1, 0), 0)),
                  pl.BlockSpec((TILE, D_MODEL), const),
                  pl.BlockSpec((1, D_MODEL), const),
                  pl.BlockSpec((D_MODEL, N_ATTN), const),
                  pl.BlockSpec((1, W_A), const),
                  pl.BlockSpec((1, W_A), const),
                  pl.BlockSpec((W_A, W_A), const)],
        out_specs=[o[0] for o in outs],
        out_shape=[o[1] for o in outs],
        compiler_params=_cparams(("arbitrary", "arbitrary")),
        name="attn_proj",
    )(x, meta_pad, norm_w_row, w_bf16, qnw_row, knw_row, bd)


def _bias_kernel(rb_ref, o_ref):
    a = lax.broadcasted_iota(I32, (NEAR, QB), 1)
    bk = lax.broadcasted_iota(I32, (NEAR, QB), 0)
    d = jnp.maximum(a - bk + (NEAR - QB), 0)
    max_exact = N_BUCKETS // 2
    ratio = jnp.maximum(d, max_exact).astype(F32) / max_exact
    large = max_exact + (jnp.log(ratio) / math.log(MAX_DISTANCE / max_exact)
                         * (N_BUCKETS - max_exact)).astype(I32)
    large = jnp.minimum(large, N_BUCKETS - 1)
    bucket = jnp.where(d < max_exact, d, large)
    for h in range(H_A):
        acc = jnp.zeros((NEAR, QB), F32)
        for bb in range(N_BUCKETS):
            acc = jnp.where(bucket == bb, rb_ref[bb, h], acc)
        o_ref[h] = (acc - rb_ref[N_BUCKETS - 1, h]) * LOG2E


def _bias_tile(rel_bias):
    return pl.pallas_call(
        _bias_kernel,
        in_specs=[pl.BlockSpec(memory_space=pltpu.SMEM)],
        out_specs=pl.BlockSpec((H_A, NEAR, QB), lambda: (0, 0, 0)),
        out_shape=jax.ShapeDtypeStruct((H_A, NEAR, QB), F32),
        name="bias_tile",
    )(rel_bias)


def _attn_kernel(qtz_ref, qit_ref, wt_ref, qit_nx, wt_nx, k_ref, vt_ref, ki_ref, bias_ref, o_ref,
                 keys2_ref, hi_ref, lo_ref, m_ref, acc_ref, j_ref):
    iq = pl.program_id(1)
    nq = pl.num_programs(1)
    keys_ref = keys2_ref.at[iq % 2]
    keys_nx = keys2_ref.at[(iq + 1) % 2]
    near0 = pl.multiple_of(FAR0 + iq * QB, 128)
    q_pos = TILE + iq * QB + lax.broadcasted_iota(I32, (1, QB), 1)

    def score_chunk(kref, qit, wt, start, size, mask_causal):
        kic = ki_ref[pl.ds(start, size), :]
        acc = jnp.zeros((size, QB), F32)
        for h in range(H_IDX):
            r = jnp.dot(kic, qit[h * D_IDX:(h + 1) * D_IDX, :], preferred_element_type=F32)
            acc = acc + jnp.maximum(r, 0.0) * wt[h:h + 1, :]
        acc = acc + 0.0
        bits = pltpu.bitcast(acc, I32)
        key = jnp.where(bits < 0, bits ^ jnp.int32(0x7FFFFFFF), bits)
        if mask_causal:
            pos = start + lax.broadcasted_iota(I32, (size, QB), 0)
            key = jnp.where(pos <= q_pos, key, INT_MIN)
        kref[pl.ds(start, size), :] = key
        hi_ref[pl.ds(start, size), :] = lax.shift_right_arithmetic(key, 16).astype(I16)
        lo_ref[pl.ds(start, size), :] = ((key & 0xFFFF) - 2 ** 15).astype(I16)

    n_big = iq // 2
    n_prev = jnp.maximum(iq - 1, 0) // 2
    odd0 = pl.multiple_of(FAR0 + n_big * 2 * KC, 128)

    def big_start(c):
        return pl.multiple_of(FAR0 + c * 2 * KC, 128)

    def far_scores(c, carry):
        score_chunk(keys_ref, qit_ref, wt_ref, big_start(c), 2 * KC, False)
        return carry

    lax.fori_loop(n_prev, n_big, far_scores, 0)

    @pl.when(iq % 2 == 1)
    def _():
        score_chunk(keys_ref, qit_ref, wt_ref, odd0, KC, False)

    score_chunk(keys_ref, qit_ref, wt_ref, near0, NEAR, True)
    keys_ref[FAR0:PAD_END, :] = jnp.full((PAD_END - FAR0, QB), INT_MIN, I32)
    hi_ref[FAR0:PAD_END, :] = jnp.full((PAD_END - FAR0, QB), -2 ** 15, I16)
    lo_ref[FAR0:PAD_END, :] = jnp.full((PAD_END - FAR0, QB), -2 ** 15, I16)

    n256 = iq + NEAR // 256
    tail = NEAR % 256

    def blk_start(c):
        return pl.multiple_of(FAR0 + c * 256, 128)

    def count(ref, pred):
        def part(start, size):
            o = pred(ref[pl.ds(start, size), :], start).reshape(size // 32, 32, QB)
            parts = [o[i] for i in range(size // 32)]
            while len(parts) > 1:
                parts = [parts[i] + parts[i + 1] for i in range(0, len(parts), 2)]
            return parts[0]

        def body(c, acc):
            return acc + part(blk_start(c), 256)
        acc = lax.fori_loop(0, n256, body, jnp.zeros((32, QB), ref.dtype))
        if tail:
            acc = acc + part(blk_start(n256), tail)
        return jnp.sum(acc.astype(I32), axis=0, keepdims=True)

    one16 = jnp.int16(1)
    zero16 = jnp.int16(0)

    def half_step(ref):
        def step(it, t):
            cand = t + lax.shift_left(jnp.int32(1), 15 - it)
            c16 = cand.astype(I16)
            cnt = count(ref, lambda blk, start: jnp.where(blk >= c16, one16, zero16))
            return jnp.where(cnt >= TOPK, cand, t)
        return step

    half_min = jnp.full((1, QB), -2 ** 15, I32)
    tau_hi = lax.fori_loop(0, 16, half_step(hi_ref), half_min)
    th16 = tau_hi.astype(I16)

    def fold_low(start, size):
        hi = hi_ref[pl.ds(start, size), :]
        lo = lo_ref[pl.ds(start, size), :]
        lo_ref[pl.ds(start, size), :] = jnp.where(hi > th16, jnp.int16(2 ** 15 - 1),
                                                  jnp.where(hi < th16, jnp.int16(-2 ** 15), lo))

    def fold_body(c, carry):
        fold_low(blk_start(c), 256)
        return carry

    lax.fori_loop(0, n256, fold_body, 0)
    if tail:
        fold_low(blk_start(n256), tail)
    tau_lo = lax.fori_loop(0, 16, half_step(lo_ref), half_min)
    tau = lax.shift_left(tau_hi, 16) + (tau_lo + 2 ** 15)

    def count32(pred):
        return count(keys_ref, pred)

    cnt_ge = count32(lambda blk, start: jnp.where(blk >= tau, 1, 0))
    excess = jnp.where(tau > INT_MIN, jnp.where(cnt_ge > TOPK, 1, 0), 0).astype(I32)
    j_ref[...] = jnp.where(tau > INT_MIN, jnp.int32(2 ** 30), jnp.int32(-1))

    @pl.when(jnp.max(excess) > 0)
    def _():
        need = TOPK - count32(lambda blk, start: jnp.where(blk > tau, 1, 0))

        def pos_step(it, lo):
            cand = lo + lax.shift_left(jnp.int32(1), 12 - it)

            def pred(blk, start):
                pos = start + lax.broadcasted_iota(I32, blk.shape, 0)
                return jnp.where(pos < cand, jnp.where(blk == tau, 1, 0), 0)
            return jnp.where(count32(pred) < need, cand, lo)

        lo = lax.fori_loop(0, 13, pos_step, jnp.zeros((1, QB), I32))
        j_ref[...] = jnp.where(excess > 0, lo, j_ref[...])

    j_last = j_ref[...]

    m_ref[...] = jnp.full(m_ref.shape, NEG, F32)
    acc_ref[...] = jnp.zeros(acc_ref.shape, F32)

    def attend(start, size, near, mid=None):
        key = keys_ref[pl.ds(start, size), :]
        pos = start + lax.broadcasted_iota(I32, (size, QB), 0)
        tie = jnp.where(pos <= j_last, 0.0, NEG)
        madd = jnp.where(key > tau, 0.0, jnp.where(key == tau, tie, NEG))
        hs = range(H_A)
        s = [jnp.dot(k_ref[pl.ds(start, size), (h // 2) * 128:(h // 2 + 1) * 128],
                     qtz_ref[h * 128:(h + 1) * 128, :], preferred_element_type=F32) + madd for h in hs]
        if mid is not None:
            mid()
        if near:
            s = [s[h] + bias_ref[h] for h in hs]
        m_all = m_ref[...]
        m_new = [jnp.maximum(m_all[h:h + 1, :], jnp.max(s[h], axis=0, keepdims=True)) for h in hs]
        m_new_all = jnp.concatenate(m_new, axis=0)
        alpha_all = jnp.exp2(m_all - m_new_all)
        m_ref[...] = m_new_all
        p = [jnp.exp2(s[h] - m_new[h]).astype(BF16) for h in hs]
        pv = [jnp.dot(vt_ref[h * VROWS:(h + 1) * VROWS, pl.ds(start, size)], p[h],
                      preferred_element_type=F32) for h in hs]
        for h in hs:
            acc_ref[h * VROWS:(h + 1) * VROWS, :] = (alpha_all[h:h + 1, :] * acc_ref[h * VROWS:(h + 1) * VROWS, :]
                                                     + pv[h])

    def fused_attend(c, carry):
        start = big_start(c)
        attend(start, 2 * KC, False,
               mid=lambda: score_chunk(keys_nx, qit_nx, wt_nx, start, 2 * KC, False))
        return carry

    def far_attend(c, carry):
        attend(big_start(c), 2 * KC, False)
        return carry

    n_fused = jnp.where(iq + 1 < nq, n_big, 0)
    lax.fori_loop(0, n_fused, fused_attend, 0)
    lax.fori_loop(n_fused, n_big, far_attend, 0)

    @pl.when(iq % 2 == 1)
    def _():
        attend(odd0, KC, False)

    attend(near0, NEAR, True)

    for j in range(W_A // 128):
        parts = []
        for h in (2 * j, 2 * j + 1):
            inv = 1.0 / acc_ref[h * VROWS + DH_A:h * VROWS + DH_A + 1, :]
            parts.append(acc_ref[h * VROWS:h * VROWS + DH_A, :] * inv)
        o_ref[:, j * 128:(j + 1) * 128] = jnp.concatenate(parts, axis=0).T


def _attention(qtz, qit, wt, k, vt, ki, bias, t):
    b, tk, _ = k.shape
    qoff = TILE // QB
    nq = t // QB

    def nxt(bi, i):
        return (bi, 0, jnp.minimum(i + 1, nq - 1) + qoff)

    return pl.pallas_call(
        _attn_kernel,
        grid=(b, nq),
        in_specs=[pl.BlockSpec((None, 2 * W_A, QB), lambda bi, i: (bi, 0, i + qoff)),
                  pl.BlockSpec((None, W_A, QB), lambda bi, i: (bi, 0, i + qoff)),
                  pl.BlockSpec((None, H_IDX, QB), lambda bi, i: (bi, 0, i + qoff)),
                  pl.BlockSpec((None, W_A, QB), nxt),
                  pl.BlockSpec((None, H_IDX, QB), nxt),
                  pl.BlockSpec((None, tk, W_A), lambda bi, i: (bi, 0, 0)),
                  pl.BlockSpec((None, H_A * VROWS, tk), lambda bi, i: (bi, 0, 0)),
                  pl.BlockSpec((None, tk, D_IDX), lambda bi, i: (bi, 0, 0)),
                  pl.BlockSpec((H_A, NEAR, QB), lambda bi, i: (0, 0, 0))],
        out_specs=pl.BlockSpec((None, QB, W_A), lambda bi, i: (bi, i, 0)),
        out_shape=jax.ShapeDtypeStruct((b, t, W_A), F32),
        scratch_shapes=[pltpu.VMEM((2, tk, QB), I32),
                        pltpu.VMEM((tk, QB), I16),
                        pltpu.VMEM((tk, QB), I16),
                        pltpu.VMEM((H_A, QB), F32),
                        pltpu.VMEM((H_A * VROWS, QB), F32),
                        pltpu.VMEM((1, QB), I32)],
        compiler_params=_cparams(("arbitrary", "arbitrary")),
        name="attn",
    )(qtz, qit, wt, qit, wt, k, vt, ki, bias)


def _dot_hi(a, b):
    return jnp.dot(a, b, precision=HI, preferred_element_type=F32)


def _mm(a, b):
    return jnp.dot(a.astype(BF16), b.astype(BF16), preferred_element_type=F32)


def _gates(sm, alog, dtb):
    z = sm + dtb
    softplus = jnp.maximum(z, 0.0) + jnp.log(1.0 + jnp.exp(-jnp.abs(z)))
    return -jnp.exp(alog) * softplus, jax.nn.sigmoid(sm)


def _gdn_local(y, g, beta, n_chunks):
    ri = lax.broadcasted_iota(I32, (CHUNK, CHUNK), 0)
    ci = lax.broadcasted_iota(I32, (CHUNK, CHUNK), 1)
    incl = ri >= ci
    strict = ri > ci
    tri = jnp.where(incl, 1.0, 0.0).astype(F32)
    tri_t = jnp.where(ri <= ci, 1.0, 0.0).astype(F32)
    eye = jnp.where(ri == ci, 1.0, 0.0).astype(F32)
    pairs = [(c, h) for c in range(n_chunks) for h in range(H_B)]
    idx = range(len(pairs))
    gc_col, gc_row = [], []
    for c in range(n_chunks):
        g_c = g[c * CHUNK:(c + 1) * CHUNK]
        gc_col.append(_dot_hi(tri, g_c))
        gc_row.append(_dot_hi(g_c.T, tri_t))
    q, kt, vb, qe, decay, kdt, glast = [], [], [], [], [], [], []
    for c, h in pairs:
        yc = y[c * CHUNK:(c + 1) * CHUNK]
        qh = yc[:, h * DK_B:(h + 1) * DK_B]
        kh = yc[:, W_B + h * DK_B:W_B + (h + 1) * DK_B]
        vh = yc[:, 2 * W_B + h * DV_B:2 * W_B + (h + 1) * DV_B]
        gcol = gc_col[c][:, S_AB + h:S_AB + h + 1]
        grow = gc_row[c][S_AB + h:S_AB + h + 1, :]
        bcol = beta[c * CHUNK:(c + 1) * CHUNK, S_BB + h:S_BB + h + 1]
        g_last = gcol[CHUNK - 1:CHUNK, :]
        egc = jnp.exp(gcol)
        kth = kh.T
        kb = kh * bcol
        q.append(jnp.concatenate([kb, qh], axis=0))
        kt.append(kth)
        vb.append(jnp.concatenate([vh * bcol, kb * egc], axis=1))
        qe.append(qh * egc)
        decay.append(jnp.exp(jnp.where(incl, gcol - grow, -jnp.inf)))
        kdt.append(kth * jnp.exp(g_last - grow))
        glast.append(jnp.exp(g_last))
    qk = [_mm(q[i], kt[i]) for i in idx]
    a = [jnp.where(strict, qk[i][0:CHUNK] * decay[i], 0.0) for i in idx]
    att = [qk[i][CHUNK:2 * CHUNK] * decay[i] for i in idx]
    tm = [eye - a[i] for i in idx]
    pw = [_mm(a[i], a[i]) for i in idx]
    for _ in range(4):
        r = [_mm(jnp.concatenate([tm[i], pw[i]], axis=0), pw[i]) for i in idx]
        tm = [tm[i] + r[i][0:CHUNK] for i in idx]
        pw = [r[i][CHUNK:2 * CHUNK] for i in idx]
    tm = [tm[i] + _mm(tm[i], pw[i]) for i in idx]
    uw = [_mm(tm[i], vb[i]) for i in idx]
    return uw, qe, att, kdt, glast


def _gdn_scan(local, c, s_list):
    uw, qe, att, kdt, glast = local
    base = c * H_B
    hs = range(H_B)
    ws = [_mm(jnp.concatenate([uw[base + h][:, DV_B:], qe[base + h]], axis=0), s_list[h]) for h in hs]
    v_new = [uw[base + h][:, :DV_B] - ws[h][0:CHUNK] for h in hs]
    ov = [_mm(jnp.concatenate([att[base + h], kdt[base + h]], axis=0), v_new[h]) for h in hs]
    outs = [ws[h][CHUNK:2 * CHUNK] + ov[h][0:CHUNK] for h in hs]
    s_out = [s_list[h] * glast[base + h] + ov[h][CHUNK:CHUNK + DK_B] for h in hs]
    return outs, s_out


def _gdn_meta_kernel(qkv_m, sm_m, alog, dtb, s_o):
    y = qkv_m[...]
    g, beta = _gates(sm_m[...], alog[...], dtb[...])
    live = lax.broadcasted_iota(I32, (CHUNK, 128), 0) >= CHUNK - N_META
    g = jnp.where(live, g, 0.0)
    beta = jnp.where(live, beta, 0.0)
    _, s_new = _gdn_scan(_gdn_local(y, g, beta, 1), 0, [jnp.zeros((DK_B, DV_B), F32)] * H_B)
    for h in range(H_B):
        s_o[h] = s_new[h]


def _gdn_kernel(qkv_x, zb_x, sm_x, s0, alog, dtb, gnw, o_ref, s_ref):
    @pl.when(pl.program_id(1) == 0)
    def _():
        s_ref[...] = s0[...]

    y = qkv_x[...]
    r = y.shape[0]
    g, beta = _gates(sm_x[...], alog[...], dtb[...])
    local = _gdn_local(y, g, beta, r // CHUNK)
    s_list = [s_ref[h] for h in range(H_B)]
    for c in range(r // CHUNK):
        rows = slice(c * CHUNK, (c + 1) * CHUNK)
        outs, s_list = _gdn_scan(local, c, s_list)
        for h in range(H_B):
            o = outs[h]
            on = o * lax.rsqrt(jnp.mean(o * o, axis=-1, keepdims=True) + EPS) * gnw[...]
            zb = zb_x[rows, h * DV_B:(h + 1) * DV_B]
            o_ref[rows, h * DV_B:(h + 1) * DV_B] = (on * (zb * jax.nn.sigmoid(zb))).astype(BF16)
    for h in range(H_B):
        s_ref[h] = s_list[h]


def _gdn(y_x, y_m, p_g, sm, alog_row, dtb_row, gnw_row):
    b, t, _ = y_x.shape
    r = CHUNK * GDN_CHUNKS_PER_STEP
    last = TILE // CHUNK - 1
    s0 = pl.pallas_call(
        _gdn_meta_kernel,
        grid=(1,),
        in_specs=[pl.BlockSpec((CHUNK, 3 * W_B), lambda i: (last, 0)),
                  pl.BlockSpec((None, CHUNK, 128), lambda i: (0, last, 0)),
                  pl.BlockSpec((1, 128), lambda i: (0, 0)),
                  pl.BlockSpec((1, 128), lambda i: (0, 0))],
        out_specs=pl.BlockSpec((H_B, DK_B, DV_B), lambda i: (0, 0, 0)),
        out_shape=jax.ShapeDtypeStruct((H_B, DK_B, DV_B), F32),
        compiler_params=_cparams(("arbitrary",)),
        name="gdn_meta",
    )(y_m, sm, alog_row, dtb_row)
    return pl.pallas_call(
        _gdn_kernel,
        grid=(b, t // r),
        in_specs=[pl.BlockSpec((None, r, 3 * W_B), lambda bi, n: (bi, n, 0)),
                  pl.BlockSpec((None, r, W_B), lambda bi, n: (bi, n, G_ZB)),
                  pl.BlockSpec((None, r, 128), lambda bi, n: (bi, n + TILE // r, 0)),
                  pl.BlockSpec((H_B, DK_B, DV_B), lambda bi, n: (0, 0, 0)),
                  pl.BlockSpec((1, 128), lambda bi, n: (0, 0)),
                  pl.BlockSpec((1, 128), lambda bi, n: (0, 0)),
                  pl.BlockSpec((1, DV_B), lambda bi, n: (0, 0))],
        out_specs=pl.BlockSpec((None, r, W_B), lambda bi, n: (bi, n, 0)),
        out_shape=jax.ShapeDtypeStruct((b, t, W_B), BF16),
        scratch_shapes=[pltpu.VMEM((H_B, DK_B, DV_B), F32)],
        compiler_params=_cparams(("arbitrary", "arbitrary")),
        name="gdn",
    )(y_x, p_g, sm, s0, alog_row, dtb_row, gnw_row)


def _merge_kernel(ya_ref, za_ref, og_ref, ga_ref, gb_ref, x_ref, wpa, wpb, wout, o_ref):
    za = za_ref[...]
    ya = (ya_ref[...] * (za * jax.nn.sigmoid(za))).astype(BF16)
    ya = jnp.dot(ya, wpa[...], preferred_element_type=F32)
    yb = jnp.dot(og_ref[...], wpb[...], preferred_element_type=F32)
    m = jax.nn.sigmoid(ga_ref[...]) * ya + jax.nn.sigmoid(gb_ref[...]) * yb
    o_ref[...] = x_ref[...] + jnp.dot(m.astype(BF16), wout[...], preferred_element_type=F32)


def _merge(ya, za, og, p_g, x, wpa, wpb, wout):
    b, t, _ = x.shape
    const = lambda bi, i: (0, 0)
    return pl.pallas_call(
        _merge_kernel,
        grid=(b, t // TILE),
        in_specs=[pl.BlockSpec((None, TILE, W_A), lambda bi, i: (bi, i, 0)),
                  pl.BlockSpec((None, TILE, W_A), lambda bi, i: (bi, i + 1, 0)),
                  pl.BlockSpec((None, TILE, W_B), lambda bi, i: (bi, i, 0)),
                  pl.BlockSpec((None, TILE, D_MODEL), lambda bi, i: (bi, i, G_GA)),
                  pl.BlockSpec((None, TILE, D_MODEL), lambda bi, i: (bi, i, G_GB)),
                  pl.BlockSpec((None, TILE, D_MODEL), lambda bi, i: (bi, i, 0)),
                  pl.BlockSpec((W_A, D_MODEL), const),
                  pl.BlockSpec((W_B, D_MODEL), const),
                  pl.BlockSpec((D_MODEL, D_MODEL), const)],
        out_specs=pl.BlockSpec((None, TILE, D_MODEL), lambda bi, i: (bi, i, 0)),
        out_shape=jax.ShapeDtypeStruct((b, t, D_MODEL), F32),
        compiler_params=_cparams(("arbitrary", "arbitrary")),
        name="merge",
    )(ya, za, og, p_g, p_g, x, wpa, wpb, wout)


def _permute_w_in(w):
    sizes = (W_A, W_A, W_A, W_A, H_IDX * D_IDX, D_IDX, H_IDX,
             H_B * DK_B, H_B * DK_B, W_B, W_B, H_B, H_B, D_MODEL, D_MODEL)
    offs = [0]
    for s in sizes:
        offs.append(offs[-1] + s)
    attn_end, qkv_end, zb_end, small_end = offs[7], offs[10], offs[11], offs[13]

    def run(lo, hi):
        return w[:, lo:hi].astype(BF16)

    pad = jnp.zeros((w.shape[0], 128 - (D_IDX + H_IDX + 2 * H_B)), BF16)
    return (run(attn_end, qkv_end),
            jnp.concatenate([run(qkv_end, zb_end), run(small_end, offs[15])], axis=1),
            jnp.concatenate([run(0, attn_end), run(zb_end, small_end), pad], axis=1))


def _lane_row(vals, offset):
    return jnp.zeros((1, 128), F32).at[0, offset:offset + vals.shape[0]].set(vals.astype(F32))


def _layer(x, meta_tokens, rel_bias, norm_w, w_in, q_norm_w, k_norm_w, conv_w, a_log, dt_bias,
           gdn_norm_w, w_proj_a, w_proj_b, w_out):
    b, t, _ = x.shape
    w_qkv, w_gate, w_attn = _permute_w_in(w_in)
    nw_row = norm_w.reshape(1, D_MODEL)
    meta_pad = jnp.zeros((TILE, D_MODEL), F32).at[PAD_END:].set(meta_tokens)
    x2d = x.reshape(b * t, D_MODEL)

    y_m, tail_m, _ = _qkv_proj(meta_pad, nw_row, w_qkv, w_gate, conv_w, jnp.zeros((8, 3 * W_B), F32), 1)
    y_x, _, p_g = _qkv_proj(x2d, nw_row, w_qkv, w_gate, conv_w, tail_m, t // TILE)
    p_g = p_g.reshape(b, t, 3 * W_B)

    hid = lax.broadcasted_iota(I32, (W_A, W_A), 0) // DH_A
    bd = (hid == hid.T).astype(BF16)
    za, sm, k_n, v_t, k_i, q_tz, q_it, w_t = _attn_proj(
        x, meta_pad, nw_row, w_attn, jnp.tile(q_norm_w, H_A).reshape(1, W_A),
        jnp.tile(k_norm_w, H_A).reshape(1, W_A), bd)
    ya = _attention(q_tz, q_it, w_t, k_n, v_t, k_i, _bias_tile(rel_bias), t)

    og = _gdn(y_x.reshape(b, t, 3 * W_B), y_m, p_g, sm, _lane_row(a_log, S_AB), _lane_row(dt_bias, S_AB),
              gdn_norm_w.reshape(1, DV_B))

    return _merge(ya, za, og, p_g, x, w_proj_a.astype(BF16), w_proj_b.astype(BF16), w_out.astype(BF16))


def kernel(x, meta_tokens, rel_bias, norm_w, w_in, q_norm_w, k_norm_w, conv_w, a_log, dt_bias,
           gdn_norm_w, w_proj_a, w_proj_b, w_out):
    depth = norm_w.shape[0]
    assert depth == 1, "meta rows are dropped after the layer; deeper stacks need them carried"
    return _layer(x, meta_tokens, rel_bias, norm_w[0], w_in[0], q_norm_w[0], k_norm_w[0], conv_w[0],
                  a_log[0], dt_bias[0], gdn_norm_w[0], w_proj_a[0], w_proj_b[0], w_out[0])
```

```python
import functools
import math

import jax
import jax.numpy as jnp
from jax import lax
from jax.experimental import pallas as pl
from jax.experimental.pallas import tpu as pltpu

F32 = jnp.float32
BF16 = jnp.bfloat16
I32 = jnp.int32
I16 = jnp.int16

D_MODEL = 1024
N_META = 16
H_A = 8
DH_A = 64
W_A = H_A * DH_A
H_IDX = 8
D_IDX = 64
TOPK = 256
N_BUCKETS = 32
MAX_DISTANCE = 128
H_B = 8
DK_B = 128
DV_B = 128
W_B = H_B * DV_B
CONV_K = 4
CHUNK = 64
EPS = 1e-6

G_ZB, G_GA, G_GB = 0, 1, 2
A_Q, A_K, A_V, A_Z, A_QI, A_S = 0, 512, 1024, 1536, 2048, 2560
N_ATTN = A_S + 128
S_KI, S_WI, S_BB, S_AB = 0, 64, 72, 80

TILE = 512
QB = 256
KC = 256
NEAR = QB + 128
FAR0 = TILE - 128
PAD_END = TILE - N_META
VROWS = DH_A + 16
LOG2E = math.log2(math.e)
INT_MIN = -2 ** 31
NEG = -1e30
VMEM_LIMIT = 58 * 1024 * 1024
HI = lax.Precision.HIGHEST
GDN_CHUNKS_PER_STEP = 4


def _cparams(sem):
    return pltpu.CompilerParams(dimension_semantics=sem, vmem_limit_bytes=VMEM_LIMIT)


def _rms_bf16(x, nw):
    ms = jnp.mean(x * x, axis=-1, keepdims=True)
    return (x * lax.rsqrt(ms + EPS) * nw).astype(BF16)


def _qkv_proj_kernel(x_ref, nw_ref, w_ref, wg_ref, convw, tail_in, y_o, tail_o, g_o, xb0, xb1, xb2,
                     *, tiles_per_seq):
    xbufs = (xb0, xb1, xb2)

    def cols(g):
        return slice(g * W_B, (g + 1) * W_B)

    @pl.when(pl.program_id(0) % tiles_per_seq == 0)
    def _():
        for g in range(3):
            xbufs[g][0:8, :] = tail_in[:, cols(g)]

    h = _rms_bf16(x_ref[...], nw_ref[...])

    unit = 2 * DK_B
    n_units = 3 * W_B // unit

    def where(u):
        g, lo = divmod(u * unit, W_B)
        return g, slice(lo, lo + unit), slice(u * unit, (u + 1) * unit)

    def project(u):
        g, cg, ca = where(u)
        xbufs[g][8:8 + TILE, cg] = jnp.dot(h, w_ref[:, ca], preferred_element_type=F32)

    def gate(u):
        _, _, ca = where(u)
        g_o[:, ca] = jnp.dot(h, wg_ref[:, ca], preferred_element_type=F32)

    def epilogue(u):
        g, cg, ca = where(u)
        xbuf = xbufs[g]
        y = convw[CONV_K - 1:CONV_K, ca] * xbuf[8:8 + TILE, cg]
        for i in range(CONV_K - 1):
            off = 8 - (CONV_K - 1) + i
            y = y + convw[i:i + 1, ca] * xbuf[off:off + TILE, cg]
        y = y * jax.nn.sigmoid(y)
        tail = xbuf[TILE:TILE + 8, cg]
        tail_o[:, ca] = tail
        xbuf[0:8, cg] = tail
        if g == 2:
            y_o[:, ca] = y
        else:
            scale = DK_B ** -0.5 if g == 0 else 1.0
            for hh in range(unit // DK_B):
                yh = y[:, hh * DK_B:(hh + 1) * DK_B]
                inv = lax.rsqrt(jnp.sum(yh * yh, axis=-1, keepdims=True) + EPS) * scale
                y_o[:, u * unit + hh * DK_B:u * unit + (hh + 1) * DK_B] = yh * inv

    project(0)
    for u in range(n_units):
        gate(u)
        if u + 1 < n_units:
            project(u + 1)
        epilogue(u)


def _qkv_proj(x2d, norm_w_row, w_bf16, wg_bf16, convw, tail_in, tiles_per_seq):
    m = x2d.shape[0]
    n = 3 * W_B
    once = pl.Buffered(1)
    return pl.pallas_call(
        functools.partial(_qkv_proj_kernel, tiles_per_seq=tiles_per_seq),
        grid=(m // TILE,),
        in_specs=[pl.BlockSpec((TILE, D_MODEL), lambda i: (i, 0)),
                  pl.BlockSpec((1, D_MODEL), lambda i: (0, 0)),
                  pl.BlockSpec((D_MODEL, n), lambda i: (0, 0), pipeline_mode=once),
                  pl.BlockSpec((D_MODEL, n), lambda i: (0, 0), pipeline_mode=once),
                  pl.BlockSpec((CONV_K, n), lambda i: (0, 0)),
                  pl.BlockSpec((8, n), lambda i: (0, 0))],
        out_specs=[pl.BlockSpec((TILE, n), lambda i: (i, 0)),
                   pl.BlockSpec((8, n), lambda i: (i, 0)),
                   pl.BlockSpec((TILE, n), lambda i: (i, 0))],
        out_shape=[jax.ShapeDtypeStruct((m, n), F32),
                   jax.ShapeDtypeStruct((m // TILE * 8, n), F32),
                   jax.ShapeDtypeStruct((m, n), F32)],
        scratch_shapes=[pltpu.VMEM((8 + TILE, W_B), F32)] * 3,
        compiler_params=_cparams(("arbitrary",)),
        name="qkv_proj",
    )(x2d, norm_w_row, w_bf16, wg_bf16, convw, tail_in)


def _head_rms(x, bd, w_row):
    ss = jnp.dot((x * x).astype(BF16), bd, preferred_element_type=F32)
    return x * lax.rsqrt(ss * (1.0 / DH_A) + EPS) * w_row


def _attn_proj_kernel(x_ref, xm_ref, nw_ref, w_ref, qnw, knw, bd_ref,
                      za_o, sm_o, k_o, vt_o, ki_o, qtz_o, qit_o, wt_o):
    x = jnp.where(pl.program_id(1) == 0, xm_ref[...], x_ref[...])
    h = _rms_bf16(x, nw_ref[...])
    bd = bd_ref[...]

    def proj(c0, width):
        return jnp.dot(h, w_ref[:, c0:c0 + width], preferred_element_type=F32)

    sm = proj(A_S, 128)
    qi = proj(A_QI, W_A)
    sm_o[...] = sm
    ki_o[...] = sm[:, S_KI:S_KI + D_IDX].astype(BF16)
    wt_o[...] = sm.T[S_WI:S_WI + H_IDX, :] * ((H_IDX ** -0.5) * (D_IDX ** -0.5))

    q = proj(A_Q, W_A)
    for j in range(W_A // 128):
        qit_o[j * 128:(j + 1) * 128, :] = qi[:, j * 128:(j + 1) * 128].T.astype(BF16)

    k = proj(A_K, W_A)
    qn = _head_rms(q, bd, qnw[...]) * (DH_A ** -0.5 * LOG2E)
    zeros = jnp.zeros((DH_A, TILE), BF16)
    for j in range(W_A // 128):
        t = qn[:, j * 128:(j + 1) * 128].T.astype(BF16)
        base = 2 * j * 128
        qtz_o[base:base + 64, :] = t[0:64]
        qtz_o[base + 64:base + 128, :] = zeros
        qtz_o[base + 128:base + 192, :] = zeros
        qtz_o[base + 192:base + 256, :] = t[64:128]

    v = proj(A_V, W_A)
    k_o[...] = _head_rms(k, bd, knw[...]).astype(BF16)

    za = proj(A_Z, W_A)
    ones_rows = jnp.where(lax.broadcasted_iota(I32, (VROWS - DH_A, TILE), 0) == 0, 1.0, 0.0).astype(BF16)
    for j in range(W_A // 128):
        t = v[:, j * 128:(j + 1) * 128].T.astype(BF16)
        for r in range(2):
            base = (2 * j + r) * VROWS
            vt_o[base:base + DH_A, :] = t[r * DH_A:(r + 1) * DH_A]
            vt_o[base + DH_A:base + VROWS, :] = ones_rows
    za_o[...] = za


def _attn_proj(x, meta_pad, norm_w_row, w_bf16, qnw_row, knw_row, bd):
    b, t, _ = x.shape
    nt = t // TILE + 1
    tk = nt * TILE

    def rows(width, dtype):
        return (pl.BlockSpec((None, TILE, width), lambda bi, i: (bi, i, 0)),
                jax.ShapeDtypeStruct((b, tk, width), dtype))

    def cols(height, dtype):
        return (pl.BlockSpec((None, height, TILE), lambda bi, i: (bi, 0, i)),
                jax.ShapeDtypeStruct((b, height, tk), dtype))

    outs = [rows(W_A, F32), rows(128, F32), rows(W_A, BF16), cols(H_A * VROWS, BF16), rows(D_IDX, BF16),
            cols(2 * W_A, BF16), cols(W_A, BF16), cols(H_IDX, F32)]
    const = lambda bi, i: (0, 0)
    return pl.pallas_call(
        _attn_proj_kernel,
        grid=(b, nt),
        in_specs=[pl.BlockSpec((None, TILE, D_MODEL), lambda bi, i: (bi, jnp.maximum(i - 1, 0), 0)),
                  pl.BlockSpec((TILE, D_MODEL), const),
                  pl.BlockSpec((1, D_MODEL), const),
                  pl.BlockSpec((D_MODEL, N_ATTN), const),
                  pl.BlockSpec((1, W_A), const),
                  pl.BlockSpec((1, W_A), const),
                  pl.BlockSpec((W_A, W_A), const)],
        out_specs=[o[0] for o in outs],
        out_shape=[o[1] for o in outs],
        compiler_params=_cparams(("arbitrary", "arbitrary")),
        name="attn_proj",
    )(x, meta_pad, norm_w_row, w_bf16, qnw_row, knw_row, bd)


def _bias_kernel(rb_ref, o_ref):
    a = lax.broadcasted_iota(I32, (NEAR, QB), 1)
    bk = lax.broadcasted_iota(I32, (NEAR, QB), 0)
    d = jnp.maximum(a - bk + (NEAR - QB), 0)
    max_exact = N_BUCKETS // 2
    ratio = jnp.maximum(d, max_exact).astype(F32) / max_exact
    large = max_exact + (jnp.log(ratio) / math.log(MAX_DISTANCE / max_exact)
                         * (N_BUCKETS - max_exact)).astype(I32)
    large = jnp.minimum(large, N_BUCKETS - 1)
    bucket = jnp.where(d < max_exact, d, large)
    for h in range(H_A):
        acc = jnp.zeros((NEAR, QB), F32)
        for bb in range(N_BUCKETS):
            acc = jnp.where(bucket == bb, rb_ref[bb, h], acc)
        o_ref[h] = (acc - rb_ref[N_BUCKETS - 1, h]) * LOG2E


def _bias_tile(rel_bias):
    return pl.pallas_call(
        _bias_kernel,
        in_specs=[pl.BlockSpec(memory_space=pltpu.SMEM)],
        out_specs=pl.BlockSpec((H_A, NEAR, QB), lambda: (0, 0, 0)),
        out_shape=jax.ShapeDtypeStruct((H_A, NEAR, QB), F32),
        name="bias_tile",
    )(rel_bias)


def _attn_kernel(qtz_ref, qit_ref, wt_ref, qit_nx, wt_nx, k_ref, vt_ref, ki_ref, bias_ref, o_ref,
                 keys2_ref, hi_ref, lo_ref, m_ref, acc_ref, j_ref):
    iq = pl.program_id(1)
    nq = pl.num_programs(1)
    keys_ref = keys2_ref.at[iq % 2]
    keys_nx = keys2_ref.at[(iq + 1) % 2]
    near0 = pl.multiple_of(FAR0 + iq * QB, 128)
    q_pos = TILE + iq * QB + lax.broadcasted_iota(I32, (1, QB), 1)

    def score_chunk(kref, qit, wt, start, size, mask_causal):
        kic = ki_ref[pl.ds(start, size), :]
        acc = jnp.zeros((size, QB), F32)
        for h in range(H_IDX):
            r = jnp.dot(kic, qit[h * D_IDX:(h + 1) * D_IDX, :], preferred_element_type=F32)
            acc = acc + jnp.maximum(r, 0.0) * wt[h:h + 1, :]
        acc = acc + 0.0
        bits = pltpu.bitcast(acc, I32)
        key = jnp.where(bits < 0, bits ^ jnp.int32(0x7FFFFFFF), bits)
        if mask_causal:
            pos = start + lax.broadcasted_iota(I32, (size, QB), 0)
            key = jnp.where(pos <= q_pos, key, INT_MIN)
        kref[pl.ds(start, size), :] = key
        hi_ref[pl.ds(start, size), :] = lax.shift_right_arithmetic(key, 16).astype(I16)
        lo_ref[pl.ds(start, size), :] = ((key & 0xFFFF) - 2 ** 15).astype(I16)

    n_big = iq // 2
    n_prev = jnp.maximum(iq - 1, 0) // 2
    odd0 = pl.multiple_of(FAR0 + n_big * 2 * KC, 128)

    def big_start(c):
        return pl.multiple_of(FAR0 + c * 2 * KC, 128)

    def far_scores(c, carry):
        score_chunk(keys_ref, qit_ref, wt_ref, big_start(c), 2 * KC, False)
        return carry

    lax.fori_loop(n_prev, n_big, far_scores, 0)

    @pl.when(iq % 2 == 1)
    def _():
        score_chunk(keys_ref, qit_ref, wt_ref, odd0, KC, False)

    score_chunk(keys_ref, qit_ref, wt_ref, near0, NEAR, True)
    keys_ref[FAR0:PAD_END, :] = jnp.full((PAD_END - FAR0, QB), INT_MIN, I32)
    hi_ref[FAR0:PAD_END, :] = jnp.full((PAD_END - FAR0, QB), -2 ** 15, I16)
    lo_ref[FAR0:PAD_END, :] = jnp.full((PAD_END - FAR0, QB), -2 ** 15, I16)

    n256 = iq + NEAR // 256
    tail = NEAR % 256

    def blk_start(c):
        return pl.multiple_of(FAR0 + c * 256, 128)

    def count(ref, pred):
        def part(start, size):
            o = pred(ref[pl.ds(start, size), :], start).reshape(size // 32, 32, QB)
            parts = [o[i] for i in range(size // 32)]
            while len(parts) > 1:
                parts = [parts[i] + parts[i + 1] for i in range(0, len(parts), 2)]
            return parts[0]

        def pair(c, acc):
            return acc + part(blk_start(2 * c), 512)

        def single(c, acc):
            return acc + part(blk_start(c), 256)
        n_pairs = n256 // 2
        acc = lax.fori_loop(0, n_pairs, pair, jnp.zeros((32, QB), ref.dtype))
        acc = lax.fori_loop(2 * n_pairs, n256, single, acc)
        if tail:
            acc = acc + part(blk_start(n256), tail)
        return jnp.sum(acc.astype(I32), axis=0, keepdims=True)

    one16 = jnp.int16(1)
    zero16 = jnp.int16(0)

    def half_step(ref):
        def step(it, t):
            cand = t + lax.shift_left(jnp.int32(1), 15 - it)
            c16 = cand.astype(I16)
            cnt = count(ref, lambda blk, start: jnp.where(blk >= c16, one16, zero16))
            return jnp.where(cnt >= TOPK, cand, t)
        return step

    half_min = jnp.full((1, QB), -2 ** 15, I32)
    tau_hi = lax.fori_loop(0, 16, half_step(hi_ref), half_min)
    th16 = tau_hi.astype(I16)

    def fold_low(start, size):
        hi = hi_ref[pl.ds(start, size), :]
        lo = lo_ref[pl.ds(start, size), :]
        lo_ref[pl.ds(start, size), :] = jnp.where(hi > th16, jnp.int16(2 ** 15 - 1),
                                                  jnp.where(hi < th16, jnp.int16(-2 ** 15), lo))

    def fold_body(c, carry):
        fold_low(blk_start(c), 256)
        return carry

    lax.fori_loop(0, n256, fold_body, 0)
    if tail:
        fold_low(blk_start(n256), tail)
    tau_lo = lax.fori_loop(0, 16, half_step(lo_ref), half_min)
    tau = lax.shift_left(tau_hi, 16) + (tau_lo + 2 ** 15)

    def count32(pred):
        return count(keys_ref, pred)

    cnt_ge = count32(lambda blk, start: jnp.where(blk >= tau, 1, 0))
    excess = jnp.where(tau > INT_MIN, jnp.where(cnt_ge > TOPK, 1, 0), 0).astype(I32)
    j_ref[...] = jnp.where(tau > INT_MIN, jnp.int32(2 ** 30), jnp.int32(-1))

    @pl.when(jnp.max(excess) > 0)
    def _():
        need = TOPK - count32(lambda blk, start: jnp.where(blk > tau, 1, 0))

        def pos_step(it, lo):
            cand = lo + lax.shift_left(jnp.int32(1), 12 - it)

            def pred(blk, start):
                pos = start + lax.broadcasted_iota(I32, blk.shape, 0)
                return jnp.where(pos < cand, jnp.where(blk == tau, 1, 0), 0)
            return jnp.where(count32(pred) < need, cand, lo)

        lo = lax.fori_loop(0, 13, pos_step, jnp.zeros((1, QB), I32))
        j_ref[...] = jnp.where(excess > 0, lo, j_ref[...])

    j_last = j_ref[...]

    m_ref[...] = jnp.full(m_ref.shape, NEG, F32)
    acc_ref[...] = jnp.zeros(acc_ref.shape, F32)

    def attend(start, size, near, mid=None):
        key = keys_ref[pl.ds(start, size), :]
        pos = start + lax.broadcasted_iota(I32, (size, QB), 0)
        tie = jnp.where(pos <= j_last, 0.0, NEG)
        madd = jnp.where(key > tau, 0.0, jnp.where(key == tau, tie, NEG))
        hs = range(H_A)
        s = [jnp.dot(k_ref[pl.ds(start, size), (h // 2) * 128:(h // 2 + 1) * 128],
                     qtz_ref[h * 128:(h + 1) * 128, :], preferred_element_type=F32) + madd for h in hs]
        if mid is not None:
            mid()
        if near:
            s = [s[h] + bias_ref[h] for h in hs]
        m_all = m_ref[...]
        m_new = [jnp.maximum(m_all[h:h + 1, :], jnp.max(s[h], axis=0, keepdims=True)) for h in hs]
        m_new_all = jnp.concatenate(m_new, axis=0)
        alpha_all = jnp.exp2(m_all - m_new_all)
        m_ref[...] = m_new_all
        p = [jnp.exp2(s[h] - m_new[h]).astype(BF16) for h in hs]
        pv = [jnp.dot(vt_ref[h * VROWS:(h + 1) * VROWS, pl.ds(start, size)], p[h],
                      preferred_element_type=F32) for h in hs]
        for h in hs:
            acc_ref[h * VROWS:(h + 1) * VROWS, :] = (alpha_all[h:h + 1, :] * acc_ref[h * VROWS:(h + 1) * VROWS, :]
                                                     + pv[h])

    def fused_attend(c, carry):
        start = big_start(c)
        attend(start, 2 * KC, False,
               mid=lambda: score_chunk(keys_nx, qit_nx, wt_nx, start, 2 * KC, False))
        return carry

    def far_attend(c, carry):
        attend(big_start(c), 2 * KC, False)
        return carry

    n_fused = jnp.where(iq + 1 < nq, n_big, 0)
    lax.fori_loop(0, n_fused, fused_attend, 0)
    lax.fori_loop(n_fused, n_big, far_attend, 0)

    @pl.when(iq % 2 == 1)
    def _():
        attend(odd0, KC, False)

    attend(near0, NEAR, True)

    for j in range(W_A // 128):
        parts = []
        for h in (2 * j, 2 * j + 1):
            inv = 1.0 / acc_ref[h * VROWS + DH_A:h * VROWS + DH_A + 1, :]
            parts.append(acc_ref[h * VROWS:h * VROWS + DH_A, :] * inv)
        o_ref[:, j * 128:(j + 1) * 128] = jnp.concatenate(parts, axis=0).T


def _attention(qtz, qit, wt, k, vt, ki, bias, t):
    b, tk, _ = k.shape
    qoff = TILE // QB
    nq = t // QB

    def nxt(bi, i):
        return (bi, 0, jnp.minimum(i + 1, nq - 1) + qoff)

    return pl.pallas_call(
        _attn_kernel,
        grid=(b, nq),
        in_specs=[pl.BlockSpec((None, 2 * W_A, QB), lambda bi, i: (bi, 0, i + qoff)),
                  pl.BlockSpec((None, W_A, QB), lambda bi, i: (bi, 0, i + qoff)),
                  pl.BlockSpec((None, H_IDX, QB), lambda bi, i: (bi, 0, i + qoff)),
                  pl.BlockSpec((None, W_A, QB), nxt),
                  pl.BlockSpec((None, H_IDX, QB), nxt),
                  pl.BlockSpec((None, tk, W_A), lambda bi, i: (bi, 0, 0)),
                  pl.BlockSpec((None, H_A * VROWS, tk), lambda bi, i: (bi, 0, 0)),
                  pl.BlockSpec((None, tk, D_IDX), lambda bi, i: (bi, 0, 0)),
                  pl.BlockSpec((H_A, NEAR, QB), lambda bi, i: (0, 0, 0))],
        out_specs=pl.BlockSpec((None, QB, W_A), lambda bi, i: (bi, i, 0)),
        out_shape=jax.ShapeDtypeStruct((b, t, W_A), F32),
        scratch_shapes=[pltpu.VMEM((2, tk, QB), I32),
                        pltpu.VMEM((tk, QB), I16),
                        pltpu.VMEM((tk, QB), I16),
                        pltpu.VMEM((H_A, QB), F32),
                        pltpu.VMEM((H_A * VROWS, QB), F32),
                        pltpu.VMEM((1, QB), I32)],
        compiler_params=_cparams(("arbitrary", "arbitrary")),
        name="attn",
    )(qtz, qit, wt, qit, wt, k, vt, ki, bias)


def _dot_hi(a, b):
    return jnp.dot(a, b, precision=HI, preferred_element_type=F32)


def _mm(a, b):
    return jnp.dot(a.astype(BF16), b.astype(BF16), preferred_element_type=F32)


def _gates(sm, alog, dtb):
    z = sm + dtb
    softplus = jnp.maximum(z, 0.0) + jnp.log(1.0 + jnp.exp(-jnp.abs(z)))
    return -jnp.exp(alog) * softplus, jax.nn.sigmoid(sm)


def _gdn_local(y, g, beta, n_chunks):
    ri = lax.broadcasted_iota(I32, (CHUNK, CHUNK), 0)
    ci = lax.broadcasted_iota(I32, (CHUNK, CHUNK), 1)
    incl = ri >= ci
    strict = ri > ci
    tri = jnp.where(incl, 1.0, 0.0).astype(F32)
    tri_t = jnp.where(ri <= ci, 1.0, 0.0).astype(F32)
    eye = jnp.where(ri == ci, 1.0, 0.0).astype(F32)
    pairs = [(c, h) for c in range(n_chunks) for h in range(H_B)]
    idx = range(len(pairs))
    gc_col, gc_row = [], []
    for c in range(n_chunks):
        g_c = g[c * CHUNK:(c + 1) * CHUNK]
        gc_col.append(_dot_hi(tri, g_c))
        gc_row.append(_dot_hi(g_c.T, tri_t))
    q, kt, vb, qe, decay, kdt, glast = [], [], [], [], [], [], []
    for c, h in pairs:
        yc = y[c * CHUNK:(c + 1) * CHUNK]
        qh = yc[:, h * DK_B:(h + 1) * DK_B]
        kh = yc[:, W_B + h * DK_B:W_B + (h + 1) * DK_B]
        vh = yc[:, 2 * W_B + h * DV_B:2 * W_B + (h + 1) * DV_B]
        gcol = gc_col[c][:, S_AB + h:S_AB + h + 1]
        grow = gc_row[c][S_AB + h:S_AB + h + 1, :]
        bcol = beta[c * CHUNK:(c + 1) * CHUNK, S_BB + h:S_BB + h + 1]
        g_last = gcol[CHUNK - 1:CHUNK, :]
        egc = jnp.exp(gcol)
        kth = kh.T
        kb = kh * bcol
        q.append(jnp.concatenate([kb, qh], axis=0))
        kt.append(kth)
        vb.append(jnp.concatenate([vh * bcol, kb * egc], axis=1))
        qe.append(qh * egc)
        decay.append(jnp.exp(jnp.where(incl, gcol - grow, -jnp.inf)))
        kdt.append(kth * jnp.exp(g_last - grow))
        glast.append(jnp.exp(g_last))
    qk = [_mm(q[i], kt[i]) for i in idx]
    a = [jnp.where(strict, qk[i][0:CHUNK] * decay[i], 0.0) for i in idx]
    att = [qk[i][CHUNK:2 * CHUNK] * decay[i] for i in idx]
    tm = [eye - a[i] for i in idx]
    pw = [_mm(a[i], a[i]) for i in idx]
    for _ in range(4):
        r = [_mm(jnp.concatenate([tm[i], pw[i]], axis=0), pw[i]) for i in idx]
        tm = [tm[i] + r[i][0:CHUNK] for i in idx]
        pw = [r[i][CHUNK:2 * CHUNK] for i in idx]
    tm = [tm[i] + _mm(tm[i], pw[i]) for i in idx]
    uw = [_mm(tm[i], vb[i]) for i in idx]
    return uw, qe, att, kdt, glast


def _gdn_scan(local, c, s_list):
    uw, qe, att, kdt, glast = local
    base = c * H_B
    hs = range(H_B)
    ws = [_mm(jnp.concatenate([uw[base + h][:, DV_B:], qe[base + h]], axis=0), s_list[h]) for h in hs]
    v_new = [uw[base + h][:, :DV_B] - ws[h][0:CHUNK] for h in hs]
    ov = [_mm(jnp.concatenate([att[base + h], kdt[base + h]], axis=0), v_new[h]) for h in hs]
    outs = [ws[h][CHUNK:2 * CHUNK] + ov[h][0:CHUNK] for h in hs]
    s_out = [s_list[h] * glast[base + h] + ov[h][CHUNK:CHUNK + DK_B] for h in hs]
    return outs, s_out


def _gdn_meta_kernel(qkv_m, sm_m, alog, dtb, s_o):
    y = qkv_m[...]
    g, beta = _gates(sm_m[...], alog[...], dtb[...])
    live = lax.broadcasted_iota(I32, (CHUNK, 128), 0) >= CHUNK - N_META
    g = jnp.where(live, g, 0.0)
    beta = jnp.where(live, beta, 0.0)
    _, s_new = _gdn_scan(_gdn_local(y, g, beta, 1), 0, [jnp.zeros((DK_B, DV_B), F32)] * H_B)
    for h in range(H_B):
        s_o[h] = s_new[h]


def _gdn_kernel(qkv_x, zb_x, sm_x, s0, alog, dtb, gnw, o_ref, s_ref):
    @pl.when(pl.program_id(1) == 0)
    def _():
        s_ref[...] = s0[...]

    y = qkv_x[...]
    r = y.shape[0]
    g, beta = _gates(sm_x[...], alog[...], dtb[...])
    local = _gdn_local(y, g, beta, r // CHUNK)
    s_list = [s_ref[h] for h in range(H_B)]
    for c in range(r // CHUNK):
        rows = slice(c * CHUNK, (c + 1) * CHUNK)
        outs, s_list = _gdn_scan(local, c, s_list)
        for h in range(H_B):
            o = outs[h]
            on = o * lax.rsqrt(jnp.mean(o * o, axis=-1, keepdims=True) + EPS) * gnw[...]
            zb = zb_x[rows, h * DV_B:(h + 1) * DV_B]
            o_ref[rows, h * DV_B:(h + 1) * DV_B] = (on * (zb * jax.nn.sigmoid(zb))).astype(BF16)
    for h in range(H_B):
        s_ref[h] = s_list[h]


def _gdn(y_x, y_m, p_g, sm, alog_row, dtb_row, gnw_row):
    b, t, _ = y_x.shape
    r = CHUNK * GDN_CHUNKS_PER_STEP
    last = TILE // CHUNK - 1
    s0 = pl.pallas_call(
        _gdn_meta_kernel,
        grid=(1,),
        in_specs=[pl.BlockSpec((CHUNK, 3 * W_B), lambda i: (last, 0)),
                  pl.BlockSpec((None, CHUNK, 128), lambda i: (0, last, 0)),
                  pl.BlockSpec((1, 128), lambda i: (0, 0)),
                  pl.BlockSpec((1, 128), lambda i: (0, 0))],
        out_specs=pl.BlockSpec((H_B, DK_B, DV_B), lambda i: (0, 0, 0)),
        out_shape=jax.ShapeDtypeStruct((H_B, DK_B, DV_B), F32),
        compiler_params=_cparams(("arbitrary",)),
        name="gdn_meta",
    )(y_m, sm, alog_row, dtb_row)
    return pl.pallas_call(
        _gdn_kernel,
        grid=(b, t // r),
        in_specs=[pl.BlockSpec((None, r, 3 * W_B), lambda bi, n: (bi, n, 0)),
                  pl.BlockSpec((None, r, W_B), lambda bi, n: (bi, n, G_ZB)),
                  pl.BlockSpec((None, r, 128), lambda bi, n: (bi, n + TILE // r, 0)),
                  pl.BlockSpec((H_B, DK_B, DV_B), lambda bi, n: (0, 0, 0)),
                  pl.BlockSpec((1, 128), lambda bi, n: (0, 0)),
                  pl.BlockSpec((1, 128), lambda bi, n: (0, 0)),
                  pl.BlockSpec((1, DV_B), lambda bi, n: (0, 0))],
        out_specs=pl.BlockSpec((None, r, W_B), lambda bi, n: (bi, n, 0)),
        out_shape=jax.ShapeDtypeStruct((b, t, W_B), BF16),
        scratch_shapes=[pltpu.VMEM((H_B, DK_B, DV_B), F32)],
        compiler_params=_cparams(("arbitrary", "arbitrary")),
        name="gdn",
    )(y_x, p_g, sm, s0, alog_row, dtb_row, gnw_row)


def _merge_kernel(ya_ref, za_ref, og_ref, ga_ref, gb_ref, x_ref, wpa, wpb, wout, o_ref):
    za = za_ref[...]
    ya = (ya_ref[...] * (za * jax.nn.sigmoid(za))).astype(BF16)
    ya = jnp.dot(ya, wpa[...], preferred_element_type=F32)
    yb = jnp.dot(og_ref[...], wpb[...], preferred_element_type=F32)
    m = jax.nn.sigmoid(ga_ref[...]) * ya + jax.nn.sigmoid(gb_ref[...]) * yb
    o_ref[...] = x_ref[...] + jnp.dot(m.astype(BF16), wout[...], preferred_element_type=F32)


def _merge(ya, za, og, p_g, x, wpa, wpb, wout):
    b, t, _ = x.shape
    const = lambda bi, i: (0, 0)
    return pl.pallas_call(
        _merge_kernel,
        grid=(b, t // TILE),
        in_specs=[pl.BlockSpec((None, TILE, W_A), lambda bi, i: (bi, i, 0)),
                  pl.BlockSpec((None, TILE, W_A), lambda bi, i: (bi, i + 1, 0)),
                  pl.BlockSpec((None, TILE, W_B), lambda bi, i: (bi, i, 0)),
                  pl.BlockSpec((None, TILE, D_MODEL), lambda bi, i: (bi, i, G_GA)),
                  pl.BlockSpec((None, TILE, D_MODEL), lambda bi, i: (bi, i, G_GB)),
                  pl.BlockSpec((None, TILE, D_MODEL), lambda bi, i: (bi, i, 0)),
                  pl.BlockSpec((W_A, D_MODEL), const),
                  pl.BlockSpec((W_B, D_MODEL), const),
                  pl.BlockSpec((D_MODEL, D_MODEL), const)],
        out_specs=pl.BlockSpec((None, TILE, D_MODEL), lambda bi, i: (bi, i, 0)),
        out_shape=jax.ShapeDtypeStruct((b, t, D_MODEL), F32),
        compiler_params=_cparams(("arbitrary", "arbitrary")),
        name="merge",
    )(ya, za, og, p_g, p_g, x, wpa, wpb, wout)


def _permute_w_in(w):
    sizes = (W_A, W_A, W_A, W_A, H_IDX * D_IDX, D_IDX, H_IDX,
             H_B * DK_B, H_B * DK_B, W_B, W_B, H_B, H_B, D_MODEL, D_MODEL)
    offs = [0]
    for s in sizes:
        offs.append(offs[-1] + s)
    attn_end, qkv_end, zb_end, small_end = offs[7], offs[10], offs[11], offs[13]

    def run(lo, hi):
        return w[:, lo:hi].astype(BF16)

    pad = jnp.zeros((w.shape[0], 128 - (D_IDX + H_IDX + 2 * H_B)), BF16)
    return (run(attn_end, qkv_end),
            jnp.concatenate([run(qkv_end, zb_end), run(small_end, offs[15])], axis=1),
            jnp.concatenate([run(0, attn_end), run(zb_end, small_end), pad], axis=1))


def _lane_row(vals, offset):
    return jnp.zeros((1, 128), F32).at[0, offset:offset + vals.shape[0]].set(vals.astype(F32))


def _layer(x, meta_tokens, rel_bias, norm_w, w_in, q_norm_w, k_norm_w, conv_w, a_log, dt_bias,
           gdn_norm_w, w_proj_a, w_proj_b, w_out):
    b, t, _ = x.shape
    w_qkv, w_gate, w_attn = _permute_w_in(w_in)
    nw_row = norm_w.reshape(1, D_MODEL)
    meta_pad = jnp.zeros((TILE, D_MODEL), F32).at[PAD_END:].set(meta_tokens)
    x2d = x.reshape(b * t, D_MODEL)

    y_m, tail_m, _ = _qkv_proj(meta_pad, nw_row, w_qkv, w_gate, conv_w, jnp.zeros((8, 3 * W_B), F32), 1)
    y_x, _, p_g = _qkv_proj(x2d, nw_row, w_qkv, w_gate, conv_w, tail_m, t // TILE)
    p_g = p_g.reshape(b, t, 3 * W_B)

    hid = lax.broadcasted_iota(I32, (W_A, W_A), 0) // DH_A
    bd = (hid == hid.T).astype(BF16)
    za, sm, k_n, v_t, k_i, q_tz, q_it, w_t = _attn_proj(
        x, meta_pad, nw_row, w_attn, jnp.tile(q_norm_w, H_A).reshape(1, W_A),
        jnp.tile(k_norm_w, H_A).reshape(1, W_A), bd)
    ya = _attention(q_tz, q_it, w_t, k_n, v_t, k_i, _bias_tile(rel_bias), t)

    og = _gdn(y_x.reshape(b, t, 3 * W_B), y_m, p_g, sm, _lane_row(a_log, S_AB), _lane_row(dt_bias, S_AB),
              gdn_norm_w.reshape(1, DV_B))

    return _merge(ya, za, og, p_g, x, w_proj_a.astype(BF16), w_proj_b.astype(BF16), w_out.astype(BF16))


def kernel(x, meta_tokens, rel_bias, norm_w, w_in, q_norm_w, k_norm_w, conv_w, a_log, dt_bias,
           gdn_norm_w, w_proj_a, w_proj_b, w_out):
    depth = norm_w.shape[0]
    assert depth == 1, "meta rows are dropped after the layer; deeper stacks need them carried"
    return _layer(x, meta_tokens, rel_bias, norm_w[0], w_in[0], q_norm_w[0], k_norm_w[0], conv_w[0],
                  a_log[0], dt_bias[0], gdn_norm_w[0], w_proj_a[0], w_proj_b[0], w_out[0])
```

```python
import functools
import math

import jax
import jax.numpy as jnp
from jax import lax
from jax.experimental import pallas as pl
from jax.experimental.pallas import tpu as pltpu

F32 = jnp.float32
BF16 = jnp.bfloat16
I32 = jnp.int32
I16 = jnp.int16

D_MODEL = 1024
N_META = 16
H_A = 8
DH_A = 64
W_A = H_A * DH_A
H_IDX = 8
D_IDX = 64
TOPK = 256
N_BUCKETS = 32
MAX_DISTANCE = 128
H_B = 8
DK_B = 128
DV_B = 128
W_B = H_B * DV_B
CONV_K = 4
CHUNK = 64
EPS = 1e-6

G_ZB, G_GA, G_GB = 0, 1, 2
A_Q, A_K, A_V, A_Z, A_QI, A_S = 0, 512, 1024, 1536, 2048, 2560
N_ATTN = A_S + 128
S_KI, S_WI, S_BB, S_AB = 0, 64, 72, 80

TILE = 512
QB = 256
KC = 256
NEAR = QB + 128
FAR0 = TILE - 128
PAD_END = TILE - N_META
VROWS = DH_A + 16
LOG2E = math.log2(math.e)
INT_MIN = -2 ** 31
NEG = -1e30
VMEM_LIMIT = 58 * 1024 * 1024
HI = lax.Precision.HIGHEST
GDN_CHUNKS_PER_STEP = 4
HEAD_GROUP = 8


def _cparams(sem):
    return pltpu.CompilerParams(dimension_semantics=sem, vmem_limit_bytes=VMEM_LIMIT)


def _rms_bf16(x, nw):
    ms = jnp.mean(x * x, axis=-1, keepdims=True)
    return (x * lax.rsqrt(ms + EPS) * nw).astype(BF16)


def _qkv_proj_kernel(x_ref, nw_ref, w_ref, wg_ref, convw, tail_in, y_o, tail_o, g_o, xb0, xb1, xb2,
                     *, tiles_per_seq):
    xbufs = (xb0, xb1, xb2)

    def cols(g):
        return slice(g * W_B, (g + 1) * W_B)

    @pl.when(pl.program_id(0) % tiles_per_seq == 0)
    def _():
        for g in range(3):
            xbufs[g][0:8, :] = tail_in[:, cols(g)]

    h = _rms_bf16(x_ref[...], nw_ref[...])

    unit = 2 * DK_B
    n_units = 3 * W_B // unit

    def where(u):
        g, lo = divmod(u * unit, W_B)
        return g, slice(lo, lo + unit), slice(u * unit, (u + 1) * unit)

    def project(u):
        g, cg, ca = where(u)
        xbufs[g][8:8 + TILE, cg] = jnp.dot(h, w_ref[:, ca], preferred_element_type=F32)

    def gate(u):
        _, _, ca = where(u)
        g_o[:, ca] = jnp.dot(h, wg_ref[:, ca], preferred_element_type=F32)

    def epilogue(u):
        g, cg, ca = where(u)
        xbuf = xbufs[g]
        y = convw[CONV_K - 1:CONV_K, ca] * xbuf[8:8 + TILE, cg]
        for i in range(CONV_K - 1):
            off = 8 - (CONV_K - 1) + i
            y = y + convw[i:i + 1, ca] * xbuf[off:off + TILE, cg]
        y = y * jax.nn.sigmoid(y)
        tail = xbuf[TILE:TILE + 8, cg]
        tail_o[:, ca] = tail
        xbuf[0:8, cg] = tail
        if g == 2:
            y_o[:, ca] = y
        else:
            scale = DK_B ** -0.5 if g == 0 else 1.0
            for hh in range(unit // DK_B):
                yh = y[:, hh * DK_B:(hh + 1) * DK_B]
                inv = lax.rsqrt(jnp.sum(yh * yh, axis=-1, keepdims=True) + EPS) * scale
                y_o[:, u * unit + hh * DK_B:u * unit + (hh + 1) * DK_B] = yh * inv

    project(0)
    for u in range(n_units):
        gate(u)
        if u + 1 < n_units:
            project(u + 1)
        epilogue(u)


def _qkv_proj(x2d, norm_w_row, w_bf16, wg_bf16, convw, tail_in, tiles_per_seq):
    m = x2d.shape[0]
    n = 3 * W_B
    once = pl.Buffered(1)
    return pl.pallas_call(
        functools.partial(_qkv_proj_kernel, tiles_per_seq=tiles_per_seq),
        grid=(m // TILE,),
        in_specs=[pl.BlockSpec((TILE, D_MODEL), lambda i: (i, 0)),
                  pl.BlockSpec((1, D_MODEL), lambda i: (0, 0)),
                  pl.BlockSpec((D_MODEL, n), lambda i: (0, 0), pipeline_mode=once),
                  pl.BlockSpec((D_MODEL, n), lambda i: (0, 0), pipeline_mode=once),
                  pl.BlockSpec((CONV_K, n), lambda i: (0, 0)),
                  pl.BlockSpec((8, n), lambda i: (0, 0))],
        out_specs=[pl.BlockSpec((TILE, n), lambda i: (i, 0)),
                   pl.BlockSpec((8, n), lambda i: (i, 0)),
                   pl.BlockSpec((TILE, n), lambda i: (i, 0))],
        out_shape=[jax.ShapeDtypeStruct((m, n), F32),
                   jax.ShapeDtypeStruct((m // TILE * 8, n), F32),
                   jax.ShapeDtypeStruct((m, n), F32)],
        scratch_shapes=[pltpu.VMEM((8 + TILE, W_B), F32)] * 3,
        compiler_params=_cparams(("arbitrary",)),
        name="qkv_proj",
    )(x2d, norm_w_row, w_bf16, wg_bf16, convw, tail_in)


def _head_rms(x, bd, w_row):
    ss = jnp.dot((x * x).astype(BF16), bd, preferred_element_type=F32)
    return x * lax.rsqrt(ss * (1.0 / DH_A) + EPS) * w_row


def _attn_proj_kernel(x_ref, xm_ref, nw_ref, w_ref, qnw, knw, bd_ref,
                      za_o, sm_o, k_o, vt_o, ki_o, qtz_o, qit_o, wt_o):
    x = jnp.where(pl.program_id(1) == 0, xm_ref[...], x_ref[...])
    h = _rms_bf16(x, nw_ref[...])
    bd = bd_ref[...]

    def proj(c0, width):
        return jnp.dot(h, w_ref[:, c0:c0 + width], preferred_element_type=F32)

    sm = proj(A_S, 128)
    qi = proj(A_QI, W_A)
    sm_o[...] = sm
    ki_o[...] = sm[:, S_KI:S_KI + D_IDX].astype(BF16)
    wt_o[...] = sm.T[S_WI:S_WI + H_IDX, :] * ((H_IDX ** -0.5) * (D_IDX ** -0.5))

    q = proj(A_Q, W_A)
    for j in range(W_A // 128):
        qit_o[j * 128:(j + 1) * 128, :] = qi[:, j * 128:(j + 1) * 128].T.astype(BF16)

    k = proj(A_K, W_A)
    qn = _head_rms(q, bd, qnw[...]) * (DH_A ** -0.5 * LOG2E)
    zeros = jnp.zeros((DH_A, TILE), BF16)
    for j in range(W_A // 128):
        t = qn[:, j * 128:(j + 1) * 128].T.astype(BF16)
        base = 2 * j * 128
        qtz_o[base:base + 64, :] = t[0:64]
        qtz_o[base + 64:base + 128, :] = zeros
        qtz_o[base + 128:base + 192, :] = zeros
        qtz_o[base + 192:base + 256, :] = t[64:128]

    v = proj(A_V, W_A)
    k_o[...] = _head_rms(k, bd, knw[...]).astype(BF16)

    za = proj(A_Z, W_A)
    ones_rows = jnp.where(lax.broadcasted_iota(I32, (VROWS - DH_A, TILE), 0) == 0, 1.0, 0.0).astype(BF16)
    for j in range(W_A // 128):
        t = v[:, j * 128:(j + 1) * 128].T.astype(BF16)
        for r in range(2):
            base = (2 * j + r) * VROWS
            vt_o[base:base + DH_A, :] = t[r * DH_A:(r + 1) * DH_A]
            vt_o[base + DH_A:base + VROWS, :] = ones_rows
    za_o[...] = za


def _attn_proj(x, meta_pad, norm_w_row, w_bf16, qnw_row, knw_row, bd):
    b, t, _ = x.shape
    nt = t // TILE + 1
    tk = nt * TILE

    def rows(width, dtype):
        return (pl.BlockSpec((None, TILE, width), lambda bi, i: (bi, i, 0)),
                jax.ShapeDtypeStruct((b, tk, width), dtype))

    def cols(height, dtype):
        return (pl.BlockSpec((None, height, TILE), lambda bi, i: (bi, 0, i)),
                jax.ShapeDtypeStruct((b, height, tk), dtype))

    outs = [rows(W_A, F32), rows(128, F32), rows(W_A, BF16), cols(H_A * VROWS, BF16), rows(D_IDX, BF16),
            cols(2 * W_A, BF16), cols(W_A, BF16), cols(H_IDX, F32)]
    const = lambda bi, i: (0, 0)
    return pl.pallas_call(
        _attn_proj_kernel,
        grid=(b, nt),
        in_specs=[pl.BlockSpec((None, TILE, D_MODEL), lambda bi, i: (bi, jnp.maximum(i - 1, 0), 0)),
                  pl.BlockSpec((TILE, D_MODEL), const),
                  pl.BlockSpec((1, D_MODEL), const),
                  pl.BlockSpec((D_MODEL, N_ATTN), const),
                  pl.BlockSpec((1, W_A), const),
                  pl.BlockSpec((1, W_A), const),
                  pl.BlockSpec((W_A, W_A), const)],
        out_specs=[o[0] for o in outs],
        out_shape=[o[1] for o in outs],
        compiler_params=_cparams(("arbitrary", "arbitrary")),
        name="attn_proj",
    )(x, meta_pad, norm_w_row, w_bf16, qnw_row, knw_row, bd)


def _bias_kernel(rb_ref, o_ref):
    a = lax.broadcasted_iota(I32, (NEAR, QB), 1)
    bk = lax.broadcasted_iota(I32, (NEAR, QB), 0)
    d = jnp.maximum(a - bk + (NEAR - QB), 0)
    max_exact = N_BUCKETS // 2
    ratio = jnp.maximum(d, max_exact).astype(F32) / max_exact
    large = max_exact + (jnp.log(ratio) / math.log(MAX_DISTANCE / max_exact)
                         * (N_BUCKETS - max_exact)).astype(I32)
    large = jnp.minimum(large, N_BUCKETS - 1)
    bucket = jnp.where(d < max_exact, d, large)
    for h in range(H_A):
        acc = jnp.zeros((NEAR, QB), F32)
        for bb in range(N_BUCKETS):
            acc = jnp.where(bucket == bb, rb_ref[bb, h], acc)
        o_ref[h] = (acc - rb_ref[N_BUCKETS - 1, h]) * LOG2E


def _bias_tile(rel_bias):
    return pl.pallas_call(
        _bias_kernel,
        in_specs=[pl.BlockSpec(memory_space=pltpu.SMEM)],
        out_specs=pl.BlockSpec((H_A, NEAR, QB), lambda: (0, 0, 0)),
        out_shape=jax.ShapeDtypeStruct((H_A, NEAR, QB), F32),
        name="bias_tile",
    )(rel_bias)


def _attn_kernel(qtz_ref, qit_ref, wt_ref, qit_nx, wt_nx, k_ref, vt_ref, ki_ref, bias_ref, o_ref,
                 keys2_ref, hi_ref, lo_ref, m_ref, acc_ref, j_ref):
    iq = pl.program_id(1)
    nq = pl.num_programs(1)
    keys_ref = keys2_ref.at[iq % 2]
    keys_nx = keys2_ref.at[(iq + 1) % 2]
    near0 = pl.multiple_of(FAR0 + iq * QB, 128)
    q_pos = TILE + iq * QB + lax.broadcasted_iota(I32, (1, QB), 1)

    def score_chunk(kref, qit, wt, start, size, mask_causal):
        kic = ki_ref[pl.ds(start, size), :]
        acc = jnp.zeros((size, QB), F32)
        for h in range(H_IDX):
            r = jnp.dot(kic, qit[h * D_IDX:(h + 1) * D_IDX, :], preferred_element_type=F32)
            acc = acc + jnp.maximum(r, 0.0) * wt[h:h + 1, :]
        acc = acc + 0.0
        bits = pltpu.bitcast(acc, I32)
        key = jnp.where(bits < 0, bits ^ jnp.int32(0x7FFFFFFF), bits)
        if mask_causal:
            pos = start + lax.broadcasted_iota(I32, (size, QB), 0)
            key = jnp.where(pos <= q_pos, key, INT_MIN)
        kref[pl.ds(start, size), :] = key
        hi_ref[pl.ds(start, size), :] = lax.shift_right_arithmetic(key, 16).astype(I16)
        lo_ref[pl.ds(start, size), :] = ((key & 0xFFFF) - 2 ** 15).astype(I16)

    n_big = iq // 2
    n_prev = jnp.maximum(iq - 1, 0) // 2
    odd0 = pl.multiple_of(FAR0 + n_big * 2 * KC, 128)

    def big_start(c):
        return pl.multiple_of(FAR0 + c * 2 * KC, 128)

    def far_scores(c, carry):
        score_chunk(keys_ref, qit_ref, wt_ref, big_start(c), 2 * KC, False)
        return carry

    lax.fori_loop(n_prev, n_big, far_scores, 0)

    @pl.when(iq % 2 == 1)
    def _():
        score_chunk(keys_ref, qit_ref, wt_ref, odd0, KC + NEAR, True)

    @pl.when(iq % 2 == 0)
    def _():
        score_chunk(keys_ref, qit_ref, wt_ref, near0, NEAR, True)
    keys_ref[FAR0:PAD_END, :] = jnp.full((PAD_END - FAR0, QB), INT_MIN, I32)
    hi_ref[FAR0:PAD_END, :] = jnp.full((PAD_END - FAR0, QB), -2 ** 15, I16)
    lo_ref[FAR0:PAD_END, :] = jnp.full((PAD_END - FAR0, QB), -2 ** 15, I16)

    n256 = iq + NEAR // 256
    tail = NEAR % 256

    def blk_start(c):
        return pl.multiple_of(FAR0 + c * 256, 128)

    def count(ref, pred):
        def part(start, size):
            o = pred(ref[pl.ds(start, size), :], start).reshape(size // 32, 32, QB)
            parts = [o[i] for i in range(size // 32)]
            while len(parts) > 1:
                parts = [parts[i] + parts[i + 1] for i in range(0, len(parts), 2)]
            return parts[0]

        def pair(c, acc):
            return acc + part(blk_start(2 * c), 512)

        def single(c, acc):
            return acc + part(blk_start(c), 256)
        n_pairs = n256 // 2
        acc = lax.fori_loop(0, n_pairs, pair, jnp.zeros((32, QB), ref.dtype))
        acc = lax.fori_loop(2 * n_pairs, n256, single, acc)
        if tail:
            acc = acc + part(blk_start(n256), tail)
        return jnp.sum(acc.astype(I32), axis=0, keepdims=True)

    one16 = jnp.int16(1)
    zero16 = jnp.int16(0)

    def half_step(ref):
        def step(it, t):
            cand = t + lax.shift_left(jnp.int32(1), 15 - it)
            c16 = cand.astype(I16)
            cnt = count(ref, lambda blk, start: jnp.where(blk >= c16, one16, zero16))
            return jnp.where(cnt >= TOPK, cand, t)
        return step

    half_min = jnp.full((1, QB), -2 ** 15, I32)
    tau_hi = lax.fori_loop(0, 16, half_step(hi_ref), half_min)
    th16 = tau_hi.astype(I16)

    def fold_low(start, size):
        hi = hi_ref[pl.ds(start, size), :]
        lo = lo_ref[pl.ds(start, size), :]
        lo_ref[pl.ds(start, size), :] = jnp.where(hi > th16, jnp.int16(2 ** 15 - 1),
                                                  jnp.where(hi < th16, jnp.int16(-2 ** 15), lo))

    def fold_body(c, carry):
        fold_low(blk_start(c), 256)
        return carry

    lax.fori_loop(0, n256, fold_body, 0)
    if tail:
        fold_low(blk_start(n256), tail)
    tau_lo = lax.fori_loop(0, 16, half_step(lo_ref), half_min)
    tau = lax.shift_left(tau_hi, 16) + (tau_lo + 2 ** 15)

    def count32(pred):
        return count(keys_ref, pred)

    cnt_ge = count32(lambda blk, start: jnp.where(blk >= tau, 1, 0))
    excess = jnp.where(tau > INT_MIN, jnp.where(cnt_ge > TOPK, 1, 0), 0).astype(I32)
    j_ref[...] = jnp.where(tau > INT_MIN, jnp.int32(2 ** 30), jnp.int32(-1))

    @pl.when(jnp.max(excess) > 0)
    def _():
        need = TOPK - count32(lambda blk, start: jnp.where(blk > tau, 1, 0))

        def pos_step(it, lo):
            cand = lo + lax.shift_left(jnp.int32(1), 12 - it)

            def pred(blk, start):
                pos = start + lax.broadcasted_iota(I32, blk.shape, 0)
                return jnp.where(pos < cand, jnp.where(blk == tau, 1, 0), 0)
            return jnp.where(count32(pred) < need, cand, lo)

        lo = lax.fori_loop(0, 13, pos_step, jnp.zeros((1, QB), I32))
        j_ref[...] = jnp.where(excess > 0, lo, j_ref[...])

    j_last = j_ref[...]

    m_ref[...] = jnp.full(m_ref.shape, NEG, F32)
    acc_ref[...] = jnp.zeros(acc_ref.shape, F32)

    def attend(start, size, near, mid=None):
        key = keys_ref[pl.ds(start, size), :]
        pos = start + lax.broadcasted_iota(I32, (size, QB), 0)
        tie = jnp.where(pos <= j_last, 0.0, NEG)
        madd = jnp.where(key > tau, 0.0, jnp.where(key == tau, tie, NEG))
        m_all = m_ref[...]
        m_rows = []
        for g0 in range(0, H_A, HEAD_GROUP):
            hs = range(g0, g0 + HEAD_GROUP)
            s = {h: jnp.dot(k_ref[pl.ds(start, size), (h // 2) * 128:(h // 2 + 1) * 128],
                            qtz_ref[h * 128:(h + 1) * 128, :], preferred_element_type=F32) + madd
                 for h in hs}
            if near is not None:
                s = {h: jnp.concatenate([s[h][:near], s[h][near:] + bias_ref[h]], axis=0) if near
                     else s[h] + bias_ref[h] for h in hs}
            m_new = {h: jnp.maximum(m_all[h:h + 1, :], jnp.max(s[h], axis=0, keepdims=True)) for h in hs}
            alpha = {h: jnp.exp2(m_all[h:h + 1, :] - m_new[h]) for h in hs}
            p = {h: jnp.exp2(s[h] - m_new[h]).astype(BF16) for h in hs}
            if mid is not None and g0 == 0:
                mid()
            pv = {h: jnp.dot(vt_ref[h * VROWS:(h + 1) * VROWS, pl.ds(start, size)], p[h],
                             preferred_element_type=F32) for h in hs}
            for h in hs:
                acc_ref[h * VROWS:(h + 1) * VROWS, :] = (alpha[h] * acc_ref[h * VROWS:(h + 1) * VROWS, :]
                                                         + pv[h])
                m_rows.append(m_new[h])
        m_ref[...] = jnp.concatenate(m_rows, axis=0)

    def fused_attend(c, carry):
        start = big_start(c)
        attend(start, 2 * KC, None,
               mid=lambda: score_chunk(keys_nx, qit_nx, wt_nx, start, 2 * KC, False))
        return carry

    def far_attend(c, carry):
        attend(big_start(c), 2 * KC, None)
        return carry

    n_fused = jnp.where(iq + 1 < nq, n_big, 0)
    lax.fori_loop(0, n_fused, fused_attend, 0)
    lax.fori_loop(n_fused, n_big, far_attend, 0)

    @pl.when(iq % 2 == 1)
    def _():
        attend(odd0, KC + NEAR, KC)

    @pl.when(iq % 2 == 0)
    def _():
        attend(near0, NEAR, 0)

    for j in range(W_A // 128):
        parts = []
        for h in (2 * j, 2 * j + 1):
            inv = 1.0 / acc_ref[h * VROWS + DH_A:h * VROWS + DH_A + 1, :]
            parts.append(acc_ref[h * VROWS:h * VROWS + DH_A, :] * inv)
        o_ref[:, j * 128:(j + 1) * 128] = jnp.concatenate(parts, axis=0).T


def _attention(qtz, qit, wt, k, vt, ki, bias, t):
    b, tk, _ = k.shape
    qoff = TILE // QB
    nq = t // QB

    def nxt(bi, i):
        return (bi, 0, jnp.minimum(i + 1, nq - 1) + qoff)

    return pl.pallas_call(
        _attn_kernel,
        grid=(b, nq),
        in_specs=[pl.BlockSpec((None, 2 * W_A, QB), lambda bi, i: (bi, 0, i + qoff)),
                  pl.BlockSpec((None, W_A, QB), lambda bi, i: (bi, 0, i + qoff)),
                  pl.BlockSpec((None, H_IDX, QB), lambda bi, i: (bi, 0, i + qoff)),
                  pl.BlockSpec((None, W_A, QB), nxt),
                  pl.BlockSpec((None, H_IDX, QB), nxt),
                  pl.BlockSpec((None, tk, W_A), lambda bi, i: (bi, 0, 0)),
                  pl.BlockSpec((None, H_A * VROWS, tk), lambda bi, i: (bi, 0, 0)),
                  pl.BlockSpec((None, tk, D_IDX), lambda bi, i: (bi, 0, 0)),
                  pl.BlockSpec((H_A, NEAR, QB), lambda bi, i: (0, 0, 0))],
        out_specs=pl.BlockSpec((None, QB, W_A), lambda bi, i: (bi, i, 0)),
        out_shape=jax.ShapeDtypeStruct((b, t, W_A), F32),
        scratch_shapes=[pltpu.VMEM((2, tk, QB), I32),
                        pltpu.VMEM((tk, QB), I16),
                        pltpu.VMEM((tk, QB), I16),
                        pltpu.VMEM((H_A, QB), F32),
                        pltpu.VMEM((H_A * VROWS, QB), F32),
                        pltpu.VMEM((1, QB), I32)],
        compiler_params=_cparams(("arbitrary", "arbitrary")),
        name="attn",
    )(qtz, qit, wt, qit, wt, k, vt, ki, bias)


def _dot_hi(a, b):
    return jnp.dot(a, b, precision=HI, preferred_element_type=F32)


def _mm(a, b):
    return jnp.dot(a.astype(BF16), b.astype(BF16), preferred_element_type=F32)


def _gates(sm, alog, dtb):
    z = sm + dtb
    softplus = jnp.maximum(z, 0.0) + jnp.log(1.0 + jnp.exp(-jnp.abs(z)))
    return -jnp.exp(alog) * softplus, jax.nn.sigmoid(sm)


def _gdn_local(y, g, beta, n_chunks):
    ri = lax.broadcasted_iota(I32, (CHUNK, CHUNK), 0)
    ci = lax.broadcasted_iota(I32, (CHUNK, CHUNK), 1)
    incl = ri >= ci
    strict = ri > ci
    tri = jnp.where(incl, 1.0, 0.0).astype(F32)
    tri_t = jnp.where(ri <= ci, 1.0, 0.0).astype(F32)
    eye = jnp.where(ri == ci, 1.0, 0.0).astype(F32)
    pairs = [(c, h) for c in range(n_chunks) for h in range(H_B)]
    idx = range(len(pairs))
    gc_col, gc_row = [], []
    for c in range(n_chunks):
        g_c = g[c * CHUNK:(c + 1) * CHUNK]
        gc_col.append(_dot_hi(tri, g_c))
        gc_row.append(_dot_hi(g_c.T, tri_t))
    q, kt, vb, qe, decay, kdt, glast = [], [], [], [], [], [], []
    for c, h in pairs:
        rows = slice(c * CHUNK, (c + 1) * CHUNK)
        qh = y[rows, h * DK_B:(h + 1) * DK_B]
        kh = y[rows, W_B + h * DK_B:W_B + (h + 1) * DK_B]
        vh = y[rows, 2 * W_B + h * DV_B:2 * W_B + (h + 1) * DV_B]
        gcol = gc_col[c][:, S_AB + h:S_AB + h + 1]
        grow = gc_row[c][S_AB + h:S_AB + h + 1, :]
        bcol = beta[c * CHUNK:(c + 1) * CHUNK, S_BB + h:S_BB + h + 1]
        g_last = gcol[CHUNK - 1:CHUNK, :]
        egc = jnp.exp(gcol)
        kth = kh.T
        kb = kh * bcol
        q.append(jnp.concatenate([kb, qh], axis=0))
        kt.append(kth)
        vb.append(jnp.concatenate([vh * bcol, kb * egc], axis=1))
        qe.append(qh * egc)
        decay.append(jnp.exp(jnp.where(incl, gcol - grow, -jnp.inf)))
        kdt.append(kth * jnp.exp(g_last - grow))
        glast.append(jnp.exp(g_last))
    qk = [_mm(q[i], kt[i]) for i in idx]
    a = [jnp.where(strict, qk[i][0:CHUNK] * decay[i], 0.0) for i in idx]
    att = [qk[i][CHUNK:2 * CHUNK] * decay[i] for i in idx]
    tm = [eye - a[i] for i in idx]
    pw = [_mm(a[i], a[i]) for i in idx]
    for _ in range(4):
        r = [_mm(jnp.concatenate([tm[i], pw[i]], axis=0), pw[i]) for i in idx]
        tm = [tm[i] + r[i][0:CHUNK] for i in idx]
        pw = [r[i][CHUNK:2 * CHUNK] for i in idx]
    tm = [tm[i] + _mm(tm[i], pw[i]) for i in idx]
    uw = [_mm(tm[i], vb[i]) for i in idx]
    return uw, qe, att, kdt, glast


def _gdn_scan(local, c, s_list):
    uw, qe, att, kdt, glast = local
    base = c * H_B
    hs = range(H_B)
    ws = [_mm(jnp.concatenate([uw[base + h][:, DV_B:], qe[base + h]], axis=0), s_list[h]) for h in hs]
    v_new = [uw[base + h][:, :DV_B] - ws[h][0:CHUNK] for h in hs]
    ov = [_mm(jnp.concatenate([att[base + h], kdt[base + h]], axis=0), v_new[h]) for h in hs]
    outs = [ws[h][CHUNK:2 * CHUNK] + ov[h][0:CHUNK] for h in hs]
    s_out = [s_list[h] * glast[base + h] + ov[h][CHUNK:CHUNK + DK_B] for h in hs]
    return outs, s_out


def _gdn_meta_kernel(qkv_m, sm_m, alog, dtb, s_o):
    y = qkv_m
    g, beta = _gates(sm_m[...], alog[...], dtb[...])
    live = lax.broadcasted_iota(I32, (CHUNK, 128), 0) >= CHUNK - N_META
    g = jnp.where(live, g, 0.0)
    beta = jnp.where(live, beta, 0.0)
    _, s_new = _gdn_scan(_gdn_local(y, g, beta, 1), 0, [jnp.zeros((DK_B, DV_B), F32)] * H_B)
    for h in range(H_B):
        s_o[h] = s_new[h]


def _gdn_kernel(qkv_x, zb_x, sm_x, s0, alog, dtb, gnw, o_ref, s_ref):
    @pl.when(pl.program_id(1) == 0)
    def _():
        s_ref[...] = s0[...]

    y = qkv_x
    r = y.shape[0]
    g, beta = _gates(sm_x[...], alog[...], dtb[...])
    local = _gdn_local(y, g, beta, r // CHUNK)
    s_list = [s_ref[h] for h in range(H_B)]
    for c in range(r // CHUNK):
        rows = slice(c * CHUNK, (c + 1) * CHUNK)
        outs, s_list = _gdn_scan(local, c, s_list)
        for h in range(H_B):
            o = outs[h]
            on = o * lax.rsqrt(jnp.mean(o * o, axis=-1, keepdims=True) + EPS) * gnw[...]
            zb = zb_x[rows, h * DV_B:(h + 1) * DV_B]
            o_ref[rows, h * DV_B:(h + 1) * DV_B] = (on * (zb * jax.nn.sigmoid(zb))).astype(BF16)
    for h in range(H_B):
        s_ref[h] = s_list[h]


def _gdn(y_x, y_m, p_g, sm, alog_row, dtb_row, gnw_row):
    b, t, _ = y_x.shape
    r = CHUNK * GDN_CHUNKS_PER_STEP
    last = TILE // CHUNK - 1
    s0 = pl.pallas_call(
        _gdn_meta_kernel,
        grid=(1,),
        in_specs=[pl.BlockSpec((CHUNK, 3 * W_B), lambda i: (last, 0)),
                  pl.BlockSpec((None, CHUNK, 128), lambda i: (0, last, 0)),
                  pl.BlockSpec((1, 128), lambda i: (0, 0)),
                  pl.BlockSpec((1, 128), lambda i: (0, 0))],
        out_specs=pl.BlockSpec((H_B, DK_B, DV_B), lambda i: (0, 0, 0)),
        out_shape=jax.ShapeDtypeStruct((H_B, DK_B, DV_B), F32),
        compiler_params=_cparams(("arbitrary",)),
        name="gdn_meta",
    )(y_m, sm, alog_row, dtb_row)
    return pl.pallas_call(
        _gdn_kernel,
        grid=(b, t // r),
        in_specs=[pl.BlockSpec((None, r, 3 * W_B), lambda bi, n: (bi, n, 0)),
                  pl.BlockSpec((None, r, W_B), lambda bi, n: (bi, n, G_ZB)),
                  pl.BlockSpec((None, r, 128), lambda bi, n: (bi, n + TILE // r, 0)),
                  pl.BlockSpec((H_B, DK_B, DV_B), lambda bi, n: (0, 0, 0)),
                  pl.BlockSpec((1, 128), lambda bi, n: (0, 0)),
                  pl.BlockSpec((1, 128), lambda bi, n: (0, 0)),
                  pl.BlockSpec((1, DV_B), lambda bi, n: (0, 0))],
        out_specs=pl.BlockSpec((None, r, W_B), lambda bi, n: (bi, n, 0)),
        out_shape=jax.ShapeDtypeStruct((b, t, W_B), BF16),
        scratch_shapes=[pltpu.VMEM((H_B, DK_B, DV_B), F32)],
        compiler_params=_cparams(("arbitrary", "arbitrary")),
        name="gdn",
    )(y_x, p_g, sm, s0, alog_row, dtb_row, gnw_row)


def _merge_kernel(ya_ref, za_ref, og_ref, ga_ref, gb_ref, x_ref, wpa, wpb, wout, o_ref):
    za = za_ref[...]
    ya = (ya_ref[...] * (za * jax.nn.sigmoid(za))).astype(BF16)
    ya = jnp.dot(ya, wpa[...], preferred_element_type=F32)
    yb = jnp.dot(og_ref[...], wpb[...], preferred_element_type=F32)
    m = jax.nn.sigmoid(ga_ref[...]) * ya + jax.nn.sigmoid(gb_ref[...]) * yb
    o_ref[...] = x_ref[...] + jnp.dot(m.astype(BF16), wout[...], preferred_element_type=F32)


def _merge(ya, za, og, p_g, x, wpa, wpb, wout):
    b, t, _ = x.shape
    const = lambda bi, i: (0, 0)
    return pl.pallas_call(
        _merge_kernel,
        grid=(b, t // TILE),
        in_specs=[pl.BlockSpec((None, TILE, W_A), lambda bi, i: (bi, i, 0)),
                  pl.BlockSpec((None, TILE, W_A), lambda bi, i: (bi, i + 1, 0)),
                  pl.BlockSpec((None, TILE, W_B), lambda bi, i: (bi, i, 0)),
                  pl.BlockSpec((None, TILE, D_MODEL), lambda bi, i: (bi, i, G_GA)),
                  pl.BlockSpec((None, TILE, D_MODEL), lambda bi, i: (bi, i, G_GB)),
                  pl.BlockSpec((None, TILE, D_MODEL), lambda bi, i: (bi, i, 0)),
                  pl.BlockSpec((W_A, D_MODEL), const),
                  pl.BlockSpec((W_B, D_MODEL), const),
                  pl.BlockSpec((D_MODEL, D_MODEL), const)],
        out_specs=pl.BlockSpec((None, TILE, D_MODEL), lambda bi, i: (bi, i, 0)),
        out_shape=jax.ShapeDtypeStruct((b, t, D_MODEL), F32),
        compiler_params=_cparams(("arbitrary", "arbitrary")),
        name="merge",
    )(ya, za, og, p_g, p_g, x, wpa, wpb, wout)


def _permute_w_in(w):
    sizes = (W_A, W_A, W_A, W_A, H_IDX * D_IDX, D_IDX, H_IDX,
             H_B * DK_B, H_B * DK_B, W_B, W_B, H_B, H_B, D_MODEL, D_MODEL)
    offs = [0]
    for s in sizes:
        offs.append(offs[-1] + s)
    attn_end, qkv_end, zb_end, small_end = offs[7], offs[10], offs[11], offs[13]

    def run(lo, hi):
        return w[:, lo:hi].astype(BF16)

    pad = jnp.zeros((w.shape[0], 128 - (D_IDX + H_IDX + 2 * H_B)), BF16)
    return (run(attn_end, qkv_end),
            jnp.concatenate([run(qkv_end, zb_end), run(small_end, offs[15])], axis=1),
            jnp.concatenate([run(0, attn_end), run(zb_end, small_end), pad], axis=1))


def _lane_row(vals, offset):
    return jnp.zeros((1, 128), F32).at[0, offset:offset + vals.shape[0]].set(vals.astype(F32))


def _layer(x, meta_tokens, rel_bias, norm_w, w_in, q_norm_w, k_norm_w, conv_w, a_log, dt_bias,
           gdn_norm_w, w_proj_a, w_proj_b, w_out):
    b, t, _ = x.shape
    w_qkv, w_gate, w_attn = _permute_w_in(w_in)
    nw_row = norm_w.reshape(1, D_MODEL)
    meta_pad = jnp.zeros((TILE, D_MODEL), F32).at[PAD_END:].set(meta_tokens)
    x2d = x.reshape(b * t, D_MODEL)

    y_m, tail_m, _ = _qkv_proj(meta_pad, nw_row, w_qkv, w_gate, conv_w, jnp.zeros((8, 3 * W_B), F32), 1)
    y_x, _, p_g = _qkv_proj(x2d, nw_row, w_qkv, w_gate, conv_w, tail_m, t // TILE)
    p_g = p_g.reshape(b, t, 3 * W_B)

    hid = lax.broadcasted_iota(I32, (W_A, W_A), 0) // DH_A
    bd = (hid == hid.T).astype(BF16)
    za, sm, k_n, v_t, k_i, q_tz, q_it, w_t = _attn_proj(
        x, meta_pad, nw_row, w_attn, jnp.tile(q_norm_w, H_A).reshape(1, W_A),
        jnp.tile(k_norm_w, H_A).reshape(1, W_A), bd)
    ya = _attention(q_tz, q_it, w_t, k_n, v_t, k_i, _bias_tile(rel_bias), t)

    og = _gdn(y_x.reshape(b, t, 3 * W_B), y_m, p_g, sm, _lane_row(a_log, S_AB), _lane_row(dt_bias, S_AB),
              gdn_norm_w.reshape(1, DV_B))

    return _merge(ya, za, og, p_g, x, w_proj_a.astype(BF16), w_proj_b.astype(BF16), w_out.astype(BF16))


def kernel(x, meta_tokens, rel_bias, norm_w, w_in, q_norm_w, k_norm_w, conv_w, a_log, dt_bias,
           gdn_norm_w, w_proj_a, w_proj_b, w_out):
    depth = norm_w.shape[0]
    assert depth == 1, "meta rows are dropped after the layer; deeper stacks need them carried"
    return _layer(x, meta_tokens, rel_bias, norm_w[0], w_in[0], q_norm_w[0], k_norm_w[0], conv_w[0],
                  a_log[0], dt_bias[0], gdn_norm_w[0], w_proj_a[0], w_proj_b[0], w_out[0])
```
